```python
import jax, jax.numpy as jnp
from jax import lax
import numpy as np

D_MODEL = 4096
BATCH = 4
SEQ = 2048
DEPTH = 4
DEC_BATCH = 8
DEC_SEQ = 8
PAST_LEN = 8192
PAGE_SIZE = 128

N_HEADS = 16
KV_HEADS = 4
HEAD_DIM = 128
GROUP = N_HEADS // KV_HEADS
ATTN_W = N_HEADS * HEAD_DIM
KV_W = KV_HEADS * HEAD_DIM
CMP_LEN = 32
CMP_STRIDE = 16
CMP_HID = HEAD_DIM
SEL_LEN = 64
SEL_TOP = 16
WINDOW = 512
SEL_QBLOCK = 32
WIN_QBLOCK = 128
GM_WIDTH = D_MODEL // 2
GM_GROUPS = 4
GM_CHUNK = 128
D_FF = 11008
IN_W = ATTN_W + 6 * KV_W + 3 * N_HEADS + 2 * GM_WIDTH + 2 * D_MODEL
EPS = 1e-6
NEG_INF = -1e30
FORCE = 1e9
SCALE = HEAD_DIM ** -0.5

kernel_name = "nsa_gmlp_macaron_hybrid_step"


def rmsnorm(x, g):
    xf = x.astype(jnp.float32)
    y = xf * lax.rsqrt(jnp.mean(xf * xf, axis=-1, keepdims=True) + EPS)
    return (y * g.astype(jnp.float32)).astype(x.dtype)


def half_ffn(h, g, w_gate, w_up, w_down):
    n = rmsnorm(h, g)
    return h + 0.5 * ((jax.nn.silu(n @ w_gate) * (n @ w_up)) @ w_down)


def mixer_inputs(h, g_norm, w_in):
    B, T, _ = h.shape
    z = rmsnorm(h, g_norm) @ w_in
    o1 = ATTN_W
    o2 = o1 + 2 * KV_W
    o3 = o2 + 2 * KV_W
    o4 = o3 + 2 * KV_W
    o5 = o4 + 3 * N_HEADS
    o6 = o5 + GM_WIDTH
    o7 = o6 + GM_WIDTH
    kv_shape = (B, T, 2, KV_HEADS, HEAD_DIM)
    q = z[..., :o1].reshape(B, T, KV_HEADS, GROUP, HEAD_DIM)
    kv_c = z[..., o1:o2].reshape(kv_shape)
    kv_s = z[..., o2:o3].reshape(kv_shape)
    kv_w = z[..., o3:o4].reshape(kv_shape)
    nsa_g = jax.nn.sigmoid(z[..., o4:o5]).reshape(B, T, 3, KV_HEADS, GROUP)
    u = jax.nn.gelu(z[..., o5:o6])
    v = jax.nn.gelu(z[..., o6:o7])
    merge_g = jax.nn.sigmoid(z[..., o7:]).reshape(B, T, 2, D_MODEL)
    return q, kv_c, kv_s, kv_w, nsa_g, u, v, merge_g


def gqa_attend(q, k, v, mask):
    s = jnp.einsum('nqgrd,nkgd->ngrqk', q, k, preferred_element_type=jnp.float32) * SCALE
    s = jnp.where(mask, s, NEG_INF)
    p = jnp.where(mask, jax.nn.softmax(s, axis=-1), 0.0)
    o = jnp.einsum('ngrqk,nkgd->nqgrd', p.astype(v.dtype), v)
    return o, p


def compress(rows, pe, w1, w2):
    L = rows.shape[1]
    n_cmp = (L - CMP_LEN) // CMP_STRIDE + 1
    idx = (jnp.arange(n_cmp) * CMP_STRIDE)[:, None] + jnp.arange(CMP_LEN)[None, :]
    blk = rows[:, idx] + pe[None, None, :, None, :]
    hid = jax.nn.gelu(jnp.einsum('bnlgd,lde->bnge', blk, w1))
    return hid @ w2


def cmp_branch(q, kv_rows, q_pos, pe, w1, w2):
    kc = compress(kv_rows[:, :, 0], pe[0], w1[0], w2[0])
    vc = compress(kv_rows[:, :, 1], pe[1], w1[1], w2[1])
    n_cmp = kc.shape[1]
    blk_end = jnp.arange(n_cmp) * CMP_STRIDE + CMP_LEN - 1
    mask = blk_end[None, :] <= q_pos[:, None]
    o, p = gqa_attend(q, kc, vc, mask)
    L = kv_rows.shape[1]
    n_sel = -(-L // SEL_LEN)
    per = SEL_LEN // CMP_STRIDE
    imp = p.sum(axis=2)
    imp = jnp.pad(imp, ((0, 0), (0, 0), (0, 0), (0, n_sel * per - n_cmp)))
    imp = imp.reshape(imp.shape[:3] + (n_sel, per)).sum(-1)
    return o, imp


def select_blocks(imp, q_pos):
    n_sel = imp.shape[-1]
    j = jnp.arange(n_sel)[None, :]
    cur = (q_pos // SEL_LEN)[:, None]
    forced = (j == 0) | (j == cur) | (j == cur - 1)
    score = jnp.where(forced, FORCE, imp)
    score = jnp.where(j <= cur, score, -1.0)
    vals, idx = lax.top_k(score, min(SEL_TOP, n_sel))
    return idx, vals >= 0.0


def sel_attend(q, kv_g, idx, valid, q_pos):
    B, Q = q.shape[:2]
    kpos = idx[..., None] * SEL_LEN + jnp.arange(SEL_LEN)
    mask = valid[..., None] & (kpos <= q_pos[None, None, :, None, None])
    s = jnp.einsum('bqgrd,bgqksd->bgrqks', q, kv_g[..., 0, :], preferred_element_type=jnp.float32) * SCALE
    s = s.reshape(B, KV_HEADS, GROUP, Q, -1)
    mask = mask.reshape(B, KV_HEADS, 1, Q, -1)
    p = jnp.where(mask, jax.nn.softmax(jnp.where(mask, s, NEG_INF), axis=-1), 0.0)
    vs = kv_g[..., 1, :].reshape(B, KV_HEADS, Q, -1, HEAD_DIM)
    return jnp.einsum('bgrqn,bgqnd->bqgrd', p.astype(vs.dtype), vs)


def sel_prompt(q, src, idx, valid, q_pos):
    B, T = q.shape[:2]
    nb = T // SEL_QBLOCK

    def qsplit(a, axis):
        a = a.reshape(a.shape[:axis] + (nb, SEL_QBLOCK) + a.shape[axis + 1:])
        return jnp.moveaxis(a, axis, 0)

    b_idx = jnp.arange(B)[:, None, None, None]
    g_idx = jnp.arange(KV_HEADS)[None, :, None, None]

    def one(args):
        qb, ib, vb, pb = args
        kv_g = src[b_idx, ib, :, :, g_idx]
        return sel_attend(qb, kv_g, ib, vb, pb)

    o = lax.map(one, (qsplit(q, 1), qsplit(idx, 2), qsplit(valid, 2), q_pos.reshape(nb, SEL_QBLOCK)))
    return jnp.moveaxis(o, 0, 1).reshape(q.shape)


def gather_sel_sample(pool_s, kv_new, page_table, idx):
    B, T = kv_new.shape[:2]
    per_page = PAGE_SIZE // SEL_LEN
    nb_past = page_table.shape[1] * per_page
    nb_new = -(-T // SEL_LEN)
    b_idx = jnp.arange(B)[:, None, None, None]
    g_idx = jnp.arange(KV_HEADS)[None, :, None, None]
    blocks = pool_s.reshape((pool_s.shape[0], per_page, SEL_LEN) + pool_s.shape[2:])
    jp = jnp.minimum(idx, nb_past - 1)
    phys = page_table[b_idx, jp // per_page]
    from_pool = blocks[phys, jp % per_page, :, :, g_idx]
    new_blocks = jnp.pad(kv_new, ((0, 0), (0, nb_new * SEL_LEN - T), (0, 0), (0, 0), (0, 0)))
    new_blocks = new_blocks.reshape(B, nb_new, SEL_LEN, 2, KV_HEADS, HEAD_DIM)
    from_new = new_blocks[b_idx, jnp.clip(idx - nb_past, 0, nb_new - 1), :, :, g_idx]
    return jnp.where((idx < nb_past)[..., None, None, None], from_pool, from_new)


def win_prompt(q, kv):
    B, T = q.shape[:2]
    nb = T // WIN_QBLOCK
    span = WINDOW + WIN_QBLOCK
    kvp = jnp.pad(kv, ((0, 0), (WINDOW, 0), (0, 0), (0, 0), (0, 0)))
    kidx = (jnp.arange(nb) * WIN_QBLOCK)[:, None] + jnp.arange(span)[None, :]
    kvb = kvp[:, kidx].reshape(B * nb, span, 2, KV_HEADS, HEAD_DIM)
    kpos = (kidx - WINDOW)[:, None, :]
    qpos = ((jnp.arange(nb) * WIN_QBLOCK)[:, None] + jnp.arange(WIN_QBLOCK)[None, :])[:, :, None]
    mask = (kpos <= qpos) & (kpos > qpos - WINDOW) & (kpos >= 0)
    mask = jnp.broadcast_to(mask[None], (B,) + mask.shape).reshape(B * nb, 1, 1, WIN_QBLOCK, span)
    qb = q.reshape(B * nb, WIN_QBLOCK, KV_HEADS, GROUP, HEAD_DIM)
    o, _ = gqa_attend(qb, kvb[:, :, 0], kvb[:, :, 1], mask)
    return o.reshape(q.shape)


def win_sample(q, kv_new, buf, past_len):
    Wb = buf.shape[1]
    T = q.shape[1]
    kv = jnp.concatenate([buf, kv_new], axis=1)
    kpos = past_len - Wb + jnp.arange(Wb + T)
    qpos = past_len + jnp.arange(T)
    mask = (kpos[None, :] <= qpos[:, None]) & (kpos[None, :] > qpos[:, None] - WINDOW)
    o, _ = gqa_attend(q, kv[:, :, 0], kv[:, :, 1], mask)
    return o, kv[:, -min(WINDOW, Wb + T):]


def gmlp_mix(u, v, w_s, b_s):
    B, T = u.shape[:2]
    cl = min(T, GM_CHUNK)
    w = jnp.tril(w_s[:, :cl, :cl])
    vc = v.reshape(B, T // cl, cl, GM_GROUPS, GM_WIDTH // GM_GROUPS)
    mixed = jnp.einsum('gij,bcjgd->bcigd', w, vc) + b_s[:, :cl].T[None, None, :, :, None]
    return u * mixed.reshape(B, T, GM_WIDTH)


def mixer_output(o_c, o_s, o_w, nsa_g, gm, merge_g, w_up_attn, w_up_gm, w_out):
    B, T = gm.shape[:2]
    o = (nsa_g[:, :, 0, ..., None] * o_c + nsa_g[:, :, 1, ..., None] * o_s
         + nsa_g[:, :, 2, ..., None] * o_w).reshape(B, T, ATTN_W)
    merged = merge_g[:, :, 0] * (o @ w_up_attn) + merge_g[:, :, 1] * (gm @ w_up_gm)
    return merged @ w_out


def prompt_layer(x, lw):
    (n1, g1, u1, d1, nm, w_in, pe, w1, w2, ws, bs, wua, wug, wo, n2, g2, u2, d2) = lw
    B, T, _ = x.shape
    h = half_ffn(x, n1, g1, u1, d1)
    q, kv_c, kv_s, kv_w, nsa_g, u, v, merge_g = mixer_inputs(h, nm, w_in)
    pos = jnp.arange(T)
    o_c, imp = cmp_branch(q, kv_c, pos, pe, w1, w2)
    idx, valid = select_blocks(imp, pos)
    src = kv_s.reshape(B, T // SEL_LEN, SEL_LEN, 2, KV_HEADS, HEAD_DIM)
    o_s = sel_prompt(q, src, idx, valid, pos)
    o_w = win_prompt(q, kv_w)
    gm = gmlp_mix(u, v, ws, bs)
    h = h + mixer_output(o_c, o_s, o_w, nsa_g, gm, merge_g, wua, wug, wo)
    h = half_ffn(h, n2, g2, u2, d2)
    return h, kv_c, kv_s, kv_w[:, -min(WINDOW, T):]


def sample_layer(x, pool_c, pool_s, win_buf, page_table, lw):
    (n1, g1, u1, d1, nm, w_in, pe, w1, w2, ws, bs, wua, wug, wo, n2, g2, u2, d2) = lw
    B, T, _ = x.shape
    past = page_table.shape[1] * PAGE_SIZE
    h = half_ffn(x, n1, g1, u1, d1)
    q, kv_c, kv_s, kv_w, nsa_g, u, v, merge_g = mixer_inputs(h, nm, w_in)
    pos = past + jnp.arange(T)
    past_c = pool_c[page_table].reshape((B, past) + pool_c.shape[2:])
    o_c, imp = cmp_branch(q, jnp.concatenate([past_c, kv_c], axis=1), pos, pe, w1, w2)
    idx, valid = select_blocks(imp, pos)
    o_s = sel_attend(q, gather_sel_sample(pool_s, kv_s, page_table, idx), idx, valid, pos)
    o_w, new_win = win_sample(q, kv_w, win_buf, past)
    gm = gmlp_mix(u, v, ws, bs)
    h = h + mixer_output(o_c, o_s, o_w, nsa_g, gm, merge_g, wua, wug, wo)
    h = half_ffn(h, n2, g2, u2, d2)
    return h, kv_c, kv_s, new_win, v


def setup_inputs(seed: int = 0) -> dict:
    key = jax.random.key(seed)
    keys = iter(jax.random.split(key, 32))

    def nrm(shape, scale):
        return jax.random.normal(next(keys), shape, jnp.float32) * scale

    n_pages = PAST_LEN // PAGE_SIZE
    n_used = DEC_BATCH * n_pages
    n_phys = n_used + max(1, n_used // 4)
    kv_row = (2, KV_HEADS, HEAD_DIM)
    return {
        "x_prompt": nrm((BATCH, SEQ, D_MODEL), 1.0),
        "x_sample": nrm((DEC_BATCH, DEC_SEQ, D_MODEL), 1.0),
        "cache_cmp_kv": nrm((DEPTH, n_phys, PAGE_SIZE) + kv_row, 1.0),
        "cache_sel_kv": nrm((DEPTH, n_phys, PAGE_SIZE) + kv_row, 1.0),
        "cache_win_kv": nrm((DEPTH, DEC_BATCH, min(WINDOW, PAST_LEN)) + kv_row, 1.0),
        "page_table": jax.random.permutation(next(keys), n_phys)[:n_used].reshape(DEC_BATCH, n_pages).astype(jnp.int32),
        "norm_ffn1": 1.0 + nrm((DEPTH, D_MODEL), 0.05),
        "ffn1_gate": nrm((DEPTH, D_MODEL, D_FF), D_MODEL ** -0.5),
        "ffn1_up": nrm((DEPTH, D_MODEL, D_FF), D_MODEL ** -0.5),
        "ffn1_down": nrm((DEPTH, D_FF, D_MODEL), D_FF ** -0.5),
        "norm_mix": 1.0 + nrm((DEPTH, D_MODEL), 0.05),
        "w_in": nrm((DEPTH, D_MODEL, IN_W), D_MODEL ** -0.5),
        "cmp_pe": nrm((DEPTH, 2, CMP_LEN, HEAD_DIM), 0.5),
        "cmp_w1": nrm((DEPTH, 2, CMP_LEN, HEAD_DIM, CMP_HID), (CMP_LEN * HEAD_DIM) ** -0.5),
        "cmp_w2": nrm((DEPTH, 2, CMP_HID, HEAD_DIM), CMP_HID ** -0.5),
        "gm_ws": nrm((DEPTH, GM_GROUPS, GM_CHUNK, GM_CHUNK), GM_CHUNK ** -0.5),
        "gm_bs": 1.0 + nrm((DEPTH, GM_GROUPS, GM_CHUNK), 0.05),
        "w_up_attn": nrm((DEPTH, ATTN_W, D_MODEL), ATTN_W ** -0.5),
        "w_up_gm": nrm((DEPTH, GM_WIDTH, D_MODEL), GM_WIDTH ** -0.5),
        "w_out": nrm((DEPTH, D_MODEL, D_MODEL), D_MODEL ** -0.5),
        "norm_ffn2": 1.0 + nrm((DEPTH, D_MODEL), 0.05),
        "ffn2_gate": nrm((DEPTH, D_MODEL, D_FF), D_MODEL ** -0.5),
        "ffn2_up": nrm((DEPTH, D_MODEL, D_FF), D_MODEL ** -0.5),
        "ffn2_down": nrm((DEPTH, D_FF, D_MODEL), D_FF ** -0.5),
        "norm_final": 1.0 + nrm((D_MODEL,), 0.05),
    }


def reference(x_prompt, x_sample, cache_cmp_kv, cache_sel_kv, cache_win_kv, page_table,
              norm_ffn1, ffn1_gate, ffn1_up, ffn1_down, norm_mix, w_in, cmp_pe, cmp_w1, cmp_w2,
              gm_ws, gm_bs, w_up_attn, w_up_gm, w_out, norm_ffn2, ffn2_gate, ffn2_up, ffn2_down,
              norm_final):
    hp, hs = x_prompt, x_sample
    cmp_p, cmp_s, sel_p, sel_s, win_p, win_s, gmv_s = [], [], [], [], [], [], []
    for l in range(DEPTH):
        lw = (norm_ffn1[l], ffn1_gate[l], ffn1_up[l], ffn1_down[l], norm_mix[l], w_in[l],
              cmp_pe[l], cmp_w1[l], cmp_w2[l], gm_ws[l], gm_bs[l], w_up_attn[l], w_up_gm[l],
              w_out[l], norm_ffn2[l], ffn2_gate[l], ffn2_up[l], ffn2_down[l])
        hp, kc_p, ks_p, kw_p = prompt_layer(hp, lw)
        hs, kc_s, ks_s, kw_s, v_s = sample_layer(hs, cache_cmp_kv[l], cache_sel_kv[l], cache_win_kv[l],
                                                 page_table, lw)
        cmp_p.append(kc_p)
        sel_p.append(ks_p)
        win_p.append(kw_p)
        cmp_s.append(kc_s)
        sel_s.append(ks_s)
        win_s.append(kw_s)
        gmv_s.append(v_s)
    y_prompt = rmsnorm(hp, norm_final)
    y_sample = rmsnorm(hs, norm_final)
    new_cmp_kv_prompt = jnp.stack(cmp_p)
    new_cmp_kv_sample = jnp.stack(cmp_s)
    new_sel_kv_prompt = jnp.stack(sel_p)
    new_sel_kv_sample = jnp.stack(sel_s)
    win_kv_prompt = jnp.stack(win_p)
    win_kv_sample = jnp.stack(win_s)
    gm_v_sample = jnp.stack(gmv_s)
    return (y_prompt, y_sample, new_cmp_kv_prompt, new_cmp_kv_sample, new_sel_kv_prompt,
            new_sel_kv_sample, win_kv_prompt, win_kv_sample, gm_v_sample)
```

```python
import functools

import jax
import jax.numpy as jnp
from jax import lax
from jax.experimental import pallas as pl
from jax.experimental.pallas import tpu as pltpu

D_MODEL = 4096
BATCH = 4
SEQ = 2048
DEPTH = 4
DEC_BATCH = 8
DEC_SEQ = 8
PAST_LEN = 8192
PAGE_SIZE = 128

N_HEADS = 16
KV_HEADS = 4
HEAD_DIM = 128
GROUP = N_HEADS // KV_HEADS
ATTN_W = N_HEADS * HEAD_DIM
KV_W = KV_HEADS * HEAD_DIM
CMP_LEN = 32
CMP_STRIDE = 16
SEL_LEN = 64
SEL_TOP = 16
WINDOW = 512
GM_WIDTH = D_MODEL // 2
GM_GROUPS = 4
GM_CHUNK = 128
D_FF = 11008
EPS = 1e-6
NEG_INF = -1e30
FORCE = 1e9
SCALE = HEAD_DIM ** -0.5

LANES = 128
SUBLANES = 8
VMEM_BUDGET = 56 * 1024 * 1024
CD = jnp.bfloat16
F32 = jnp.float32
PER_SEL = SEL_LEN // CMP_STRIDE

ROW_TILE = 1376
COL_TILE = 256
Q_TILE = 128


def _pick_tile(n, target, mult):
    best = n
    for t in range(mult, min(n, target) + 1, mult):
        if n % t == 0:
            best = t
    return best


def _params(sem, block_bytes):
    limit = min(VMEM_BUDGET, max(16 * 1024 * 1024, int(block_bytes)))
    return pltpu.CompilerParams(dimension_semantics=sem, vmem_limit_bytes=limit)


def _nbytes(shape, dtype):
    n = 1
    for s in shape:
        n *= s
    return n * jnp.dtype(dtype).itemsize


def _dot(a, b):
    return jnp.dot(a, b, preferred_element_type=F32)


def _dot_nt(a, b):
    return lax.dot_general(a, b, (((1,), (1,)), ((), ())), preferred_element_type=F32)


def _rms_kernel(x_ref, g_ref, o_ref):
    x = x_ref[...]
    y = x * lax.rsqrt(jnp.mean(x * x, axis=-1, keepdims=True) + EPS)
    o_ref[...] = (y * g_ref[...]).astype(o_ref.dtype)


def _rmsnorm(x, g, out_dtype, rows=None, row_off=0, tile=None):
    M, D = x.shape
    rows = M if rows is None else rows
    tm = tile or _pick_tile(rows, 344, 16)
    assert rows % tm == 0 and row_off % tm == 0
    off = row_off // tm
    blk = 2 * _nbytes((tm, D), F32) * 2 + 2 * _nbytes((tm, D), out_dtype)
    return pl.pallas_call(
        _rms_kernel,
        grid=(rows // tm,),
        in_specs=[pl.BlockSpec((tm, D), lambda i: (i + off, 0)),
                  pl.BlockSpec((1, D), lambda i: (0, 0))],
        out_specs=pl.BlockSpec((tm, D), lambda i: (i, 0)),
        out_shape=jax.ShapeDtypeStruct((rows, D), out_dtype),
        compiler_params=_params(("parallel",), blk),
        name="rmsnorm",
    )(x, g.reshape(1, D))


def _proj_kernel(x_ref, w_ref, o_ref, *, act):
    z = _dot(x_ref[...], w_ref[...])
    if act == "sigmoid":
        z = jax.nn.sigmoid(z)
    elif act == "gelu":
        z = jax.nn.gelu(z)
    o_ref[...] = z.astype(o_ref.dtype)


def _proj(x, w, act, out_dtype, name):
    M, K = x.shape
    N = w.shape[1]
    tm = _pick_tile(M, ROW_TILE, 16)
    tn = _pick_tile(N, COL_TILE, LANES)
    blk = 2 * (_nbytes((tm, K), x.dtype) + _nbytes((K, tn), w.dtype) + _nbytes((tm, tn), out_dtype)) \
        + 4 * _nbytes((tm, tn), F32)
    return pl.pallas_call(
        functools.partial(_proj_kernel, act=act),
        grid=(M // tm, N // tn),
        in_specs=[pl.BlockSpec((tm, K), lambda i, j: (i, 0)),
                  pl.BlockSpec((K, tn), lambda i, j: (0, j))],
        out_specs=pl.BlockSpec((tm, tn), lambda i, j: (i, j)),
        out_shape=jax.ShapeDtypeStruct((M, N), out_dtype),
        compiler_params=_params(("parallel", "parallel"), blk),
        name=name,
    )(x, w)


def _swiglu_kernel(x_ref, wg_ref, wu_ref, o_ref):
    x = x_ref[...]
    a = _dot(x, wg_ref[...])
    b = _dot(x, wu_ref[...])
    o_ref[...] = (jax.nn.silu(a) * b).astype(o_ref.dtype)


def _swiglu(x, wg, wu):
    M, K = x.shape
    N = wg.shape[1]
    tm = _pick_tile(M, ROW_TILE, 16)
    tn = _pick_tile(N, COL_TILE, LANES)
    blk = 2 * (_nbytes((tm, K), x.dtype) + 2 * _nbytes((K, tn), wg.dtype) + _nbytes((tm, tn), CD)) \
        + 6 * _nbytes((tm, tn), F32)
    return pl.pallas_call(
        _swiglu_kernel,
        grid=(M // tm, N // tn),
        in_specs=[pl.BlockSpec((tm, K), lambda i, j: (i, 0)),
                  pl.BlockSpec((K, tn), lambda i, j: (0, j)),
                  pl.BlockSpec((K, tn), lambda i, j: (0, j))],
        out_specs=pl.BlockSpec((tm, tn), lambda i, j: (i, j)),
        out_shape=jax.ShapeDtypeStruct((M, N), CD),
        compiler_params=_params(("parallel", "parallel"), blk),
        name="ffn_swiglu",
    )(x, wg, wu)


def _resid_kernel(x_ref, w_ref, h_ref, o_ref, *, scale):
    o_ref[...] = h_ref[...] + scale * _dot(x_ref[...], w_ref[...])


def _resid_proj(x, w, h, scale, name):
    M, K = x.shape
    N = w.shape[1]
    tm = _pick_tile(M, ROW_TILE // 2, 16)
    tn = _pick_tile(N, COL_TILE, LANES)
    blk = 2 * (_nbytes((tm, K), x.dtype) + _nbytes((K, tn), w.dtype) + 2 * _nbytes((tm, tn), F32)) \
        + 4 * _nbytes((tm, tn), F32)
    return pl.pallas_call(
        functools.partial(_resid_kernel, scale=scale),
        grid=(M // tm, N // tn),
        in_specs=[pl.BlockSpec((tm, K), lambda i, j: (i, 0)),
                  pl.BlockSpec((K, tn), lambda i, j: (0, j)),
                  pl.BlockSpec((tm, tn), lambda i, j: (i, j))],
        out_specs=pl.BlockSpec((tm, tn), lambda i, j: (i, j)),
        out_shape=jax.ShapeDtypeStruct((M, N), F32),
        compiler_params=_params(("parallel", "parallel"), blk),
        name=name,
    )(x, w, h)


def _merge_kernel(o_ref, gm_ref, wa_ref, wg_ref, ga_ref, gg_ref, out_ref):
    a = _dot(o_ref[...], wa_ref[...])
    g = _dot(gm_ref[...], wg_ref[...])
    out_ref[...] = (ga_ref[...].astype(F32) * a + gg_ref[...].astype(F32) * g).astype(out_ref.dtype)


def _merge(o, gm, wa, wg, mg):
    M, Ka = o.shape
    Kg = gm.shape[1]
    N = wa.shape[1]
    tm = _pick_tile(M, ROW_TILE, 16)
    tn = _pick_tile(N, COL_TILE, LANES)
    nj = N // tn
    blk = 2 * (_nbytes((tm, Ka + Kg), CD) + _nbytes((Ka + Kg, tn), CD) + 3 * _nbytes((tm, tn), mg.dtype)) \
        + 6 * _nbytes((tm, tn), F32)
    return pl.pallas_call(
        _merge_kernel,
        grid=(M // tm, nj),
        in_specs=[pl.BlockSpec((tm, Ka), lambda i, j: (i, 0)),
                  pl.BlockSpec((tm, Kg), lambda i, j: (i, 0)),
                  pl.BlockSpec((Ka, tn), lambda i, j: (0, j)),
                  pl.BlockSpec((Kg, tn), lambda i, j: (0, j)),
                  pl.BlockSpec((tm, tn), lambda i, j: (i, j)),
                  pl.BlockSpec((tm, tn), lambda i, j: (i, j + nj))],
        out_specs=pl.BlockSpec((tm, tn), lambda i, j: (i, j)),
        out_shape=jax.ShapeDtypeStruct((M, N), CD),
        compiler_params=_params(("parallel", "parallel"), blk),
        name="mixer_merge",
    )(o, gm, wa, wg, mg, mg)


def _compress_kernel(x_ref, pe_ref, w1_ref, w2_ref, o_ref, *, n_half):
    half = CMP_LEN // 2
    lo = jnp.zeros((n_half, HEAD_DIM), F32)
    hi = jnp.zeros((n_half, HEAD_DIM), F32)
    for l in range(half):
        xl = x_ref[pl.ds(l, n_half, stride=CMP_STRIDE), :]
        lo = lo + _dot((xl + pe_ref[0, l:l + 1, :]).astype(CD), w1_ref[0, l])
        hi = hi + _dot((xl + pe_ref[0, half + l:half + l + 1, :]).astype(CD), w1_ref[0, half + l])
    hid = jax.nn.gelu(lo + pltpu.roll(hi, n_half - 1, 0))
    o_ref[0, 0] = _dot(hid.astype(CD), w2_ref[0]).astype(o_ref.dtype)


def _compress(x2d, n_batch, rows, pe, w1, w2):
    assert CMP_LEN == 2 * CMP_STRIDE and rows % (CMP_STRIDE * SUBLANES) == 0
    n_half = rows // CMP_STRIDE
    blk = 2 * _nbytes((rows, HEAD_DIM), F32) + 8 * _nbytes((n_half, HEAD_DIM), F32) \
        + 4 * _nbytes((CMP_LEN, HEAD_DIM, HEAD_DIM), CD)
    return pl.pallas_call(
        functools.partial(_compress_kernel, n_half=n_half),
        grid=(n_batch, 2 * KV_HEADS),
        in_specs=[pl.BlockSpec((rows, HEAD_DIM), lambda b, c: (b, c)),
                  pl.BlockSpec((1, CMP_LEN, HEAD_DIM), lambda b, c: (c // KV_HEADS, 0, 0)),
                  pl.BlockSpec((1, CMP_LEN, HEAD_DIM, HEAD_DIM), lambda b, c: (c // KV_HEADS, 0, 0, 0)),
                  pl.BlockSpec((1, HEAD_DIM, HEAD_DIM), lambda b, c: (c // KV_HEADS, 0, 0))],
        out_specs=pl.BlockSpec((1, 1, n_half, HEAD_DIM), lambda b, c: (b, c, 0, 0)),
        out_shape=jax.ShapeDtypeStruct((n_batch, 2 * KV_HEADS, n_half, HEAD_DIM), CD),
        compiler_params=_params(("parallel", "parallel"), blk),
        name="nsa_compress",
    )(x2d, pe, w1, w2)


def _masked_softmax(s, mask):
    s = jnp.where(mask, s, NEG_INF)
    m = jnp.max(s, axis=-1, keepdims=True)
    e = jnp.where(mask, jnp.exp(s - m), 0.0)
    d = jnp.sum(e, axis=-1, keepdims=True)
    return e * (1.0 / jnp.where(d > 0.0, d, 1.0))


def _group_sum_lanes(p, n_out):
    n = p.shape[1]
    src = lax.broadcasted_iota(jnp.int32, (n, n_out), 0) // PER_SEL
    dst = lax.broadcasted_iota(jnp.int32, (n, n_out), 1)
    gmat = (src == dst).astype(CD)
    h1 = p.astype(CD)
    r1 = p - h1.astype(F32)
    h2 = r1.astype(CD)
    h3 = (r1 - h2.astype(F32)).astype(CD)
    return _dot(h1, gmat) + _dot(h2, gmat) + _dot(h3, gmat)


def _stack_heads(q, col0):
    return jnp.concatenate([q[:, col0 + r * HEAD_DIM: col0 + (r + 1) * HEAD_DIM] for r in range(GROUP)], axis=0)


def _attn_prompt_kernel(q_ref, ks_ref, vs_ref, kw_ref, vw_ref, kc_ref, vc_ref, g_ref, o_ref,
                        ks_b, vs_b, kw_b, vw_b, *, tq, T):
    i = pl.program_id(2)
    n_sel = T // SEL_LEN
    n_cmp = T // CMP_STRIDE

    @pl.when(i == 0)
    def _():
        ks_b[...] = ks_ref[...].astype(CD)
        vs_b[...] = vs_ref[...].astype(CD)
        kw_b[...] = kw_ref[...].astype(CD)
        vw_b[...] = vw_ref[...].astype(CD)

    qs = _stack_heads(q_ref[...], 0)
    qpos1 = i * tq + lax.broadcasted_iota(jnp.int32, (tq, 1), 0)
    qpos = jnp.concatenate([qpos1] * GROUP, axis=0)

    s = _dot_nt(qs, kc_ref[0, 0]) * SCALE
    blk_end = lax.broadcasted_iota(jnp.int32, (1, n_cmp), 1) * CMP_STRIDE + (CMP_LEN - 1)
    p = _masked_softmax(s, (blk_end <= qpos) & (blk_end < T))
    o_c = _dot(p.astype(CD), vc_ref[0, 0])
    psum = p[0:tq]
    for r in range(1, GROUP):
        psum = psum + p[r * tq:(r + 1) * tq]
    imp = _group_sum_lanes(psum, LANES)

    j = lax.broadcasted_iota(jnp.int32, (1, LANES), 1)
    cur = qpos1 // SEL_LEN
    forced = (j == 0) | (j == cur) | (j == cur - 1)
    score = jnp.where(forced, FORCE, imp)
    score = jnp.where(j <= cur, score, -1.0)
    sc = score.T[:n_sel]
    jrow = lax.broadcasted_iota(jnp.int32, (n_sel, 1), 0)
    rank = jnp.zeros((n_sel, tq), jnp.int32)
    for a in range(n_sel):
        sa = sc[a:a + 1, :]
        rank = rank + ((sa > sc) | ((sa == sc) & (a < jrow))).astype(jnp.int32)
    sel_t = ((rank < SEL_TOP) & (sc >= 0.0)).astype(F32)
    if n_sel < LANES:
        sel_t = jnp.concatenate([sel_t, jnp.zeros((LANES - n_sel, tq), F32)], axis=0)
    sel = sel_t.T.astype(CD)
    expand = (lax.broadcasted_iota(jnp.int32, (LANES, T), 1) // SEL_LEN
              == lax.broadcasted_iota(jnp.int32, (LANES, T), 0)).astype(CD)
    key_sel = _dot(sel, expand)
    key_sel = jnp.concatenate([key_sel] * GROUP, axis=0)

    s = _dot_nt(qs, ks_b[...]) * SCALE
    kpos = lax.broadcasted_iota(jnp.int32, (1, T), 1)
    p = _masked_softmax(s, (key_sel > 0.5) & (kpos <= qpos))
    o_s = _dot(p.astype(CD), vs_b[...])

    span = WINDOW + tq
    start = pl.multiple_of(jnp.maximum(i * tq - WINDOW, 0), tq)
    s = _dot_nt(qs, kw_b[pl.ds(start, span), :]) * SCALE
    kposw = start + lax.broadcasted_iota(jnp.int32, (1, span), 1)
    p = _masked_softmax(s, (kposw <= qpos) & (kposw > qpos - WINDOW))
    o_w = _dot(p.astype(CD), vw_b[pl.ds(start, span), :])

    g = g_ref[...]
    for r in range(GROUP):
        rows = slice(r * tq, (r + 1) * tq)
        o = (g[:, r:r + 1] * o_c[rows] + g[:, GROUP + r:GROUP + r + 1] * o_s[rows]
             + g[:, 2 * GROUP + r:2 * GROUP + r + 1] * o_w[rows])
        o_ref[:, r * HEAD_DIM:(r + 1) * HEAD_DIM] = o.astype(o_ref.dtype)


def _attn_prompt(q, kv, kc, gates):
    T = SEQ
    tq = Q_TILE
    assert T % tq == 0 and WINDOW % tq == 0 and WINDOW + tq <= T and T // SEL_LEN <= LANES
    assert T // CMP_STRIDE == LANES
    nq = T // tq
    gw = GROUP * HEAD_DIM
    kvb = 2 * KV_W // HEAD_DIM
    blk = 2 * 4 * _nbytes((T, HEAD_DIM), F32) + 4 * _nbytes((T, HEAD_DIM), CD) \
        + 6 * _nbytes((GROUP * tq, T), F32) + 4 * _nbytes((tq, gw), F32)
    kv_spec = lambda off: pl.BlockSpec((T, HEAD_DIM), lambda b, g, i: (b, off + g))
    return pl.pallas_call(
        functools.partial(_attn_prompt_kernel, tq=tq, T=T),
        grid=(BATCH, KV_HEADS, nq),
        in_specs=[pl.BlockSpec((tq, gw), lambda b, g, i: (b * nq + i, g)),
                  kv_spec(kvb), kv_spec(kvb + KV_HEADS), kv_spec(2 * kvb), kv_spec(2 * kvb + KV_HEADS),
                  pl.BlockSpec((1, 1, T // CMP_STRIDE, HEAD_DIM), lambda b, g, i: (b, g, 0, 0)),
                  pl.BlockSpec((1, 1, T // CMP_STRIDE, HEAD_DIM), lambda b, g, i: (b, KV_HEADS + g, 0, 0)),
                  pl.BlockSpec((tq, LANES), lambda b, g, i: (b * nq + i, g))],
        out_specs=pl.BlockSpec((tq, gw), lambda b, g, i: (b * nq + i, g)),
        out_shape=jax.ShapeDtypeStruct((BATCH * T, ATTN_W), CD),
        scratch_shapes=[pltpu.VMEM((T, HEAD_DIM), CD)] * 4,
        compiler_params=_params(("parallel", "parallel", "arbitrary"), blk),
        name="nsa_prompt_attention",
    )(q, kv, kv, kv, kv, kc, kc, gates)


def _gather_pages_kernel(pt_ref, pool_ref, o_ref):
    o_ref[...] = pool_ref[...]


def _gather_pages(pool, layer, page_table):
    nb, n_pages = page_table.shape
    W = pool.shape[-1]
    blk = 4 * _nbytes((PAGE_SIZE, W), F32)
    return pl.pallas_call(
        _gather_pages_kernel,
        grid_spec=pltpu.PrefetchScalarGridSpec(
            num_scalar_prefetch=1,
            grid=(nb, n_pages),
            in_specs=[pl.BlockSpec((None, None, PAGE_SIZE, W), lambda b, p, pt: (layer, pt[b, p], 0, 0))],
            out_specs=pl.BlockSpec((PAGE_SIZE, W), lambda b, p, pt: (b * n_pages + p, 0))),
        out_shape=jax.ShapeDtypeStruct((nb * n_pages * PAGE_SIZE, W), F32),
        compiler_params=_params(("parallel", "parallel"), blk),
        name="gather_cmp_pages",
    )(page_table, pool)


def _sample_cmp_kernel(q_ref, kc_ref, vc_ref, oc_ref, sel_ref, *, n_cmp_valid):
    T = DEC_SEQ
    n_cmp = kc_ref.shape[2]
    qs = _stack_heads(q_ref[...], 0).astype(CD)
    qpos1 = PAST_LEN + lax.broadcasted_iota(jnp.int32, (T, 1), 0)
    qpos = jnp.concatenate([qpos1] * GROUP, axis=0)
    s = _dot_nt(qs, kc_ref[0, 0]) * SCALE
    n = lax.broadcasted_iota(jnp.int32, (1, n_cmp), 1)
    p = _masked_softmax(s, (n * CMP_STRIDE + (CMP_LEN - 1) <= qpos) & (n < n_cmp_valid))
    oc_ref[0, 0] = _dot(p.astype(CD), vc_ref[0, 0])
    psum = p[0:T]
    for r in range(1, GROUP):
        psum = psum + p[r * T:(r + 1) * T]
    imp = _group_sum_lanes(psum, LANES)

    j = lax.broadcasted_iota(jnp.int32, (1, LANES), 1)
    cur = qpos1 // SEL_LEN
    score = jnp.where((j == 0) | (j == cur - 1), FORCE, imp)
    rank = jnp.zeros((T, LANES), jnp.int32)
    for sh in range(1, LANES):
        other = pltpu.roll(score, sh, 1)
        oj = jnp.where(j >= sh, j - sh, j - sh + LANES)
        rank = rank + ((other > score) | ((other == score) & (oj < j))).astype(jnp.int32)
    sel_ref[0, 0] = (rank < SEL_TOP - 1).astype(F32)


def _sample_cmp(q_s, kc_s, n_cmp_valid):
    nb = DEC_BATCH
    gw = GROUP * HEAD_DIM
    n_cmp = kc_s.shape[2]
    blk = 16 * _nbytes((GROUP * DEC_SEQ, n_cmp), F32) + 4 * _nbytes((n_cmp, HEAD_DIM), CD)
    return pl.pallas_call(
        functools.partial(_sample_cmp_kernel, n_cmp_valid=n_cmp_valid),
        grid=(nb, KV_HEADS),
        in_specs=[pl.BlockSpec((DEC_SEQ, gw), lambda b, g: (b, g)),
                  pl.BlockSpec((1, 1, n_cmp, HEAD_DIM), lambda b, g: (b, g, 0, 0)),
                  pl.BlockSpec((1, 1, n_cmp, HEAD_DIM), lambda b, g: (b, KV_HEADS + g, 0, 0))],
        out_specs=[pl.BlockSpec((1, 1, GROUP * DEC_SEQ, HEAD_DIM), lambda b, g: (b, g, 0, 0)),
                   pl.BlockSpec((1, 1, DEC_SEQ, LANES), lambda b, g: (b, g, 0, 0))],
        out_shape=[jax.ShapeDtypeStruct((nb, KV_HEADS, GROUP * DEC_SEQ, HEAD_DIM), F32),
                   jax.ShapeDtypeStruct((nb, KV_HEADS, DEC_SEQ, LANES), F32)],
        compiler_params=_params(("parallel", "parallel"), blk),
        name="nsa_sample_cmp_attention",
    )(q_s, kc_s, kc_s)


def _pad_rows(x, rows):
    return jnp.concatenate([x, jnp.zeros((rows - x.shape[0], x.shape[1]), x.dtype)], axis=0)


def _sample_sel_kernel(pt_ref, q_ref, sel_ref, page_ref, new_ref, o_ref, m_sc, l_sc, acc_sc):
    T = DEC_SEQ
    R = GROUP * T
    pg = pl.program_id(1)
    per_page = PAGE_SIZE // SEL_LEN

    @pl.when(pg == 0)
    def _():
        m_sc[...] = jnp.full(m_sc.shape, NEG_INF, F32)
        l_sc[...] = jnp.zeros(l_sc.shape, F32)
        acc_sc[...] = jnp.zeros(acc_sc.shape, F32)

    def update(g, s, mask, v):
        rows = slice(g * R, (g + 1) * R)
        m_prev = m_sc[rows]
        m_new = jnp.maximum(m_prev, jnp.max(jnp.where(mask, s, NEG_INF), axis=-1, keepdims=True))
        alpha = jnp.exp(m_prev - m_new)
        e = jnp.where(mask, jnp.exp(s - m_new), 0.0)
        l_sc[rows] = alpha * l_sc[rows] + jnp.sum(e, axis=-1, keepdims=True)
        acc_sc[rows] = alpha * acc_sc[rows] + _dot(e.astype(CD), v)
        m_sc[rows] = m_new

    q = q_ref[...]
    blk_of_key = pg * per_page + lax.broadcasted_iota(jnp.int32, (LANES, PAGE_SIZE), 1) // SEL_LEN
    expand = (blk_of_key == lax.broadcasted_iota(jnp.int32, (LANES, PAGE_SIZE), 0)).astype(CD)
    key_sel = _dot(sel_ref[...].reshape(KV_HEADS * T, LANES).astype(CD), expand)
    for g in range(KV_HEADS):
        qg = _stack_heads(q, g * GROUP * HEAD_DIM).astype(CD)
        k = page_ref[:, g * HEAD_DIM:(g + 1) * HEAD_DIM].astype(CD)
        v = page_ref[:, KV_W + g * HEAD_DIM:KV_W + (g + 1) * HEAD_DIM].astype(CD)
        mask = jnp.concatenate([key_sel[g * T:(g + 1) * T]] * GROUP, axis=0) > 0.5
        update(g, _dot_nt(qg, k) * SCALE, mask, v)

    @pl.when(pg == pl.num_programs(1) - 1)
    def _():
        t = jnp.concatenate([lax.broadcasted_iota(jnp.int32, (T, 1), 0)] * GROUP, axis=0)
        i = lax.broadcasted_iota(jnp.int32, (1, LANES), 1)
        mask = (i < T) & (i <= t)
        for g in range(KV_HEADS):
            qg = _stack_heads(q, g * GROUP * HEAD_DIM).astype(CD)
            k = _pad_rows(new_ref[:, g * HEAD_DIM:(g + 1) * HEAD_DIM], LANES).astype(CD)
            v = _pad_rows(new_ref[:, KV_W + g * HEAD_DIM:KV_W + (g + 1) * HEAD_DIM], LANES).astype(CD)
            update(g, _dot_nt(qg, k) * SCALE, mask, v)
        o_ref[...] = (acc_sc[...] * (1.0 / l_sc[...])).reshape(o_ref.shape)


def _sample_sel(q_s, sel, pool, layer, page_table, kv_s):
    nb, n_pages = page_table.shape
    W = 2 * KV_W
    R = GROUP * DEC_SEQ
    blk = 4 * _nbytes((PAGE_SIZE, W), F32) + 8 * _nbytes((KV_HEADS * R, LANES), F32)
    return pl.pallas_call(
        _sample_sel_kernel,
        grid_spec=pltpu.PrefetchScalarGridSpec(
            num_scalar_prefetch=1,
            grid=(nb, n_pages),
            in_specs=[pl.BlockSpec((DEC_SEQ, ATTN_W), lambda b, p, pt: (b, 0)),
                      pl.BlockSpec((None, KV_HEADS, DEC_SEQ, LANES), lambda b, p, pt: (b, 0, 0, 0)),
                      pl.BlockSpec((None, None, PAGE_SIZE, W), lambda b, p, pt: (layer, pt[b, p], 0, 0)),
                      pl.BlockSpec((DEC_SEQ, W), lambda b, p, pt: (b, 1))],
            out_specs=pl.BlockSpec((None, KV_HEADS, R, HEAD_DIM), lambda b, p, pt: (b, 0, 0, 0)),
            scratch_shapes=[pltpu.VMEM((KV_HEADS * R, 1), F32), pltpu.VMEM((KV_HEADS * R, 1), F32),
                            pltpu.VMEM((KV_HEADS * R, HEAD_DIM), F32)]),
        out_shape=jax.ShapeDtypeStruct((nb, KV_HEADS, R, HEAD_DIM), F32),
        compiler_params=_params(("parallel", "arbitrary"), blk),
        name="nsa_sample_sel_attention",
    )(page_table, q_s, sel, pool, kv_s)


def _sample_win_kernel(q_ref, buf_ref, new_ref, oc_ref, os_ref, g_ref, o_ref, *, wb):
    T = DEC_SEQ
    q = q_ref[...]
    gates = g_ref[...]
    t = jnp.concatenate([lax.broadcasted_iota(jnp.int32, (T, 1), 0)] * GROUP, axis=0)
    qpos = PAST_LEN + t
    i = lax.broadcasted_iota(jnp.int32, (1, wb + LANES), 1)
    kpos = PAST_LEN - wb + i
    mask = (i < wb + T) & (kpos <= qpos) & (kpos > qpos - WINDOW)
    for g in range(KV_HEADS):
        qg = _stack_heads(q, g * GROUP * HEAD_DIM).astype(CD)
        kcol = slice(g * HEAD_DIM, (g + 1) * HEAD_DIM)
        vcol = slice(KV_W + g * HEAD_DIM, KV_W + (g + 1) * HEAD_DIM)
        k = jnp.concatenate([buf_ref[:, kcol], _pad_rows(new_ref[:, kcol], LANES)], axis=0).astype(CD)
        v = jnp.concatenate([buf_ref[:, vcol], _pad_rows(new_ref[:, vcol], LANES)], axis=0).astype(CD)
        p = _masked_softmax(_dot_nt(qg, k) * SCALE, mask)
        o_w = _dot(p.astype(CD), v)
        o_c = oc_ref[g]
        o_s = os_ref[g]
        gc = g * LANES
        for r in range(GROUP):
            rows = slice(r * T, (r + 1) * T)
            o = (gates[:, gc + r:gc + r + 1] * o_c[rows]
                 + gates[:, gc + GROUP + r:gc + GROUP + r + 1] * o_s[rows]
                 + gates[:, gc + 2 * GROUP + r:gc + 2 * GROUP + r + 1] * o_w[rows])
            col = (g * GROUP + r) * HEAD_DIM
            o_ref[:, col:col + HEAD_DIM] = o


def _sample_win(q_s, win_buf, layer, kv_s, o_c, o_s, gates_s):
    nb = DEC_BATCH
    wb = win_buf.shape[2]
    W = 2 * KV_W
    R = GROUP * DEC_SEQ
    assert wb % LANES == 0
    blk = 4 * _nbytes((wb, W), F32) + 8 * _nbytes((R, wb + LANES), F32)
    return pl.pallas_call(
        functools.partial(_sample_win_kernel, wb=wb),
        grid=(nb,),
        in_specs=[pl.BlockSpec((DEC_SEQ, ATTN_W), lambda b: (b, 0)),
                  pl.BlockSpec((None, None, wb, W), lambda b: (layer, b, 0, 0)),
                  pl.BlockSpec((DEC_SEQ, W), lambda b: (b, 2)),
                  pl.BlockSpec((None, KV_HEADS, R, HEAD_DIM), lambda b: (b, 0, 0, 0)),
                  pl.BlockSpec((None, KV_HEADS, R, HEAD_DIM), lambda b: (b, 0, 0, 0)),
                  pl.BlockSpec((DEC_SEQ, KV_HEADS * LANES), lambda b: (b, 0))],
        out_specs=pl.BlockSpec((DEC_SEQ, ATTN_W), lambda b: (b, 0)),
        out_shape=jax.ShapeDtypeStruct((nb * DEC_SEQ, ATTN_W), F32),
        compiler_params=_params(("parallel",), blk),
        name="nsa_sample_win_attention",
    )(q_s, win_buf, kv_s, o_c, o_s, gates_s)


def _gmlp_kernel(u_ref, v_ref, w_ref, b_ref, o_ref, *, chunk, n_sub):
    R = w_ref.shape[1]
    i = lax.broadcasted_iota(jnp.int32, (R, R), 0)
    j = lax.broadcasted_iota(jnp.int32, (R, R), 1)
    w = jnp.where((j <= i) & (i // chunk == j // chunk), w_ref[0], 0.0).astype(CD)
    for c in range(n_sub):
        rows = slice(c * R, (c + 1) * R)
        mixed = _dot(w, v_ref[rows, :].astype(CD)) + b_ref[0]
        o_ref[rows, :] = (u_ref[rows, :].astype(F32) * mixed).astype(o_ref.dtype)


def _gmlp(u, v, w, b, n_rows, R, chunk, n_sub, row_blk_off):
    cw = GM_WIDTH // GM_GROUPS
    tr = R * n_sub
    assert n_rows % tr == 0
    blk = 2 * (_nbytes((tr, cw), u.dtype) + _nbytes((tr, cw), v.dtype) + _nbytes((tr, cw), CD)) \
        + 4 * _nbytes((R, cw), F32) + 4 * _nbytes((R, R), F32)
    return pl.pallas_call(
        functools.partial(_gmlp_kernel, chunk=chunk, n_sub=n_sub),
        grid=(n_rows // tr, GM_GROUPS),
        in_specs=[pl.BlockSpec((tr, cw), lambda i, g: (i + row_blk_off, g)),
                  pl.BlockSpec((tr, cw), lambda i, g: (i + row_blk_off, g)),
                  pl.BlockSpec((1, R, R), lambda i, g: (g, 0, 0)),
                  pl.BlockSpec((1, R, 1), lambda i, g: (g, 0, 0))],
        out_specs=pl.BlockSpec((tr, cw), lambda i, g: (i, g)),
        out_shape=jax.ShapeDtypeStruct((n_rows, GM_WIDTH), CD),
        compiler_params=_params(("parallel", "parallel"), blk),
        name="gmlp_mix",
    )(u, v, w, b)


def _half_ffn(h, g, w_gate, w_up, w_down):
    n = _rmsnorm(h, g, CD)
    act = _swiglu(n, w_gate.astype(CD), w_up.astype(CD))
    return _resid_proj(act, w_down.astype(CD), h, 0.5, "ffn_down")


def _split_w_in(w_in):
    o1 = ATTN_W
    o4 = o1 + 6 * KV_W
    o5 = o4 + 3 * N_HEADS
    o6 = o5 + GM_WIDTH
    o7 = o6 + GM_WIDTH
    D = w_in.shape[0]
    wg = w_in[:, o4:o5].reshape(D, 3, KV_HEADS, GROUP).transpose(0, 2, 1, 3).reshape(D, KV_HEADS, 3 * GROUP)
    wg = jnp.pad(wg, ((0, 0), (0, 0), (0, LANES - 3 * GROUP))).reshape(D, KV_HEADS * LANES)
    return (w_in[:, :o1].astype(CD), w_in[:, o1:o4].astype(CD), wg.astype(CD), w_in[:, o5:o6].astype(CD),
            w_in[:, o6:o7].astype(CD), w_in[:, o7:].astype(CD))


def _layer(h, layer, lw, cache_cmp, cache_sel, cache_win, page_table):
    (n1, g1, u1, d1, nm, w_in, pe, w1, w2, ws, bs, wua, wug, wo, n2, g2, u2, d2) = lw
    MP = BATCH * SEQ
    MS = DEC_BATCH * DEC_SEQ
    n_pages = page_table.shape[1]
    assert n_pages * PAGE_SIZE == PAST_LEN and PAST_LEN // SEL_LEN == LANES and DEC_SEQ <= SEL_LEN
    assert DEC_SEQ == SUBLANES and PAST_LEN % (CMP_STRIDE * SUBLANES) == 0

    h = _half_ffn(h, n1, g1, u1, d1)

    w_q, w_kv, w_gate, w_u, w_v, w_mg = _split_w_in(w_in)
    n = _rmsnorm(h, nm, CD)
    q = _proj(n, w_q, None, CD, "proj_q")
    kv = _proj(n, w_kv, None, F32, "proj_kv")
    gates = _proj(n, w_gate, "sigmoid", F32, "proj_branch_gates")
    u = _proj(n, w_u, "gelu", CD, "proj_gmlp_u")
    v = _proj(n, w_v, "gelu", F32, "proj_gmlp_v")
    mg = _proj(n, w_mg, "sigmoid", CD, "proj_merge_gates")

    w1c = w1.astype(CD)
    w2c = w2.astype(CD)

    kc_p = _compress(kv, BATCH, SEQ, pe, w1c, w2c)
    o_p = _attn_prompt(q, kv, kc_p, gates)
    cw = GM_WIDTH // GM_GROUPS
    n_sub = 4
    gm_p = _gmlp(u, v, ws, bs[:, :, None], MP, GM_CHUNK, GM_CHUNK, n_sub, 0)

    q_s = q[MP:].astype(F32)
    kv_s = kv[MP:]
    gates_s = gates[MP:]
    past_c = _gather_pages(cache_cmp, layer, page_table)
    kc_s = _compress(past_c, DEC_BATCH, PAST_LEN, pe, w1c, w2c)
    n_cmp_valid = (PAST_LEN + DEC_SEQ - CMP_LEN) // CMP_STRIDE + 1
    oc_s, sel_s = _sample_cmp(q_s, kc_s, n_cmp_valid)
    os_s = _sample_sel(q_s, sel_s, cache_sel, layer, page_table, kv_s)
    o_s = _sample_win(q_s, cache_win, layer, kv_s, oc_s, os_s, gates_s)
    cl = min(DEC_SEQ, GM_CHUNK)
    ws_s = jnp.tile(ws[:, :cl, :cl], (1, MS // cl, MS // cl))
    bs_s = jnp.tile(bs[:, :cl], (1, MS // cl))[:, :, None]
    gm_s = _gmlp(u, v, ws_s, bs_s, MS, MS, cl, 1, MP // MS)

    o = jnp.concatenate([o_p, o_s.astype(CD)], axis=0)
    gm = jnp.concatenate([gm_p, gm_s], axis=0)
    merged = _merge(o, gm, wua.astype(CD), wug.astype(CD), mg)
    h = _resid_proj(merged, wo.astype(CD), h, 1.0, "mixer_out")
    h = _half_ffn(h, n2, g2, u2, d2)

    kv_row = (2, KV_HEADS, HEAD_DIM)
    kv_p = kv[:MP].reshape(BATCH, SEQ, 3, 2 * KV_W)
    kv_s3 = kv_s.reshape(DEC_BATCH, DEC_SEQ, 3, 2 * KV_W)
    wk = min(WINDOW, SEQ)
    win_p = kv_p[:, SEQ - wk:, 2].reshape((BATCH, wk) + kv_row)
    new_win = jnp.concatenate([cache_win[layer].reshape(DEC_BATCH, -1, 2 * KV_W), kv_s3[:, :, 2]], axis=1)
    new_win = new_win[:, -min(WINDOW, new_win.shape[1]):].reshape((DEC_BATCH, -1) + kv_row)
    outs = (kv_p[:, :, 0].reshape((BATCH, SEQ) + kv_row), kv_s3[:, :, 0].reshape((DEC_BATCH, DEC_SEQ) + kv_row),
            kv_p[:, :, 1].reshape((BATCH, SEQ) + kv_row), kv_s3[:, :, 1].reshape((DEC_BATCH, DEC_SEQ) + kv_row),
            win_p, new_win, v[MP:].reshape(DEC_BATCH, DEC_SEQ, GM_WIDTH))
    return h, outs


def kernel(x_prompt, x_sample, cache_cmp_kv, cache_sel_kv, cache_win_kv, page_table, norm_ffn1, ffn1_gate, ffn1_up, ffn1_down, norm_mix, w_in, cmp_pe, cmp_w1, cmp_w2, gm_ws, gm_bs, w_up_attn, w_up_gm, w_out, norm_ffn2, ffn2_gate, ffn2_up, ffn2_down, norm_final):
    MP = BATCH * SEQ
    MS = DEC_BATCH * DEC_SEQ
    h = jnp.concatenate([x_prompt.reshape(MP, D_MODEL), x_sample.reshape(MS, D_MODEL)], axis=0)
    n_phys = cache_cmp_kv.shape[1]
    cache_cmp = cache_cmp_kv.reshape(DEPTH, n_phys, PAGE_SIZE, 2 * KV_W)
    cache_sel = cache_sel_kv.reshape(DEPTH, n_phys, PAGE_SIZE, 2 * KV_W)
    cache_win = cache_win_kv.reshape(DEPTH, DEC_BATCH, -1, 2 * KV_W)
    per_layer = []
    for l in range(DEPTH):
        lw = (norm_ffn1[l], ffn1_gate[l], ffn1_up[l], ffn1_down[l], norm_mix[l], w_in[l], cmp_pe[l], cmp_w1[l],
              cmp_w2[l], gm_ws[l], gm_bs[l], w_up_attn[l], w_up_gm[l], w_out[l], norm_ffn2[l], ffn2_gate[l],
              ffn2_up[l], ffn2_down[l])
        h, outs = _layer(h, l, lw, cache_cmp, cache_sel, cache_win, page_table)
        per_layer.append(outs)
    y_prompt = _rmsnorm(h, norm_final, F32, rows=MP, row_off=0).reshape(BATCH, SEQ, D_MODEL)
    y_sample = _rmsnorm(h, norm_final, F32, rows=MS, row_off=MP, tile=MS).reshape(DEC_BATCH, DEC_SEQ, D_MODEL)
    stacked = tuple(jnp.stack([per_layer[l][k] for l in range(DEPTH)]) for k in range(7))
    return (y_prompt, y_sample) + stacked
```

```python
import functools

import jax
import jax.numpy as jnp
from jax import lax
from jax.experimental import pallas as pl
from jax.experimental.pallas import tpu as pltpu

D_MODEL = 4096
BATCH = 4
SEQ = 2048
DEPTH = 4
DEC_BATCH = 8
DEC_SEQ = 8
PAST_LEN = 8192
PAGE_SIZE = 128

N_HEADS = 16
KV_HEADS = 4
HEAD_DIM = 128
GROUP = N_HEADS // KV_HEADS
ATTN_W = N_HEADS * HEAD_DIM
KV_W = KV_HEADS * HEAD_DIM
CMP_LEN = 32
CMP_STRIDE = 16
SEL_LEN = 64
SEL_TOP = 16
WINDOW = 512
GM_WIDTH = D_MODEL // 2
GM_GROUPS = 4
GM_CHUNK = 128
D_FF = 11008
EPS = 1e-6
NEG_INF = -1e30
FORCE = 1e9
SCALE = HEAD_DIM ** -0.5

LANES = 128
SUBLANES = 8
VMEM_BUDGET = 56 * 1024 * 1024
CD = jnp.bfloat16
F32 = jnp.float32
PER_SEL = SEL_LEN // CMP_STRIDE
KVC = 2 * KV_HEADS

ROW_TILE = 1376
COL_TILE = 512
Q_TILE = 128
SEL_KEY_STEP = 512
KV_ROW_TILE = 512
SEL_PAGES = 8
CMP_PAGES = 4


def _pick_tile(n, target, mult):
    best = n
    for t in range(mult, min(n, target) + 1, mult):
        if n % t == 0:
            best = t
    return best


def _params(sem, block_bytes):
    limit = min(VMEM_BUDGET, max(16 * 1024 * 1024, int(block_bytes)))
    return pltpu.CompilerParams(dimension_semantics=sem, vmem_limit_bytes=limit)


def _nbytes(shape, dtype):
    n = 1
    for s in shape:
        n *= s
    return n * jnp.dtype(dtype).itemsize


def _dot(a, b):
    return jnp.dot(a, b, preferred_element_type=F32)


def _dot_nt(a, b):
    return lax.dot_general(a, b, (((1,), (1,)), ((), ())), preferred_element_type=F32)


def _rms_kernel(x_ref, g_ref, o_ref):
    x = x_ref[...]
    y = x * lax.rsqrt(jnp.mean(x * x, axis=-1, keepdims=True) + EPS)
    o_ref[...] = (y * g_ref[...]).astype(o_ref.dtype)


def _rmsnorm(x, g, out_dtype, rows=None, row_off=0, tile=None):
    M, D = x.shape
    rows = M if rows is None else rows
    tm = tile or _pick_tile(rows, 344, 16)
    assert rows % tm == 0 and row_off % tm == 0
    off = row_off // tm
    blk = 2 * _nbytes((tm, D), F32) * 2 + 2 * _nbytes((tm, D), out_dtype)
    return pl.pallas_call(
        _rms_kernel,
        grid=(rows // tm,),
        in_specs=[pl.BlockSpec((tm, D), lambda i: (i + off, 0)),
                  pl.BlockSpec((1, D), lambda i: (0, 0))],
        out_specs=pl.BlockSpec((tm, D), lambda i: (i, 0)),
        out_shape=jax.ShapeDtypeStruct((rows, D), out_dtype),
        compiler_params=_params(("parallel",), blk),
        name="rmsnorm",
    )(x, g.reshape(1, D))


def _proj_kernel(x_ref, w_ref, o_ref, *, act):
    z = _dot(x_ref[...], w_ref[...])
    if act == "sigmoid":
        z = jax.nn.sigmoid(z)
    o_ref[...] = z.astype(o_ref.dtype)


def _proj(x, w, layer, act, out_dtype, name):
    M, K = x.shape
    N = w.shape[2]
    tm = _pick_tile(M, ROW_TILE, 16)
    tn = _pick_tile(N, COL_TILE, LANES)
    blk = 2 * (_nbytes((tm, K), x.dtype) + _nbytes((K, tn), w.dtype) + _nbytes((tm, tn), out_dtype)) \
        + 4 * _nbytes((tm, tn), F32)
    return pl.pallas_call(
        functools.partial(_proj_kernel, act=act),
        grid=(M // tm, N // tn),
        in_specs=[pl.BlockSpec((tm, K), lambda i, j: (i, 0)),
                  pl.BlockSpec((None, K, tn), lambda i, j: (layer, 0, j))],
        out_specs=pl.BlockSpec((tm, tn), lambda i, j: (i, j)),
        out_shape=jax.ShapeDtypeStruct((M, N), out_dtype),
        compiler_params=_params(("parallel", "parallel"), blk),
        name=name,
    )(x, w)


def _swiglu_kernel(x_ref, wg_ref, wu_ref, o_ref):
    x = x_ref[...]
    a = _dot(x, wg_ref[...])
    b = _dot(x, wu_ref[...])
    o_ref[...] = (jax.nn.silu(a) * b).astype(o_ref.dtype)


def _swiglu(x, wg, wu, layer):
    M, K = x.shape
    N = wg.shape[2]
    tm = _pick_tile(M, ROW_TILE, 16)
    tn = _pick_tile(N, COL_TILE // 2, LANES)
    blk = 2 * (_nbytes((tm, K), x.dtype) + 2 * _nbytes((K, tn), wg.dtype) + _nbytes((tm, tn), CD)) \
        + 6 * _nbytes((tm, tn), F32)
    w_spec = pl.BlockSpec((None, K, tn), lambda i, j: (layer, 0, j))
    return pl.pallas_call(
        _swiglu_kernel,
        grid=(M // tm, N // tn),
        in_specs=[pl.BlockSpec((tm, K), lambda i, j: (i, 0)), w_spec, w_spec],
        out_specs=pl.BlockSpec((tm, tn), lambda i, j: (i, j)),
        out_shape=jax.ShapeDtypeStruct((M, N), CD),
        compiler_params=_params(("parallel", "parallel"), blk),
        name="ffn_swiglu",
    )(x, wg, wu)


def _resid_kernel(x_ref, w_ref, h_ref, o_ref, *, scale):
    o_ref[...] = h_ref[...] + scale * _dot(x_ref[...], w_ref[...])


def _resid_proj(x, w, layer, h, scale, tm_target, tn_target, name):
    M, K = x.shape
    N = w.shape[2]
    tm = _pick_tile(M, tm_target, 16)
    tn = _pick_tile(N, tn_target, LANES)
    blk = 2 * (_nbytes((tm, K), x.dtype) + _nbytes((K, tn), w.dtype) + 2 * _nbytes((tm, tn), F32)) \
        + 4 * _nbytes((tm, tn), F32)
    return pl.pallas_call(
        functools.partial(_resid_kernel, scale=scale),
        grid=(M // tm, N // tn),
        in_specs=[pl.BlockSpec((tm, K), lambda i, j: (i, 0)),
                  pl.BlockSpec((None, K, tn), lambda i, j: (layer, 0, j)),
                  pl.BlockSpec((tm, tn), lambda i, j: (i, j))],
        out_specs=pl.BlockSpec((tm, tn), lambda i, j: (i, j)),
        out_shape=jax.ShapeDtypeStruct((M, N), F32),
        compiler_params=_params(("parallel", "parallel"), blk),
        name=name,
    )(x, w, h)


def _merge_kernel(o_ref, gm_ref, wa_ref, wg_ref, ga_ref, gg_ref, out_ref):
    a = _dot(o_ref[...], wa_ref[...])
    g = _dot(gm_ref[...], wg_ref[...])
    out_ref[...] = (ga_ref[...].astype(F32) * a + gg_ref[...].astype(F32) * g).astype(out_ref.dtype)


def _merge(o, gm, wa, wg, layer, mg):
    M, Ka = o.shape
    Kg = gm.shape[1]
    N = wa.shape[2]
    tm = _pick_tile(M, ROW_TILE, 16)
    tn = _pick_tile(N, COL_TILE, LANES)
    nj = N // tn
    blk = 2 * (_nbytes((tm, Ka + Kg), CD) + _nbytes((Ka + Kg, tn), CD) + 3 * _nbytes((tm, tn), mg.dtype)) \
        + 6 * _nbytes((tm, tn), F32)
    return pl.pallas_call(
        _merge_kernel,
        grid=(M // tm, nj),
        in_specs=[pl.BlockSpec((tm, Ka), lambda i, j: (i, 0)),
                  pl.BlockSpec((tm, Kg), lambda i, j: (i, 0)),
                  pl.BlockSpec((None, Ka, tn), lambda i, j: (layer, 0, j)),
                  pl.BlockSpec((None, Kg, tn), lambda i, j: (layer, 0, j)),
                  pl.BlockSpec((tm, tn), lambda i, j: (i, j)),
                  pl.BlockSpec((tm, tn), lambda i, j: (i, j + nj))],
        out_specs=pl.BlockSpec((tm, tn), lambda i, j: (i, j)),
        out_shape=jax.ShapeDtypeStruct((M, N), CD),
        compiler_params=_params(("parallel", "parallel"), blk),
        name="mixer_merge",
    )(o, gm, wa, wg, mg, mg)


def _kv_kernel(*refs, has_prev, has_std, last_only):
    x_ref, w_ref = refs[0], refs[1]
    il_ref = refs[2 + has_prev]
    tm = x_ref.shape[0]
    z = _dot(x_ref[...], w_ref[...])
    if has_std:
        std_ref = refs[3 + has_prev]
        std_ref[...] = z.astype(std_ref.dtype)

    def write():
        for c in range(KVC):
            il_ref[pl.ds(c, tm, stride=KVC), :] = z[:, c * HEAD_DIM:(c + 1) * HEAD_DIM]

    if last_only:
        tiles_per_batch = SEQ // tm
        pl.when(pl.program_id(0) % tiles_per_batch == tiles_per_batch - 1)(write)
    else:
        write()


def _kv_proj_prompt(x, w, layer, prev, std_dtype, last_only, name):
    K = x.shape[1]
    W = 2 * KV_W
    MP = BATCH * SEQ
    tm = KV_ROW_TILE
    assert SEQ % tm == 0 and (not last_only or tm == min(WINDOW, SEQ))
    if last_only:
        il_shape = (DEPTH, BATCH, tm * KVC, HEAD_DIM)
        il_spec = pl.BlockSpec((None, None, tm * KVC, HEAD_DIM), lambda i: (layer, i // (SEQ // tm), 0, 0))
    else:
        il_shape = (DEPTH, MP * KVC, HEAD_DIM)
        il_spec = pl.BlockSpec((None, tm * KVC, HEAD_DIM), lambda i: (layer, i, 0))
    in_specs = [pl.BlockSpec((tm, K), lambda i: (i, 0)), pl.BlockSpec((None, K, W), lambda i: (layer, 0, 0))]
    args = [x, w]
    aliases = {}
    if prev is not None:
        in_specs.append(pl.BlockSpec(memory_space=pl.ANY))
        args.append(prev)
        aliases = {2: 0}
    blk = 2 * (_nbytes((tm, K), CD) + _nbytes((K, W), CD) + 2 * _nbytes((tm, W), F32)) + 3 * _nbytes((tm, W), F32)
    return pl.pallas_call(
        functools.partial(_kv_kernel, has_prev=prev is not None, has_std=True, last_only=last_only),
        grid=(MP // tm,),
        in_specs=in_specs,
        out_specs=[il_spec, pl.BlockSpec((tm, W), lambda i: (i, 0))],
        out_shape=[jax.ShapeDtypeStruct(il_shape, F32), jax.ShapeDtypeStruct((MP, W), std_dtype)],
        input_output_aliases=aliases,
        compiler_params=_params(("arbitrary",), blk),
        name=name,
    )(*args)


def _kv_proj_sample(x, w, layer, name):
    K = x.shape[1]
    W = 2 * KV_W
    MP = BATCH * SEQ
    MS = DEC_BATCH * DEC_SEQ
    assert MP % MS == 0
    blk = 2 * (_nbytes((MS, K), CD) + _nbytes((K, W), CD) + _nbytes((MS, W), F32)) + 3 * _nbytes((MS, W), F32)
    return pl.pallas_call(
        functools.partial(_kv_kernel, has_prev=False, has_std=False, last_only=False),
        grid=(1,),
        in_specs=[pl.BlockSpec((MS, K), lambda i: (MP // MS, 0)),
                  pl.BlockSpec((None, K, W), lambda i: (layer, 0, 0))],
        out_specs=pl.BlockSpec((MS * KVC, HEAD_DIM), lambda i: (0, 0)),
        out_shape=jax.ShapeDtypeStruct((MS * KVC, HEAD_DIM), F32),
        compiler_params=_params(("arbitrary",), blk),
        name=name,
    )(x, w)


def _gmlp_kernel(x_ref, wu_ref, wv_ref, ws_ref, b_ref, gm_ref, *v_out, chunk):
    R = ws_ref.shape[1]
    tm = x_ref.shape[0]
    x = x_ref[...]
    u = jax.nn.gelu(_dot(x, wu_ref[...]))
    v = jax.nn.gelu(_dot(x, wv_ref[...]))
    if v_out:
        v_out[0][...] = v
    i = lax.broadcasted_iota(jnp.int32, (R, R), 0)
    j = lax.broadcasted_iota(jnp.int32, (R, R), 1)
    w = jnp.where((j <= i) & (i // chunk == j // chunk), ws_ref[0], 0.0).astype(CD)
    vc = v.astype(CD)
    for c in range(tm // R):
        rows = slice(c * R, (c + 1) * R)
        mixed = _dot(w, vc[rows]) + b_ref[0]
        gm_ref[rows, :] = (u[rows] * mixed).astype(gm_ref.dtype)


def _gmlp(x, wu, wv, layer, ws, bs, n_rows, row_off, tm, R, chunk, want_v):
    K = x.shape[1]
    cw = GM_WIDTH // GM_GROUPS
    assert n_rows % tm == 0 and row_off % tm == 0 and tm % R == 0 and cw % LANES == 0
    off = row_off // tm
    blk = 2 * (_nbytes((tm, K), CD) + 2 * _nbytes((K, cw), CD) + 3 * _nbytes((tm, cw), F32)) \
        + 8 * _nbytes((tm, cw), F32)
    w_spec = pl.BlockSpec((None, K, cw), lambda i, g: (layer, 0, g))
    out_specs = [pl.BlockSpec((tm, cw), lambda i, g: (i, g))]
    out_shape = [jax.ShapeDtypeStruct((n_rows, GM_WIDTH), CD)]
    if want_v:
        out_specs.append(pl.BlockSpec((tm, cw), lambda i, g: (i, g)))
        out_shape.append(jax.ShapeDtypeStruct((n_rows, GM_WIDTH), F32))
    return pl.pallas_call(
        functools.partial(_gmlp_kernel, chunk=chunk),
        grid=(n_rows // tm, GM_GROUPS),
        in_specs=[pl.BlockSpec((tm, K), lambda i, g: (i + off, 0)), w_spec, w_spec,
                  pl.BlockSpec((1, R, R), lambda i, g: (g, 0, 0)),
                  pl.BlockSpec((1, R, 1), lambda i, g: (g, 0, 0))],
        out_specs=out_specs,
        out_shape=out_shape,
        compiler_params=_params(("parallel", "parallel"), blk),
        name="gmlp_branch",
    )(x, wu, wv, ws, bs)


def _compress_rows(load_row, pe_l, w1_l, w2, n_half):
    half = CMP_LEN // 2
    lo = jnp.zeros((n_half, HEAD_DIM), F32)
    hi = jnp.zeros((n_half, HEAD_DIM), F32)
    for l in range(half):
        xl = load_row(l)
        lo = lo + _dot((xl + pe_l(l)).astype(CD), w1_l(l))
        hi = hi + _dot((xl + pe_l(half + l)).astype(CD), w1_l(half + l))
    hid = jax.nn.gelu(lo + pltpu.roll(hi, n_half - 1, 0))
    return _dot(hid.astype(CD), w2)


def _compress_prompt_kernel(x_ref, pe_ref, w1_ref, w2_ref, o_ref, *, n_half):
    tok = _compress_rows(lambda l: x_ref[pl.ds(l, n_half, stride=CMP_STRIDE), :],
                         lambda l: pe_ref[0, l:l + 1, :], lambda l: w1_ref[0, l], w2_ref[0], n_half)
    o_ref[0, 0] = tok.astype(o_ref.dtype)


def _compress_prompt(kv_c, pe, w1, w2):
    assert CMP_LEN == 2 * CMP_STRIDE and SEQ % (CMP_STRIDE * SUBLANES) == 0
    n_half = SEQ // CMP_STRIDE
    blk = 2 * _nbytes((SEQ, HEAD_DIM), F32) + 8 * _nbytes((n_half, HEAD_DIM), F32) \
        + 4 * _nbytes((CMP_LEN, HEAD_DIM, HEAD_DIM), CD)
    return pl.pallas_call(
        functools.partial(_compress_prompt_kernel, n_half=n_half),
        grid=(BATCH, KVC),
        in_specs=[pl.BlockSpec((SEQ, HEAD_DIM), lambda b, c: (b, c)),
                  pl.BlockSpec((1, CMP_LEN, HEAD_DIM), lambda b, c: (c // KV_HEADS, 0, 0)),
                  pl.BlockSpec((1, CMP_LEN, HEAD_DIM, HEAD_DIM), lambda b, c: (c // KV_HEADS, 0, 0, 0)),
                  pl.BlockSpec((1, HEAD_DIM, HEAD_DIM), lambda b, c: (c // KV_HEADS, 0, 0))],
        out_specs=pl.BlockSpec((1, 1, n_half, HEAD_DIM), lambda b, c: (b, c, 0, 0)),
        out_shape=jax.ShapeDtypeStruct((BATCH, KVC, n_half, HEAD_DIM), CD),
        compiler_params=_params(("parallel", "parallel"), blk),
        name="nsa_compress_prompt",
    )(kv_c, pe, w1, w2)


def _compress_sample_kernel(pt_ref, *refs, n_half):
    page_refs = refs[:CMP_PAGES]
    pe_ref, w1_ref, w2_ref, o_ref, buf = refs[CMP_PAGES:]
    step = pl.program_id(1)
    page_rows = PAGE_SIZE * KVC
    for k, pr in enumerate(page_refs):
        start = pl.multiple_of((step * CMP_PAGES + k) * page_rows, page_rows)
        buf[pl.ds(start, page_rows), :] = pr[...]

    @pl.when(step == pl.num_programs(1) - 1)
    def _():
        def one_slot(c, carry):
            kv = c // KV_HEADS
            tok = _compress_rows(lambda l: buf[pl.ds(l * KVC + c, n_half, stride=CMP_STRIDE * KVC), :],
                                 lambda l: pe_ref[kv, pl.ds(l, 1), :], lambda l: w1_ref[kv, l], w2_ref[kv],
                                 n_half)
            o_ref[c] = tok.astype(o_ref.dtype)
            return carry
        lax.fori_loop(0, KVC, one_slot, 0)


def _compress_sample(pool, layer, page_table, pe, w1, w2):
    nb, n_pages = page_table.shape
    assert n_pages % CMP_PAGES == 0 and PAST_LEN % (CMP_STRIDE * SUBLANES) == 0
    n_half = PAST_LEN // CMP_STRIDE
    page_rows = PAGE_SIZE * KVC
    blk = _nbytes((PAST_LEN * KVC, HEAD_DIM), F32) + 2 * CMP_PAGES * _nbytes((page_rows, HEAD_DIM), F32) \
        + 4 * _nbytes((2, CMP_LEN, HEAD_DIM, HEAD_DIM), CD) + 2 * _nbytes((KVC, n_half, HEAD_DIM), CD) \
        + 10 * _nbytes((n_half, HEAD_DIM), F32)

    def page_spec(k):
        return pl.BlockSpec((None, None, page_rows, HEAD_DIM),
                            lambda b, s, pt: (layer, pt[b, s * CMP_PAGES + k], 0, 0))

    return pl.pallas_call(
        functools.partial(_compress_sample_kernel, n_half=n_half),
        grid_spec=pltpu.PrefetchScalarGridSpec(
            num_scalar_prefetch=1,
            grid=(nb, n_pages // CMP_PAGES),
            in_specs=[page_spec(k) for k in range(CMP_PAGES)]
            + [pl.BlockSpec((2, CMP_LEN, HEAD_DIM), lambda b, s, pt: (0, 0, 0)),
               pl.BlockSpec((2, CMP_LEN, HEAD_DIM, HEAD_DIM), lambda b, s, pt: (0, 0, 0, 0)),
               pl.BlockSpec((2, HEAD_DIM, HEAD_DIM), lambda b, s, pt: (0, 0, 0))],
            out_specs=pl.BlockSpec((None, KVC, n_half, HEAD_DIM), lambda b, s, pt: (b, 0, 0, 0)),
            scratch_shapes=[pltpu.VMEM((PAST_LEN * KVC, HEAD_DIM), F32)]),
        out_shape=jax.ShapeDtypeStruct((nb, KVC, n_half, HEAD_DIM), CD),
        compiler_params=_params(("parallel", "arbitrary"), blk),
        name="nsa_compress_sample",
    )(page_table, *([pool] * CMP_PAGES), pe, w1, w2)


def _masked_softmax(s, mask):
    s = jnp.where(mask, s, NEG_INF)
    m = jnp.max(s, axis=-1, keepdims=True)
    e = jnp.where(mask, jnp.exp(s - m), 0.0)
    d = jnp.sum(e, axis=-1, keepdims=True)
    return e * (1.0 / jnp.where(d > 0.0, d, 1.0))


def _attend_bias(qs, k, v, bias, tq):
    s = _dot_nt(qs, k)
    es, ds = [], []
    for r in range(GROUP):
        sr = s[r * tq:(r + 1) * tq] * SCALE + bias
        e = jnp.exp(sr - jnp.max(sr, axis=-1, keepdims=True))
        ds.append(jnp.sum(e, axis=-1, keepdims=True))
        es.append(e.astype(CD))
    o = _dot(jnp.concatenate(es, axis=0), v)
    return o * (1.0 / jnp.concatenate(ds, axis=0))


def _group_sum_lanes(p, n_out):
    n = p.shape[1]
    src = lax.broadcasted_iota(jnp.int32, (n, n_out), 0) // PER_SEL
    dst = lax.broadcasted_iota(jnp.int32, (n, n_out), 1)
    gmat = (src == dst).astype(CD)
    h1 = p.astype(CD)
    r1 = p - h1.astype(F32)
    h2 = r1.astype(CD)
    h3 = (r1 - h2.astype(F32)).astype(CD)
    return _dot(h1, gmat) + _dot(h2, gmat) + _dot(h3, gmat)


def _stack_heads(q, col0):
    return jnp.concatenate([q[:, col0 + r * HEAD_DIM: col0 + (r + 1) * HEAD_DIM] for r in range(GROUP)], axis=0)


def _gate_sum(gates, col0, o_c, o_s, o_w, r, rows):
    return (gates[:, col0 + r:col0 + r + 1] * o_c[rows]
            + gates[:, col0 + GROUP + r:col0 + GROUP + r + 1] * o_s[rows]
            + gates[:, col0 + 2 * GROUP + r:col0 + 2 * GROUP + r + 1] * o_w[rows])


def _attn_prompt_kernel(q_ref, ks_ref, vs_ref, kw_ref, vw_ref, kc_ref, vc_ref, g_ref, o_ref, *, tq, T):
    i = pl.program_id(2)
    n_sel = T // SEL_LEN
    n_cmp = T // CMP_STRIDE

    qs = _stack_heads(q_ref[...], 0)
    qpos1 = i * tq + lax.broadcasted_iota(jnp.int32, (tq, 1), 0)
    qpos = jnp.concatenate([qpos1] * GROUP, axis=0)

    s = _dot_nt(qs, kc_ref[0, 0]) * SCALE
    blk_end = lax.broadcasted_iota(jnp.int32, (1, n_cmp), 1) * CMP_STRIDE + (CMP_LEN - 1)
    p = _masked_softmax(s, (blk_end <= qpos) & (blk_end < T))
    o_c = _dot(p.astype(CD), vc_ref[0, 0])
    psum = p[0:tq]
    for r in range(1, GROUP):
        psum = psum + p[r * tq:(r + 1) * tq]
    imp = _group_sum_lanes(psum, LANES)

    j = lax.broadcasted_iota(jnp.int32, (1, LANES), 1)
    cur = qpos1 // SEL_LEN
    forced = (j == 0) | (j == cur) | (j == cur - 1)
    score = jnp.where(forced, FORCE, imp)
    score = jnp.where(j <= cur, score, -1.0)
    sc = score.T[:n_sel]
    jrow = lax.broadcasted_iota(jnp.int32, (n_sel, 1), 0)
    rank = jnp.zeros((n_sel, tq), jnp.int32)
    for a in range(n_sel):
        sa = sc[a:a + 1, :]
        rank = rank + ((sa > sc) | ((sa == sc) & (a < jrow))).astype(jnp.int32)
    sel_t = ((rank < SEL_TOP) & (sc >= 0.0)).astype(F32)
    if n_sel < LANES:
        sel_t = jnp.concatenate([sel_t, jnp.zeros((LANES - n_sel, tq), F32)], axis=0)
    sel = sel_t.T.astype(CD)

    def sel_branch(nk):
        def run():
            expand = (lax.broadcasted_iota(jnp.int32, (LANES, nk), 1) // SEL_LEN
                      == lax.broadcasted_iota(jnp.int32, (LANES, nk), 0)).astype(CD)
            key_sel = _dot(sel, expand)
            kpos = lax.broadcasted_iota(jnp.int32, (1, nk), 1)
            bias = jnp.where((key_sel > 0.5) & (kpos <= qpos1), 0.0, NEG_INF)
            return _attend_bias(qs, ks_ref[0:nk, :], vs_ref[0:nk, :], bias, tq)
        return run

    n_ext = T // SEL_KEY_STEP
    o_s = lax.switch((i * tq) // SEL_KEY_STEP, [sel_branch(SEL_KEY_STEP * (k + 1)) for k in range(n_ext)])

    span = WINDOW + tq
    start = pl.multiple_of(jnp.maximum(i * tq - WINDOW, 0), tq)
    kposw = start + lax.broadcasted_iota(jnp.int32, (1, span), 1)
    bias_w = jnp.where((kposw <= qpos1) & (kposw > qpos1 - WINDOW), 0.0, NEG_INF)
    o_w = _attend_bias(qs, kw_ref[pl.ds(start, span), :], vw_ref[pl.ds(start, span), :], bias_w, tq)

    g = g_ref[...]
    for r in range(GROUP):
        o = _gate_sum(g, 0, o_c, o_s, o_w, r, slice(r * tq, (r + 1) * tq))
        o_ref[:, r * HEAD_DIM:(r + 1) * HEAD_DIM] = o.astype(o_ref.dtype)


def _attn_prompt(q, kv_sel, kv_win, kc, gates):
    T = SEQ
    tq = Q_TILE
    assert T % tq == 0 and WINDOW % tq == 0 and WINDOW + tq <= T and T // SEL_LEN <= LANES
    assert T // CMP_STRIDE == LANES and T % SEL_KEY_STEP == 0 and SEL_KEY_STEP % tq == 0
    nq = T // tq
    gw = GROUP * HEAD_DIM
    blk = 2 * 4 * _nbytes((T, HEAD_DIM), CD) + 8 * _nbytes((GROUP * tq, T), F32) + 4 * _nbytes((tq, gw), F32)
    k_spec = pl.BlockSpec((T, HEAD_DIM), lambda b, g, i: (b, g))
    v_spec = pl.BlockSpec((T, HEAD_DIM), lambda b, g, i: (b, KV_HEADS + g))
    return pl.pallas_call(
        functools.partial(_attn_prompt_kernel, tq=tq, T=T),
        grid=(BATCH, KV_HEADS, nq),
        in_specs=[pl.BlockSpec((tq, gw), lambda b, g, i: (b * nq + i, g)),
                  k_spec, v_spec, k_spec, v_spec,
                  pl.BlockSpec((1, 1, T // CMP_STRIDE, HEAD_DIM), lambda b, g, i: (b, g, 0, 0)),
                  pl.BlockSpec((1, 1, T // CMP_STRIDE, HEAD_DIM), lambda b, g, i: (b, KV_HEADS + g, 0, 0)),
                  pl.BlockSpec((tq, LANES), lambda b, g, i: (b * nq + i, g))],
        out_specs=pl.BlockSpec((tq, gw), lambda b, g, i: (b * nq + i, g)),
        out_shape=jax.ShapeDtypeStruct((BATCH * T, ATTN_W), CD),
        compiler_params=_params(("parallel", "parallel", "parallel"), blk),
        name="nsa_prompt_attention",
    )(q, kv_sel, kv_sel, kv_win, kv_win, kc, kc, gates)


def _sample_cmp_kernel(q_ref, kc_ref, vc_ref, oc_ref, sel_ref, *, n_cmp_valid):
    T = DEC_SEQ
    n_cmp = kc_ref.shape[2]
    qs = _stack_heads(q_ref[...], 0).astype(CD)
    qpos1 = PAST_LEN + lax.broadcasted_iota(jnp.int32, (T, 1), 0)
    qpos = jnp.concatenate([qpos1] * GROUP, axis=0)
    s = _dot_nt(qs, kc_ref[0, 0]) * SCALE
    n = lax.broadcasted_iota(jnp.int32, (1, n_cmp), 1)
    p = _masked_softmax(s, (n * CMP_STRIDE + (CMP_LEN - 1) <= qpos) & (n < n_cmp_valid))
    oc_ref[0, 0] = _dot(p.astype(CD), vc_ref[0, 0])
    psum = p[0:T]
    for r in range(1, GROUP):
        psum = psum + p[r * T:(r + 1) * T]
    imp = _group_sum_lanes(psum, LANES)

    j = lax.broadcasted_iota(jnp.int32, (1, LANES), 1)
    cur = qpos1 // SEL_LEN
    score = jnp.where((j == 0) | (j == cur - 1), FORCE, imp)
    rank = jnp.zeros((T, LANES), jnp.int32)
    for sh in range(1, LANES):
        other = pltpu.roll(score, sh, 1)
        oj = jnp.where(j >= sh, j - sh, j - sh + LANES)
        rank = rank + ((other > score) | ((other == score) & (oj < j))).astype(jnp.int32)
    sel_ref[0, 0] = (rank < SEL_TOP - 1).astype(F32)


def _sample_cmp(q_s, kc_s, n_cmp_valid):
    nb = DEC_BATCH
    gw = GROUP * HEAD_DIM
    n_cmp = kc_s.shape[2]
    blk = 16 * _nbytes((GROUP * DEC_SEQ, n_cmp), F32) + 4 * _nbytes((n_cmp, HEAD_DIM), CD)
    return pl.pallas_call(
        functools.partial(_sample_cmp_kernel, n_cmp_valid=n_cmp_valid),
        grid=(nb, KV_HEADS),
        in_specs=[pl.BlockSpec((DEC_SEQ, gw), lambda b, g: (b, g)),
                  pl.BlockSpec((1, 1, n_cmp, HEAD_DIM), lambda b, g: (b, g, 0, 0)),
                  pl.BlockSpec((1, 1, n_cmp, HEAD_DIM), lambda b, g: (b, KV_HEADS + g, 0, 0))],
        out_specs=[pl.BlockSpec((1, 1, GROUP * DEC_SEQ, HEAD_DIM), lambda b, g: (b, g, 0, 0)),
                   pl.BlockSpec((1, 1, DEC_SEQ, LANES), lambda b, g: (b, g, 0, 0))],
        out_shape=[jax.ShapeDtypeStruct((nb, KV_HEADS, GROUP * DEC_SEQ, HEAD_DIM), F32),
                   jax.ShapeDtypeStruct((nb, KV_HEADS, DEC_SEQ, LANES), F32)],
        compiler_params=_params(("parallel", "parallel"), blk),
        name="nsa_sample_cmp_attention",
    )(q_s, kc_s, kc_s)


def _pad_rows(x, rows):
    return jnp.concatenate([x, jnp.zeros((rows - x.shape[0], x.shape[1]), x.dtype)], axis=0)


def _head_rows(ref, slot, n):
    return ref[pl.ds(slot, n, stride=KVC), :]


def _sample_sel_kernel(pt_ref, q_ref, sel_ref, *refs):
    page_refs = refs[:SEL_PAGES]
    new_ref, o_ref, m_sc, l_sc, acc_sc = refs[SEL_PAGES:]
    T = DEC_SEQ
    R = GROUP * T
    step = pl.program_id(1)
    nk = SEL_PAGES * PAGE_SIZE

    @pl.when(step == 0)
    def _():
        m_sc[...] = jnp.full(m_sc.shape, NEG_INF, F32)
        l_sc[...] = jnp.zeros(l_sc.shape, F32)
        acc_sc[...] = jnp.zeros(acc_sc.shape, F32)

    def update(g, s, mask, v):
        rows = slice(g * R, (g + 1) * R)
        m_prev = m_sc[rows]
        m_new = jnp.maximum(m_prev, jnp.max(jnp.where(mask, s, NEG_INF), axis=-1, keepdims=True))
        alpha = jnp.exp(m_prev - m_new)
        e = jnp.where(mask, jnp.exp(s - m_new), 0.0)
        l_sc[rows] = alpha * l_sc[rows] + jnp.sum(e, axis=-1, keepdims=True)
        acc_sc[rows] = alpha * acc_sc[rows] + _dot(e.astype(CD), v)
        m_sc[rows] = m_new

    q = q_ref[...]
    blk_of_key = step * (nk // SEL_LEN) + lax.broadcasted_iota(jnp.int32, (LANES, nk), 1) // SEL_LEN
    expand = (blk_of_key == lax.broadcasted_iota(jnp.int32, (LANES, nk), 0)).astype(CD)
    key_sel = _dot(sel_ref[...].reshape(KV_HEADS * T, LANES).astype(CD), expand)
    for g in range(KV_HEADS):
        qg = _stack_heads(q, g * GROUP * HEAD_DIM).astype(CD)
        k = jnp.concatenate([_head_rows(pr, g, PAGE_SIZE) for pr in page_refs], axis=0).astype(CD)
        v = jnp.concatenate([_head_rows(pr, KV_HEADS + g, PAGE_SIZE) for pr in page_refs], axis=0).astype(CD)
        mask = jnp.concatenate([key_sel[g * T:(g + 1) * T]] * GROUP, axis=0) > 0.5
        update(g, _dot_nt(qg, k) * SCALE, mask, v)

    @pl.when(step == pl.num_programs(1) - 1)
    def _():
        t = jnp.concatenate([lax.broadcasted_iota(jnp.int32, (T, 1), 0)] * GROUP, axis=0)
        i = lax.broadcasted_iota(jnp.int32, (1, LANES), 1)
        mask = (i < T) & (i <= t)
        for g in range(KV_HEADS):
            qg = _stack_heads(q, g * GROUP * HEAD_DIM).astype(CD)
            k = _pad_rows(_head_rows(new_ref, g, T), LANES).astype(CD)
            v = _pad_rows(_head_rows(new_ref, KV_HEADS + g, T), LANES).astype(CD)
            update(g, _dot_nt(qg, k) * SCALE, mask, v)
        o_ref[...] = (acc_sc[...] * (1.0 / l_sc[...])).reshape(o_ref.shape)


def _sample_sel(q_s, sel, pool, layer, page_table, new_rows):
    nb, n_pages = page_table.shape
    assert n_pages % SEL_PAGES == 0 and (SEL_PAGES * PAGE_SIZE) % SEL_LEN == 0
    R = GROUP * DEC_SEQ
    page_rows = PAGE_SIZE * KVC
    blk = 2 * SEL_PAGES * _nbytes((page_rows, HEAD_DIM), F32) + 16 * _nbytes((R, SEL_PAGES * PAGE_SIZE), F32) \
        + 8 * _nbytes((SEL_PAGES * PAGE_SIZE, HEAD_DIM), F32)

    def page_spec(k):
        return pl.BlockSpec((None, None, page_rows, HEAD_DIM),
                            lambda b, s, pt: (layer, pt[b, s * SEL_PAGES + k], 0, 0))

    return pl.pallas_call(
        _sample_sel_kernel,
        grid_spec=pltpu.PrefetchScalarGridSpec(
            num_scalar_prefetch=1,
            grid=(nb, n_pages // SEL_PAGES),
            in_specs=[pl.BlockSpec((DEC_SEQ, ATTN_W), lambda b, s, pt: (b, 0)),
                      pl.BlockSpec((None, KV_HEADS, DEC_SEQ, LANES), lambda b, s, pt: (b, 0, 0, 0))]
            + [page_spec(k) for k in range(SEL_PAGES)]
            + [pl.BlockSpec((DEC_SEQ * KVC, HEAD_DIM), lambda b, s, pt: (b, 0))],
            out_specs=pl.BlockSpec((None, KV_HEADS, R, HEAD_DIM), lambda b, s, pt: (b, 0, 0, 0)),
            scratch_shapes=[pltpu.VMEM((KV_HEADS * R, 1), F32), pltpu.VMEM((KV_HEADS * R, 1), F32),
                            pltpu.VMEM((KV_HEADS * R, HEAD_DIM), F32)]),
        out_shape=jax.ShapeDtypeStruct((nb, KV_HEADS, R, HEAD_DIM), F32),
        compiler_params=_params(("parallel", "arbitrary"), blk),
        name="nsa_sample_sel_attention",
    )(page_table, q_s, sel, *([pool] * SEL_PAGES), new_rows)


def _sample_win_kernel(q_ref, buf_ref, new_ref, oc_ref, os_ref, g_ref, o_ref, *, wb):
    T = DEC_SEQ
    q = q_ref[...]
    gates = g_ref[...]
    t = jnp.concatenate([lax.broadcasted_iota(jnp.int32, (T, 1), 0)] * GROUP, axis=0)
    qpos = PAST_LEN + t
    i = lax.broadcasted_iota(jnp.int32, (1, wb + LANES), 1)
    kpos = PAST_LEN - wb + i
    mask = (i < wb + T) & (kpos <= qpos) & (kpos > qpos - WINDOW)
    for g in range(KV_HEADS):
        qg = _stack_heads(q, g * GROUP * HEAD_DIM).astype(CD)
        k = jnp.concatenate([_head_rows(buf_ref, g, wb), _pad_rows(_head_rows(new_ref, g, T), LANES)],
                            axis=0).astype(CD)
        v = jnp.concatenate([_head_rows(buf_ref, KV_HEADS + g, wb),
                             _pad_rows(_head_rows(new_ref, KV_HEADS + g, T), LANES)], axis=0).astype(CD)
        p = _masked_softmax(_dot_nt(qg, k) * SCALE, mask)
        o_w = _dot(p.astype(CD), v)
        o_c = oc_ref[g]
        o_s = os_ref[g]
        for r in range(GROUP):
            o = _gate_sum(gates, g * LANES, o_c, o_s, o_w, r, slice(r * T, (r + 1) * T))
            col = (g * GROUP + r) * HEAD_DIM
            o_ref[:, col:col + HEAD_DIM] = o


def _sample_win(q_s, win_buf, layer, new_rows, o_c, o_s, gates_s):
    nb = DEC_BATCH
    wb = win_buf.shape[2] // KVC
    R = GROUP * DEC_SEQ
    assert wb % LANES == 0
    blk = 2 * _nbytes((wb * KVC, HEAD_DIM), F32) + 8 * _nbytes((R, wb + LANES), F32) \
        + 8 * _nbytes((wb + LANES, HEAD_DIM), F32)
    return pl.pallas_call(
        functools.partial(_sample_win_kernel, wb=wb),
        grid=(nb,),
        in_specs=[pl.BlockSpec((DEC_SEQ, ATTN_W), lambda b: (b, 0)),
                  pl.BlockSpec((None, None, wb * KVC, HEAD_DIM), lambda b: (layer, b, 0, 0)),
                  pl.BlockSpec((DEC_SEQ * KVC, HEAD_DIM), lambda b: (b, 0)),
                  pl.BlockSpec((None, KV_HEADS, R, HEAD_DIM), lambda b: (b, 0, 0, 0)),
                  pl.BlockSpec((None, KV_HEADS, R, HEAD_DIM), lambda b: (b, 0, 0, 0)),
                  pl.BlockSpec((DEC_SEQ, KV_HEADS * LANES), lambda b: (b, 0))],
        out_specs=pl.BlockSpec((DEC_SEQ, ATTN_W), lambda b: (b, 0)),
        out_shape=jax.ShapeDtypeStruct((nb * DEC_SEQ, ATTN_W), F32),
        compiler_params=_params(("parallel",), blk),
        name="nsa_sample_win_attention",
    )(q_s, win_buf, new_rows, o_c, o_s, gates_s)


def _half_ffn(h, g, w_gate, w_up, w_down, layer):
    n = _rmsnorm(h, g, CD)
    act = _swiglu(n, w_gate, w_up, layer)
    return _resid_proj(act, w_down, layer, h, 0.5, ROW_TILE // 2, COL_TILE // 2, "ffn_down")


def _prep_w_in(w_in):
    o1 = ATTN_W
    o4 = o1 + 6 * KV_W
    o5 = o4 + 3 * N_HEADS
    o6 = o5 + GM_WIDTH
    o7 = o6 + GM_WIDTH
    D = w_in.shape[1]
    wg = w_in[:, :, o4:o5].reshape(DEPTH, D, 3, KV_HEADS, GROUP).transpose(0, 1, 3, 2, 4)
    wg = jnp.pad(wg.reshape(DEPTH, D, KV_HEADS, 3 * GROUP), ((0, 0), (0, 0), (0, 0), (0, LANES - 3 * GROUP)))
    kv = [w_in[:, :, o1 + s * 2 * KV_W:o1 + (s + 1) * 2 * KV_W].astype(CD) for s in range(3)]
    return dict(q=w_in[:, :, :o1].astype(CD), kv=kv, gate=wg.reshape(DEPTH, D, KV_HEADS * LANES).astype(CD),
                u=w_in[:, :, o5:o6].astype(CD), v=w_in[:, :, o6:o7].astype(CD), mg=w_in[:, :, o7:].astype(CD))


def _layer(h, layer, W, small, caches, page_table, kv_bufs):
    (n1, nm, pe, w1, w2, ws, bs, n2) = small
    cache_cmp, cache_sel, cache_win = caches
    MP = BATCH * SEQ
    MS = DEC_BATCH * DEC_SEQ
    n_pages = page_table.shape[1]
    assert n_pages * PAGE_SIZE == PAST_LEN and PAST_LEN // SEL_LEN == LANES and DEC_SEQ <= SEL_LEN
    assert DEC_SEQ == SUBLANES and MP % MS == 0

    h = _half_ffn(h, n1, W["g1"], W["u1"], W["d1"], layer)

    n = _rmsnorm(h, nm, CD)
    q = _proj(n, W["q"], layer, None, CD, "proj_q")
    gates = _proj(n, W["gate"], layer, "sigmoid", F32, "proj_branch_gates")
    mg = _proj(n, W["mg"], layer, "sigmoid", CD, "proj_merge_gates")
    cmp_buf, kv_c = _kv_proj_prompt(n, W["kv"][0], layer, kv_bufs[0], F32, False, "proj_kv_cmp")
    sel_buf, kv_sel = _kv_proj_prompt(n, W["kv"][1], layer, kv_bufs[1], CD, False, "proj_kv_sel")
    win_buf, kv_win = _kv_proj_prompt(n, W["kv"][2], layer, kv_bufs[2], CD, True, "proj_kv_win")
    new_s = [_kv_proj_sample(n, W["kv"][s], layer, "proj_kv_sample") for s in range(3)]

    w1c = w1.astype(CD)
    w2c = w2.astype(CD)

    kc_p = _compress_prompt(kv_c, pe, w1c, w2c)
    o_p = _attn_prompt(q, kv_sel, kv_win, kc_p, gates)
    (gm_p,) = _gmlp(n, W["u"], W["v"], layer, ws, bs[:, :, None], MP, 0, 1024, GM_CHUNK, GM_CHUNK, False)

    q_s = q[MP:].astype(F32)
    gates_s = gates[MP:]
    kc_s = _compress_sample(cache_cmp, layer, page_table, pe, w1c, w2c)
    n_cmp_valid = (PAST_LEN + DEC_SEQ - CMP_LEN) // CMP_STRIDE + 1
    oc_s, sel_s = _sample_cmp(q_s, kc_s, n_cmp_valid)
    os_s = _sample_sel(q_s, sel_s, cache_sel, layer, page_table, new_s[1])
    o_s = _sample_win(q_s, cache_win, layer, new_s[2], oc_s, os_s, gates_s)
    cl = min(DEC_SEQ, GM_CHUNK)
    ws_s = jnp.tile(ws[:, :cl, :cl], (1, MS // cl, MS // cl))
    bs_s = jnp.tile(bs[:, :cl], (1, MS // cl))[:, :, None]
    gm_s, v_s = _gmlp(n, W["u"], W["v"], layer, ws_s, bs_s, MS, MP, MS, MS, cl, True)

    o = jnp.concatenate([o_p, o_s.astype(CD)], axis=0)
    gm = jnp.concatenate([gm_p, gm_s], axis=0)
    merged = _merge(o, gm, W["wua"], W["wug"], layer, mg)
    h = _resid_proj(merged, W["wo"], layer, h, 1.0, ROW_TILE, COL_TILE, "mixer_out")
    h = _half_ffn(h, n2, W["g2"], W["u2"], W["d2"], layer)
    return h, (cmp_buf, sel_buf, win_buf), (new_s, v_s)


def kernel(x_prompt, x_sample, cache_cmp_kv, cache_sel_kv, cache_win_kv, page_table, norm_ffn1, ffn1_gate, ffn1_up, ffn1_down, norm_mix, w_in, cmp_pe, cmp_w1, cmp_w2, gm_ws, gm_bs, w_up_attn, w_up_gm, w_out, norm_ffn2, ffn2_gate, ffn2_up, ffn2_down, norm_final):
    MP = BATCH * SEQ
    MS = DEC_BATCH * DEC_SEQ
    kv_row = (2, KV_HEADS, HEAD_DIM)
    h = jnp.concatenate([x_prompt.reshape(MP, D_MODEL), x_sample.reshape(MS, D_MODEL)], axis=0)
    n_phys = cache_cmp_kv.shape[1]
    wb = cache_win_kv.shape[2]
    caches = (cache_cmp_kv.reshape(DEPTH, n_phys, PAGE_SIZE * KVC, HEAD_DIM),
              cache_sel_kv.reshape(DEPTH, n_phys, PAGE_SIZE * KVC, HEAD_DIM),
              cache_win_kv.reshape(DEPTH, DEC_BATCH, wb * KVC, HEAD_DIM))
    W = _prep_w_in(w_in)
    W.update(g1=ffn1_gate.astype(CD), u1=ffn1_up.astype(CD), d1=ffn1_down.astype(CD),
             g2=ffn2_gate.astype(CD), u2=ffn2_up.astype(CD), d2=ffn2_down.astype(CD),
             wua=w_up_attn.astype(CD), wug=w_up_gm.astype(CD), wo=w_out.astype(CD))
    kv_bufs = (None, None, None)
    sample_outs = []
    for l in range(DEPTH):
        small = (norm_ffn1[l], norm_mix[l], cmp_pe[l], cmp_w1[l], cmp_w2[l], gm_ws[l], gm_bs[l], norm_ffn2[l])
        h, kv_bufs, s_out = _layer(h, l, W, small, caches, page_table, kv_bufs)
        sample_outs.append(s_out)
    y_prompt = _rmsnorm(h, norm_final, F32, rows=MP, row_off=0).reshape(BATCH, SEQ, D_MODEL)
    y_sample = _rmsnorm(h, norm_final, F32, rows=MS, row_off=MP, tile=MS).reshape(DEC_BATCH, DEC_SEQ, D_MODEL)

    cmp_buf, sel_buf, win_buf = kv_bufs
    wk = min(WINDOW, SEQ)
    new = [jnp.stack([sample_outs[l][0][s] for l in range(DEPTH)]).reshape((DEPTH, DEC_BATCH, DEC_SEQ) + kv_row)
           for s in range(3)]
    win_all = jnp.concatenate([cache_win_kv, new[2]], axis=2)
    win_s = win_all[:, :, -min(WINDOW, win_all.shape[2]):]
    gm_v = jnp.stack([sample_outs[l][1] for l in range(DEPTH)]).reshape(DEPTH, DEC_BATCH, DEC_SEQ, GM_WIDTH)
    return (y_prompt, y_sample,
            cmp_buf.reshape((DEPTH, BATCH, SEQ) + kv_row), new[0],
            sel_buf.reshape((DEPTH, BATCH, SEQ) + kv_row), new[1],
            win_buf.reshape((DEPTH, BATCH, wk) + kv_row), win_s, gm_v)
```

```python
import functools

import jax
import jax.numpy as jnp
from jax import lax
from jax.experimental import pallas as pl
from jax.experimental.pallas import tpu as pltpu

D_MODEL = 4096
BATCH = 4
SEQ = 2048
DEPTH = 4
DEC_BATCH = 8
DEC_SEQ = 8
PAST_LEN = 8192
PAGE_SIZE = 128

N_HEADS = 16
KV_HEADS = 4
HEAD_DIM = 128
GROUP = N_HEADS // KV_HEADS
ATTN_W = N_HEADS * HEAD_DIM
KV_W = KV_HEADS * HEAD_DIM
CMP_LEN = 32
CMP_STRIDE = 16
SEL_LEN = 64
SEL_TOP = 16
WINDOW = 512
GM_WIDTH = D_MODEL // 2
GM_GROUPS = 4
GM_CHUNK = 128
D_FF = 11008
EPS = 1e-6
NEG_INF = -1e30
FORCE = 1e9
SCALE = HEAD_DIM ** -0.5

LANES = 128
SUBLANES = 8
VMEM_BUDGET = 56 * 1024 * 1024
CD = jnp.bfloat16
F32 = jnp.float32
PER_SEL = SEL_LEN // CMP_STRIDE
KVC = 2 * KV_HEADS

ROW_TILE = 1376
COL_TILE = 512
Q_TILE = 256
SEL_KEY_STEP = 512
KV_ROW_TILE = 512
SEL_PAGES = 8
CMP_PAGES = 4


def _pick_tile(n, target, mult):
    best = n
    for t in range(mult, min(n, target) + 1, mult):
        if n % t == 0:
            best = t
    return best


def _params(sem, block_bytes):
    limit = min(VMEM_BUDGET, max(16 * 1024 * 1024, int(block_bytes)))
    return pltpu.CompilerParams(dimension_semantics=sem, vmem_limit_bytes=limit)


def _nbytes(shape, dtype):
    n = 1
    for s in shape:
        n *= s
    return n * jnp.dtype(dtype).itemsize


def _dot(a, b):
    return jnp.dot(a, b, preferred_element_type=F32)


def _dot_nt(a, b):
    return lax.dot_general(a, b, (((1,), (1,)), ((), ())), preferred_element_type=F32)


def _rms_kernel(x_ref, g_ref, o_ref):
    x = x_ref[...]
    y = x * lax.rsqrt(jnp.mean(x * x, axis=-1, keepdims=True) + EPS)
    o_ref[...] = (y * g_ref[...]).astype(o_ref.dtype)


def _rmsnorm(x, g, out_dtype, rows=None, row_off=0, tile=None):
    M, D = x.shape
    rows = M if rows is None else rows
    tm = tile or _pick_tile(rows, 344, 16)
    assert rows % tm == 0 and row_off % tm == 0
    off = row_off // tm
    blk = 2 * _nbytes((tm, D), F32) * 2 + 2 * _nbytes((tm, D), out_dtype)
    return pl.pallas_call(
        _rms_kernel,
        grid=(rows // tm,),
        in_specs=[pl.BlockSpec((tm, D), lambda i: (i + off, 0)),
                  pl.BlockSpec((1, D), lambda i: (0, 0))],
        out_specs=pl.BlockSpec((tm, D), lambda i: (i, 0)),
        out_shape=jax.ShapeDtypeStruct((rows, D), out_dtype),
        compiler_params=_params(("parallel",), blk),
        name="rmsnorm",
    )(x, g.reshape(1, D))


def _proj_kernel(x_ref, w_ref, o_ref, *, act):
    z = _dot(x_ref[...], w_ref[...])
    if act == "sigmoid":
        z = jax.nn.sigmoid(z)
    o_ref[...] = z.astype(o_ref.dtype)


def _proj(x, w, layer, act, out_dtype, name):
    M, K = x.shape
    N = w.shape[2]
    tm = _pick_tile(M, ROW_TILE, 16)
    tn = _pick_tile(N, COL_TILE, LANES)
    blk = 2 * (_nbytes((tm, K), x.dtype) + _nbytes((K, tn), w.dtype) + _nbytes((tm, tn), out_dtype)) \
        + 4 * _nbytes((tm, tn), F32)
    return pl.pallas_call(
        functools.partial(_proj_kernel, act=act),
        grid=(M // tm, N // tn),
        in_specs=[pl.BlockSpec((tm, K), lambda i, j: (i, 0)),
                  pl.BlockSpec((None, K, tn), lambda i, j: (layer, 0, j))],
        out_specs=pl.BlockSpec((tm, tn), lambda i, j: (i, j)),
        out_shape=jax.ShapeDtypeStruct((M, N), out_dtype),
        compiler_params=_params(("parallel", "parallel"), blk),
        name=name,
    )(x, w)


def _swiglu_kernel(x_ref, wg_ref, wu_ref, o_ref):
    x = x_ref[...]
    a = _dot(x, wg_ref[...].astype(CD))
    b = _dot(x, wu_ref[...].astype(CD))
    o_ref[...] = (jax.nn.silu(a) * b).astype(o_ref.dtype)


def _swiglu(x, wg, wu, layer):
    M, K = x.shape
    N = wg.shape[2]
    tm = _pick_tile(M, ROW_TILE, 16)
    tn = _pick_tile(N, COL_TILE // 2, LANES)
    blk = 2 * (_nbytes((tm, K), x.dtype) + 2 * _nbytes((K, tn), wg.dtype) + _nbytes((tm, tn), CD)) \
        + 2 * _nbytes((K, tn), CD) + 6 * _nbytes((tm, tn), F32)
    w_spec = pl.BlockSpec((None, K, tn), lambda i, j: (layer, 0, j))
    return pl.pallas_call(
        _swiglu_kernel,
        grid=(M // tm, N // tn),
        in_specs=[pl.BlockSpec((tm, K), lambda i, j: (i, 0)), w_spec, w_spec],
        out_specs=pl.BlockSpec((tm, tn), lambda i, j: (i, j)),
        out_shape=jax.ShapeDtypeStruct((M, N), CD),
        compiler_params=_params(("parallel", "parallel"), blk),
        name="ffn_swiglu",
    )(x, wg, wu)


def _resid_kernel(x_ref, w_ref, h_ref, o_ref, *, scale):
    o_ref[...] = h_ref[...] + scale * _dot(x_ref[...], w_ref[...])


def _resid_proj(x, w, layer, h, scale, tm_target, tn_target, name):
    M, K = x.shape
    N = w.shape[2]
    tm = _pick_tile(M, tm_target, 16)
    tn = _pick_tile(N, tn_target, LANES)
    blk = 2 * (_nbytes((tm, K), x.dtype) + _nbytes((K, tn), w.dtype) + 2 * _nbytes((tm, tn), F32)) \
        + 4 * _nbytes((tm, tn), F32)
    return pl.pallas_call(
        functools.partial(_resid_kernel, scale=scale),
        grid=(M // tm, N // tn),
        in_specs=[pl.BlockSpec((tm, K), lambda i, j: (i, 0)),
                  pl.BlockSpec((None, K, tn), lambda i, j: (layer, 0, j)),
                  pl.BlockSpec((tm, tn), lambda i, j: (i, j))],
        out_specs=pl.BlockSpec((tm, tn), lambda i, j: (i, j)),
        out_shape=jax.ShapeDtypeStruct((M, N), F32),
        compiler_params=_params(("parallel", "parallel"), blk),
        name=name,
    )(x, w, h)


def _merge_kernel(n_ref, o_ref, gm_ref, wma_ref, wmg_ref, wa_ref, wg_ref, out_ref):
    n = n_ref[...]
    ga = jax.nn.sigmoid(_dot(n, wma_ref[...]))
    gg = jax.nn.sigmoid(_dot(n, wmg_ref[...]))
    a = _dot(o_ref[...], wa_ref[...])
    g = _dot(gm_ref[...], wg_ref[...])
    out_ref[...] = (ga * a + gg * g).astype(out_ref.dtype)


def _merge(n, o, gm, w_mg, wa, wg, layer):
    M, K = n.shape
    Ka = o.shape[1]
    Kg = gm.shape[1]
    N = wa.shape[2]
    tm = _pick_tile(M, ROW_TILE // 2, 16)
    tn = _pick_tile(N, COL_TILE // 2, LANES)
    nj = N // tn
    blk = 2 * (_nbytes((tm, K + Ka + Kg), CD) + _nbytes((2 * K + Ka + Kg, tn), CD) + _nbytes((tm, tn), CD)) \
        + 8 * _nbytes((tm, tn), F32)
    return pl.pallas_call(
        _merge_kernel,
        grid=(M // tm, nj),
        in_specs=[pl.BlockSpec((tm, K), lambda i, j: (i, 0)),
                  pl.BlockSpec((tm, Ka), lambda i, j: (i, 0)),
                  pl.BlockSpec((tm, Kg), lambda i, j: (i, 0)),
                  pl.BlockSpec((None, K, tn), lambda i, j: (layer, 0, j)),
                  pl.BlockSpec((None, K, tn), lambda i, j: (layer, 0, j + nj)),
                  pl.BlockSpec((None, Ka, tn), lambda i, j: (layer, 0, j)),
                  pl.BlockSpec((None, Kg, tn), lambda i, j: (layer, 0, j))],
        out_specs=pl.BlockSpec((tm, tn), lambda i, j: (i, j)),
        out_shape=jax.ShapeDtypeStruct((M, N), CD),
        compiler_params=_params(("parallel", "parallel"), blk),
        name="mixer_merge",
    )(n, o, gm, w_mg, w_mg, wa, wg)


def _kv_kernel(*refs, has_prev, has_std, last_only):
    x_ref, w_ref = refs[0], refs[1]
    il_ref = refs[2 + has_prev]
    tm = x_ref.shape[0]
    z = _dot(x_ref[...], w_ref[...])
    if has_std:
        std_ref = refs[3 + has_prev]
        std_ref[...] = z.astype(std_ref.dtype)

    def write():
        for c in range(KVC):
            il_ref[pl.ds(c, tm, stride=KVC), :] = z[:, c * HEAD_DIM:(c + 1) * HEAD_DIM]

    if last_only:
        tiles_per_batch = SEQ // tm
        pl.when(pl.program_id(0) % tiles_per_batch == tiles_per_batch - 1)(write)
    else:
        write()


def _kv_proj_prompt(x, w, layer, prev, std_dtype, last_only, name):
    K = x.shape[1]
    W = 2 * KV_W
    MP = BATCH * SEQ
    tm = KV_ROW_TILE
    assert SEQ % tm == 0 and (not last_only or tm == min(WINDOW, SEQ))
    if last_only:
        il_shape = (DEPTH, BATCH, tm * KVC, HEAD_DIM)
        il_spec = pl.BlockSpec((None, None, tm * KVC, HEAD_DIM), lambda i: (layer, i // (SEQ // tm), 0, 0))
    else:
        il_shape = (DEPTH, MP * KVC, HEAD_DIM)
        il_spec = pl.BlockSpec((None, tm * KVC, HEAD_DIM), lambda i: (layer, i, 0))
    in_specs = [pl.BlockSpec((tm, K), lambda i: (i, 0)), pl.BlockSpec((None, K, W), lambda i: (layer, 0, 0))]
    args = [x, w]
    aliases = {}
    if prev is not None:
        in_specs.append(pl.BlockSpec(memory_space=pl.ANY))
        args.append(prev)
        aliases = {2: 0}
    blk = 2 * (_nbytes((tm, K), CD) + _nbytes((K, W), CD) + 2 * _nbytes((tm, W), F32)) + 3 * _nbytes((tm, W), F32)
    return pl.pallas_call(
        functools.partial(_kv_kernel, has_prev=prev is not None, has_std=True, last_only=last_only),
        grid=(MP // tm,),
        in_specs=in_specs,
        out_specs=[il_spec, pl.BlockSpec((tm, W), lambda i: (i, 0))],
        out_shape=[jax.ShapeDtypeStruct(il_shape, F32), jax.ShapeDtypeStruct((MP, W), std_dtype)],
        input_output_aliases=aliases,
        compiler_params=_params(("arbitrary",), blk),
        name=name,
    )(*args)


def _kv_proj_sample(x, w, layer, name):
    K = x.shape[1]
    W = 2 * KV_W
    MP = BATCH * SEQ
    MS = DEC_BATCH * DEC_SEQ
    assert MP % MS == 0
    blk = 2 * (_nbytes((MS, K), CD) + _nbytes((K, W), CD) + _nbytes((MS, W), F32)) + 3 * _nbytes((MS, W), F32)
    return pl.pallas_call(
        functools.partial(_kv_kernel, has_prev=False, has_std=False, last_only=False),
        grid=(1,),
        in_specs=[pl.BlockSpec((MS, K), lambda i: (MP // MS, 0)),
                  pl.BlockSpec((None, K, W), lambda i: (layer, 0, 0))],
        out_specs=pl.BlockSpec((MS * KVC, HEAD_DIM), lambda i: (0, 0)),
        out_shape=jax.ShapeDtypeStruct((MS * KVC, HEAD_DIM), F32),
        compiler_params=_params(("arbitrary",), blk),
        name=name,
    )(x, w)


def _gmlp_kernel(x_ref, wu_ref, wv_ref, ws_ref, b_ref, gm_ref, *v_out, chunk):
    R = ws_ref.shape[1]
    tm = x_ref.shape[0]
    x = x_ref[...]
    u = jax.nn.gelu(_dot(x, wu_ref[...]))
    v = jax.nn.gelu(_dot(x, wv_ref[...]))
    if v_out:
        v_out[0][...] = v
    i = lax.broadcasted_iota(jnp.int32, (R, R), 0)
    j = lax.broadcasted_iota(jnp.int32, (R, R), 1)
    w = jnp.where((j <= i) & (i // chunk == j // chunk), ws_ref[0], 0.0).astype(CD)
    vc = v.astype(CD)
    for c in range(tm // R):
        rows = slice(c * R, (c + 1) * R)
        mixed = _dot(w, vc[rows]) + b_ref[0]
        gm_ref[rows, :] = (u[rows] * mixed).astype(gm_ref.dtype)


def _gmlp(x, wu, wv, layer, ws, bs, n_rows, row_off, tm, R, chunk, want_v):
    K = x.shape[1]
    cw = GM_WIDTH // GM_GROUPS
    assert n_rows % tm == 0 and row_off % tm == 0 and tm % R == 0 and cw % LANES == 0
    off = row_off // tm
    blk = 2 * (_nbytes((tm, K), CD) + 2 * _nbytes((K, cw), CD) + 3 * _nbytes((tm, cw), F32)) \
        + 8 * _nbytes((tm, cw), F32)
    w_spec = pl.BlockSpec((None, K, cw), lambda i, g: (layer, 0, g))
    out_specs = [pl.BlockSpec((tm, cw), lambda i, g: (i, g))]
    out_shape = [jax.ShapeDtypeStruct((n_rows, GM_WIDTH), CD)]
    if want_v:
        out_specs.append(pl.BlockSpec((tm, cw), lambda i, g: (i, g)))
        out_shape.append(jax.ShapeDtypeStruct((n_rows, GM_WIDTH), F32))
    return pl.pallas_call(
        functools.partial(_gmlp_kernel, chunk=chunk),
        grid=(n_rows // tm, GM_GROUPS),
        in_specs=[pl.BlockSpec((tm, K), lambda i, g: (i + off, 0)), w_spec, w_spec,
                  pl.BlockSpec((1, R, R), lambda i, g: (g, 0, 0)),
                  pl.BlockSpec((1, R, 1), lambda i, g: (g, 0, 0))],
        out_specs=out_specs,
        out_shape=out_shape,
        compiler_params=_params(("parallel", "parallel"), blk),
        name="gmlp_branch",
    )(x, wu, wv, ws, bs)


def _compress_rows(load_row, pe_l, w1_lo, w1_hi, w2, xlo, xhi, n_half):
    half = CMP_LEN // 2
    for l in range(half):
        xl = load_row(l)
        cols = slice(l * HEAD_DIM, (l + 1) * HEAD_DIM)
        xlo[:, cols] = (xl + pe_l(l)).astype(CD)
        xhi[:, cols] = (xl + pe_l(half + l)).astype(CD)
    hid = jax.nn.gelu(_dot(xlo[...], w1_lo) + pltpu.roll(_dot(xhi[...], w1_hi), n_half - 1, 0))
    return _dot(hid.astype(CD), w2)


def _compress_prompt_kernel(x_ref, pe_ref, w1_ref, w2_ref, o_ref, xlo, xhi, *, n_half):
    kh = (CMP_LEN // 2) * HEAD_DIM
    tok = _compress_rows(lambda l: x_ref[pl.ds(l, n_half, stride=CMP_STRIDE), :],
                         lambda l: pe_ref[0, l:l + 1, :], w1_ref[0, 0:kh, :], w1_ref[0, kh:2 * kh, :], w2_ref[0],
                         xlo, xhi, n_half)
    o_ref[0, 0] = tok.astype(o_ref.dtype)


def _compress_prompt(kv_c, pe, w1, w2):
    assert CMP_LEN == 2 * CMP_STRIDE and SEQ % (CMP_STRIDE * SUBLANES) == 0
    n_half = SEQ // CMP_STRIDE
    kh = (CMP_LEN // 2) * HEAD_DIM
    blk = 2 * _nbytes((SEQ, HEAD_DIM), F32) + 8 * _nbytes((n_half, HEAD_DIM), F32) \
        + 4 * _nbytes((2 * kh, HEAD_DIM), CD) + 2 * _nbytes((n_half, kh), CD)
    return pl.pallas_call(
        functools.partial(_compress_prompt_kernel, n_half=n_half),
        grid=(BATCH, KVC),
        in_specs=[pl.BlockSpec((SEQ, HEAD_DIM), lambda b, c: (b, c)),
                  pl.BlockSpec((1, CMP_LEN, HEAD_DIM), lambda b, c: (c // KV_HEADS, 0, 0)),
                  pl.BlockSpec((1, 2 * kh, HEAD_DIM), lambda b, c: (c // KV_HEADS, 0, 0)),
                  pl.BlockSpec((1, HEAD_DIM, HEAD_DIM), lambda b, c: (c // KV_HEADS, 0, 0))],
        out_specs=pl.BlockSpec((1, 1, n_half, HEAD_DIM), lambda b, c: (b, c, 0, 0)),
        out_shape=jax.ShapeDtypeStruct((BATCH, KVC, n_half, HEAD_DIM), CD),
        scratch_shapes=[pltpu.VMEM((n_half, kh), CD), pltpu.VMEM((n_half, kh), CD)],
        compiler_params=_params(("parallel", "parallel"), blk),
        name="nsa_compress_prompt",
    )(kv_c, pe, w1, w2)


def _compress_sample_kernel(pt_ref, *refs, n_half):
    page_refs = refs[:CMP_PAGES]
    pe_ref, w1_ref, w2_ref, o_ref, buf, xlo, xhi = refs[CMP_PAGES:]
    step = pl.program_id(1)
    page_rows = PAGE_SIZE * KVC
    kh = (CMP_LEN // 2) * HEAD_DIM
    for k, pr in enumerate(page_refs):
        start = pl.multiple_of((step * CMP_PAGES + k) * page_rows, page_rows)
        buf[pl.ds(start, page_rows), :] = pr[...]

    @pl.when(step == pl.num_programs(1) - 1)
    def _():
        def one_slot(c, carry):
            kv = c // KV_HEADS
            tok = _compress_rows(lambda l: buf[pl.ds(l * KVC + c, n_half, stride=CMP_STRIDE * KVC), :],
                                 lambda l: pe_ref[kv, pl.ds(l, 1), :], w1_ref[kv, pl.ds(0, kh), :],
                                 w1_ref[kv, pl.ds(kh, kh), :], w2_ref[kv], xlo, xhi, n_half)
            o_ref[c] = tok.astype(o_ref.dtype)
            return carry
        lax.fori_loop(0, KVC, one_slot, 0)


def _compress_sample(pool, layer, page_table, pe, w1, w2):
    nb, n_pages = page_table.shape
    assert n_pages % CMP_PAGES == 0 and PAST_LEN % (CMP_STRIDE * SUBLANES) == 0
    n_half = PAST_LEN // CMP_STRIDE
    page_rows = PAGE_SIZE * KVC
    kh = (CMP_LEN // 2) * HEAD_DIM
    blk = _nbytes((PAST_LEN * KVC, HEAD_DIM), F32) + 2 * CMP_PAGES * _nbytes((page_rows, HEAD_DIM), F32) \
        + 4 * _nbytes((2, 2 * kh, HEAD_DIM), CD) + 2 * _nbytes((KVC, n_half, HEAD_DIM), CD) \
        + 2 * _nbytes((n_half, kh), CD) + 10 * _nbytes((n_half, HEAD_DIM), F32)

    def page_spec(k):
        return pl.BlockSpec((None, None, page_rows, HEAD_DIM),
                            lambda b, s, pt: (layer, pt[b, s * CMP_PAGES + k], 0, 0))

    return pl.pallas_call(
        functools.partial(_compress_sample_kernel, n_half=n_half),
        grid_spec=pltpu.PrefetchScalarGridSpec(
            num_scalar_prefetch=1,
            grid=(nb, n_pages // CMP_PAGES),
            in_specs=[page_spec(k) for k in range(CMP_PAGES)]
            + [pl.BlockSpec((2, CMP_LEN, HEAD_DIM), lambda b, s, pt: (0, 0, 0)),
               pl.BlockSpec((2, 2 * kh, HEAD_DIM), lambda b, s, pt: (0, 0, 0)),
               pl.BlockSpec((2, HEAD_DIM, HEAD_DIM), lambda b, s, pt: (0, 0, 0))],
            out_specs=pl.BlockSpec((None, KVC, n_half, HEAD_DIM), lambda b, s, pt: (b, 0, 0, 0)),
            scratch_shapes=[pltpu.VMEM((PAST_LEN * KVC, HEAD_DIM), F32), pltpu.VMEM((n_half, kh), CD),
                            pltpu.VMEM((n_half, kh), CD)]),
        out_shape=jax.ShapeDtypeStruct((nb, KVC, n_half, HEAD_DIM), CD),
        compiler_params=_params(("parallel", "arbitrary"), blk),
        name="nsa_compress_sample",
    )(page_table, *([pool] * CMP_PAGES), pe, w1, w2)


def _masked_softmax(s, mask):
    s = jnp.where(mask, s, NEG_INF)
    m = jnp.max(s, axis=-1, keepdims=True)
    e = jnp.where(mask, jnp.exp(s - m), 0.0)
    d = jnp.sum(e, axis=-1, keepdims=True)
    return e * (1.0 / jnp.where(d > 0.0, d, 1.0))


def _attend_bias(qs, k, v, bias, tq):
    s = _dot_nt(qs, k)
    es, ds = [], []
    for r in range(GROUP):
        sr = s[r * tq:(r + 1) * tq] * SCALE + bias
        e = jnp.exp(sr - jnp.max(sr, axis=-1, keepdims=True))
        ds.append(jnp.sum(e, axis=-1, keepdims=True))
        es.append(e.astype(CD))
    o = _dot(jnp.concatenate(es, axis=0), v)
    return o * (1.0 / jnp.concatenate(ds, axis=0))


def _group_sum_lanes(p, n_out):
    n = p.shape[1]
    src = lax.broadcasted_iota(jnp.int32, (n, n_out), 0) // PER_SEL
    dst = lax.broadcasted_iota(jnp.int32, (n, n_out), 1)
    gmat = (src == dst).astype(CD)
    h1 = p.astype(CD)
    r1 = p - h1.astype(F32)
    h2 = r1.astype(CD)
    h3 = (r1 - h2.astype(F32)).astype(CD)
    return _dot(h1, gmat) + _dot(h2, gmat) + _dot(h3, gmat)


def _stack_heads(q, col0):
    return jnp.concatenate([q[:, col0 + r * HEAD_DIM: col0 + (r + 1) * HEAD_DIM] for r in range(GROUP)], axis=0)


def _gate_sum(gates, col0, o_c, o_s, o_w, r, rows):
    return (gates[:, col0 + r:col0 + r + 1] * o_c[rows]
            + gates[:, col0 + GROUP + r:col0 + GROUP + r + 1] * o_s[rows]
            + gates[:, col0 + 2 * GROUP + r:col0 + 2 * GROUP + r + 1] * o_w[rows])


def _attn_prompt_kernel(q_ref, ks_ref, vs_ref, kw_ref, vw_ref, kc_ref, vc_ref, g_ref, o_ref, *, tq, T):
    i = pl.program_id(2)
    n_sel = T // SEL_LEN
    n_cmp = T // CMP_STRIDE

    qs = _stack_heads(q_ref[...], 0)
    qpos1 = i * tq + lax.broadcasted_iota(jnp.int32, (tq, 1), 0)
    qpos = jnp.concatenate([qpos1] * GROUP, axis=0)

    span = WINDOW + tq
    start = pl.multiple_of(jnp.maximum(i * tq - WINDOW, 0), tq)
    kposw = start + lax.broadcasted_iota(jnp.int32, (1, span), 1)
    bias_w = jnp.where((kposw <= qpos1) & (kposw > qpos1 - WINDOW), 0.0, NEG_INF)
    o_w = _attend_bias(qs, kw_ref[pl.ds(start, span), :], vw_ref[pl.ds(start, span), :], bias_w, tq)

    s = _dot_nt(qs, kc_ref[0, 0]) * SCALE
    blk_end = lax.broadcasted_iota(jnp.int32, (1, n_cmp), 1) * CMP_STRIDE + (CMP_LEN - 1)
    p = _masked_softmax(s, (blk_end <= qpos) & (blk_end < T))
    o_c = _dot(p.astype(CD), vc_ref[0, 0])
    psum = p[0:tq]
    for r in range(1, GROUP):
        psum = psum + p[r * tq:(r + 1) * tq]
    imp = _group_sum_lanes(psum, LANES)

    j = lax.broadcasted_iota(jnp.int32, (1, LANES), 1)
    cur = qpos1 // SEL_LEN
    forced = (j == 0) | (j == cur) | (j == cur - 1)
    score = jnp.where(forced, FORCE, imp)
    score = jnp.where(j <= cur, score, -1.0)
    sc = score.T[:n_sel]
    jrow = lax.broadcasted_iota(jnp.int32, (n_sel, 1), 0)
    rank = jnp.zeros((n_sel, tq), jnp.int32)
    for a in range(n_sel):
        sa = sc[a:a + 1, :]
        rank = rank + ((sa > sc) | ((sa == sc) & (a < jrow))).astype(jnp.int32)
    sel_t = ((rank < SEL_TOP) & (sc >= 0.0)).astype(F32)
    if n_sel < LANES:
        sel_t = jnp.concatenate([sel_t, jnp.zeros((LANES - n_sel, tq), F32)], axis=0)
    sel = sel_t.T.astype(CD)

    def sel_branch(nk):
        def run():
            expand = (lax.broadcasted_iota(jnp.int32, (LANES, nk), 1) // SEL_LEN
                      == lax.broadcasted_iota(jnp.int32, (LANES, nk), 0)).astype(CD)
            key_sel = _dot(sel, expand)
            kpos = lax.broadcasted_iota(jnp.int32, (1, nk), 1)
            bias = jnp.where((key_sel > 0.5) & (kpos <= qpos1), 0.0, NEG_INF)
            return _attend_bias(qs, ks_ref[0:nk, :], vs_ref[0:nk, :], bias, tq)
        return run

    n_ext = T // SEL_KEY_STEP
    o_s = lax.switch((i * tq) // SEL_KEY_STEP, [sel_branch(SEL_KEY_STEP * (k + 1)) for k in range(n_ext)])

    g = g_ref[...]
    for r in range(GROUP):
        o = _gate_sum(g, 0, o_c, o_s, o_w, r, slice(r * tq, (r + 1) * tq))
        o_ref[:, r * HEAD_DIM:(r + 1) * HEAD_DIM] = o.astype(o_ref.dtype)


def _attn_prompt(q, kv_sel, kv_win, kc, gates):
    T = SEQ
    tq = Q_TILE
    assert T % tq == 0 and WINDOW % tq == 0 and WINDOW + tq <= T and T // SEL_LEN <= LANES
    assert T // CMP_STRIDE == LANES and T % SEL_KEY_STEP == 0 and SEL_KEY_STEP % tq == 0
    nq = T // tq
    gw = GROUP * HEAD_DIM
    blk = 2 * 4 * _nbytes((T, HEAD_DIM), CD) + 8 * _nbytes((GROUP * tq, T), F32) + 4 * _nbytes((tq, gw), F32)
    k_spec = pl.BlockSpec((T, HEAD_DIM), lambda b, g, i: (b, g))
    v_spec = pl.BlockSpec((T, HEAD_DIM), lambda b, g, i: (b, KV_HEADS + g))
    return pl.pallas_call(
        functools.partial(_attn_prompt_kernel, tq=tq, T=T),
        grid=(BATCH, KV_HEADS, nq),
        in_specs=[pl.BlockSpec((tq, gw), lambda b, g, i: (b * nq + i, g)),
                  k_spec, v_spec, k_spec, v_spec,
                  pl.BlockSpec((1, 1, T // CMP_STRIDE, HEAD_DIM), lambda b, g, i: (b, g, 0, 0)),
                  pl.BlockSpec((1, 1, T // CMP_STRIDE, HEAD_DIM), lambda b, g, i: (b, KV_HEADS + g, 0, 0)),
                  pl.BlockSpec((tq, LANES), lambda b, g, i: (b * nq + i, g))],
        out_specs=pl.BlockSpec((tq, gw), lambda b, g, i: (b * nq + i, g)),
        out_shape=jax.ShapeDtypeStruct((BATCH * T, ATTN_W), CD),
        compiler_params=_params(("parallel", "parallel", "parallel"), blk),
        name="nsa_prompt_attention",
    )(q, kv_sel, kv_sel, kv_win, kv_win, kc, kc, gates)


def _sample_cmp_kernel(q_ref, kc_ref, vc_ref, oc_ref, sel_ref, *, n_cmp_valid):
    T = DEC_SEQ
    n_cmp = kc_ref.shape[2]
    qs = _stack_heads(q_ref[...], 0).astype(CD)
    qpos1 = PAST_LEN + lax.broadcasted_iota(jnp.int32, (T, 1), 0)
    qpos = jnp.concatenate([qpos1] * GROUP, axis=0)
    s = _dot_nt(qs, kc_ref[0, 0]) * SCALE
    n = lax.broadcasted_iota(jnp.int32, (1, n_cmp), 1)
    p = _masked_softmax(s, (n * CMP_STRIDE + (CMP_LEN - 1) <= qpos) & (n < n_cmp_valid))
    oc_ref[0, 0] = _dot(p.astype(CD), vc_ref[0, 0])
    psum = p[0:T]
    for r in range(1, GROUP):
        psum = psum + p[r * T:(r + 1) * T]
    imp = _group_sum_lanes(psum, LANES)

    j = lax.broadcasted_iota(jnp.int32, (1, LANES), 1)
    cur = qpos1 // SEL_LEN
    score = jnp.where((j == 0) | (j == cur - 1), FORCE, imp)
    rank = jnp.zeros((T, LANES), jnp.int32)
    for sh in range(1, LANES):
        other = pltpu.roll(score, sh, 1)
        oj = jnp.where(j >= sh, j - sh, j - sh + LANES)
        rank = rank + ((other > score) | ((other == score) & (oj < j))).astype(jnp.int32)
    sel_ref[0, 0] = (rank < SEL_TOP - 1).astype(F32)


def _sample_cmp(q_s, kc_s, n_cmp_valid):
    nb = DEC_BATCH
    gw = GROUP * HEAD_DIM
    n_cmp = kc_s.shape[2]
    blk = 16 * _nbytes((GROUP * DEC_SEQ, n_cmp), F32) + 4 * _nbytes((n_cmp, HEAD_DIM), CD)
    return pl.pallas_call(
        functools.partial(_sample_cmp_kernel, n_cmp_valid=n_cmp_valid),
        grid=(nb, KV_HEADS),
        in_specs=[pl.BlockSpec((DEC_SEQ, gw), lambda b, g: (b, g)),
                  pl.BlockSpec((1, 1, n_cmp, HEAD_DIM), lambda b, g: (b, g, 0, 0)),
                  pl.BlockSpec((1, 1, n_cmp, HEAD_DIM), lambda b, g: (b, KV_HEADS + g, 0, 0))],
        out_specs=[pl.BlockSpec((1, 1, GROUP * DEC_SEQ, HEAD_DIM), lambda b, g: (b, g, 0, 0)),
                   pl.BlockSpec((1, 1, DEC_SEQ, LANES), lambda b, g: (b, g, 0, 0))],
        out_shape=[jax.ShapeDtypeStruct((nb, KV_HEADS, GROUP * DEC_SEQ, HEAD_DIM), F32),
                   jax.ShapeDtypeStruct((nb, KV_HEADS, DEC_SEQ, LANES), F32)],
        compiler_params=_params(("parallel", "parallel"), blk),
        name="nsa_sample_cmp_attention",
    )(q_s, kc_s, kc_s)


def _pad_rows(x, rows):
    return jnp.concatenate([x, jnp.zeros((rows - x.shape[0], x.shape[1]), x.dtype)], axis=0)


def _head_rows(ref, slot, n):
    return ref[pl.ds(slot, n, stride=KVC), :]


def _sample_sel_kernel(pt_ref, q_ref, sel_ref, *refs):
    page_refs = refs[:SEL_PAGES]
    new_ref, o_ref, m_sc, l_sc, acc_sc = refs[SEL_PAGES:]
    T = DEC_SEQ
    R = GROUP * T
    step = pl.program_id(1)
    nk = SEL_PAGES * PAGE_SIZE

    @pl.when(step == 0)
    def _():
        m_sc[...] = jnp.full(m_sc.shape, NEG_INF, F32)
        l_sc[...] = jnp.zeros(l_sc.shape, F32)
        acc_sc[...] = jnp.zeros(acc_sc.shape, F32)

    def update(g, s, mask, v):
        rows = slice(g * R, (g + 1) * R)
        m_prev = m_sc[rows]
        m_new = jnp.maximum(m_prev, jnp.max(jnp.where(mask, s, NEG_INF), axis=-1, keepdims=True))
        alpha = jnp.exp(m_prev - m_new)
        e = jnp.where(mask, jnp.exp(s - m_new), 0.0)
        l_sc[rows] = alpha * l_sc[rows] + jnp.sum(e, axis=-1, keepdims=True)
        acc_sc[rows] = alpha * acc_sc[rows] + _dot(e.astype(CD), v)
        m_sc[rows] = m_new

    q = q_ref[...]
    blk_of_key = step * (nk // SEL_LEN) + lax.broadcasted_iota(jnp.int32, (LANES, nk), 1) // SEL_LEN
    expand = (blk_of_key == lax.broadcasted_iota(jnp.int32, (LANES, nk), 0)).astype(CD)
    key_sel = _dot(sel_ref[...].reshape(KV_HEADS * T, LANES).astype(CD), expand)
    for g in range(KV_HEADS):
        qg = _stack_heads(q, g * GROUP * HEAD_DIM).astype(CD)
        k = jnp.concatenate([_head_rows(pr, g, PAGE_SIZE) for pr in page_refs], axis=0).astype(CD)
        v = jnp.concatenate([_head_rows(pr, KV_HEADS + g, PAGE_SIZE) for pr in page_refs], axis=0).astype(CD)
        mask = jnp.concatenate([key_sel[g * T:(g + 1) * T]] * GROUP, axis=0) > 0.5
        update(g, _dot_nt(qg, k) * SCALE, mask, v)

    @pl.when(step == pl.num_programs(1) - 1)
    def _():
        t = jnp.concatenate([lax.broadcasted_iota(jnp.int32, (T, 1), 0)] * GROUP, axis=0)
        i = lax.broadcasted_iota(jnp.int32, (1, LANES), 1)
        mask = (i < T) & (i <= t)
        for g in range(KV_HEADS):
            qg = _stack_heads(q, g * GROUP * HEAD_DIM).astype(CD)
            k = _pad_rows(_head_rows(new_ref, g, T), LANES).astype(CD)
            v = _pad_rows(_head_rows(new_ref, KV_HEADS + g, T), LANES).astype(CD)
            update(g, _dot_nt(qg, k) * SCALE, mask, v)
        o_ref[...] = (acc_sc[...] * (1.0 / l_sc[...])).reshape(o_ref.shape)


def _sample_sel(q_s, sel, pool, layer, page_table, new_rows):
    nb, n_pages = page_table.shape
    assert n_pages % SEL_PAGES == 0 and (SEL_PAGES * PAGE_SIZE) % SEL_LEN == 0
    R = GROUP * DEC_SEQ
    page_rows = PAGE_SIZE * KVC
    blk = 2 * SEL_PAGES * _nbytes((page_rows, HEAD_DIM), F32) + 16 * _nbytes((R, SEL_PAGES * PAGE_SIZE), F32) \
        + 8 * _nbytes((SEL_PAGES * PAGE_SIZE, HEAD_DIM), F32)

    def page_spec(k):
        return pl.BlockSpec((None, None, page_rows, HEAD_DIM),
                            lambda b, s, pt: (layer, pt[b, s * SEL_PAGES + k], 0, 0))

    return pl.pallas_call(
        _sample_sel_kernel,
        grid_spec=pltpu.PrefetchScalarGridSpec(
            num_scalar_prefetch=1,
            grid=(nb, n_pages // SEL_PAGES),
            in_specs=[pl.BlockSpec((DEC_SEQ, ATTN_W), lambda b, s, pt: (b, 0)),
                      pl.BlockSpec((None, KV_HEADS, DEC_SEQ, LANES), lambda b, s, pt: (b, 0, 0, 0))]
            + [page_spec(k) for k in range(SEL_PAGES)]
            + [pl.BlockSpec((DEC_SEQ * KVC, HEAD_DIM), lambda b, s, pt: (b, 0))],
            out_specs=pl.BlockSpec((None, KV_HEADS, R, HEAD_DIM), lambda b, s, pt: (b, 0, 0, 0)),
            scratch_shapes=[pltpu.VMEM((KV_HEADS * R, 1), F32), pltpu.VMEM((KV_HEADS * R, 1), F32),
                            pltpu.VMEM((KV_HEADS * R, HEAD_DIM), F32)]),
        out_shape=jax.ShapeDtypeStruct((nb, KV_HEADS, R, HEAD_DIM), F32),
        compiler_params=_params(("parallel", "arbitrary"), blk),
        name="nsa_sample_sel_attention",
    )(page_table, q_s, sel, *([pool] * SEL_PAGES), new_rows)


def _sample_win_kernel(q_ref, buf_ref, new_ref, oc_ref, os_ref, g_ref, o_ref, *, wb):
    T = DEC_SEQ
    q = q_ref[...]
    gates = g_ref[...]
    t = jnp.concatenate([lax.broadcasted_iota(jnp.int32, (T, 1), 0)] * GROUP, axis=0)
    qpos = PAST_LEN + t
    i = lax.broadcasted_iota(jnp.int32, (1, wb + LANES), 1)
    kpos = PAST_LEN - wb + i
    mask = (i < wb + T) & (kpos <= qpos) & (kpos > qpos - WINDOW)
    for g in range(KV_HEADS):
        qg = _stack_heads(q, g * GROUP * HEAD_DIM).astype(CD)
        k = jnp.concatenate([_head_rows(buf_ref, g, wb), _pad_rows(_head_rows(new_ref, g, T), LANES)],
                            axis=0).astype(CD)
        v = jnp.concatenate([_head_rows(buf_ref, KV_HEADS + g, wb),
                             _pad_rows(_head_rows(new_ref, KV_HEADS + g, T), LANES)], axis=0).astype(CD)
        p = _masked_softmax(_dot_nt(qg, k) * SCALE, mask)
        o_w = _dot(p.astype(CD), v)
        o_c = oc_ref[g]
        o_s = os_ref[g]
        for r in range(GROUP):
            o = _gate_sum(gates, g * LANES, o_c, o_s, o_w, r, slice(r * T, (r + 1) * T))
            col = (g * GROUP + r) * HEAD_DIM
            o_ref[:, col:col + HEAD_DIM] = o


def _sample_win(q_s, win_buf, layer, new_rows, o_c, o_s, gates_s):
    nb = DEC_BATCH
    wb = win_buf.shape[2] // KVC
    R = GROUP * DEC_SEQ
    assert wb % LANES == 0
    blk = 2 * _nbytes((wb * KVC, HEAD_DIM), F32) + 8 * _nbytes((R, wb + LANES), F32) \
        + 8 * _nbytes((wb + LANES, HEAD_DIM), F32)
    return pl.pallas_call(
        functools.partial(_sample_win_kernel, wb=wb),
        grid=(nb,),
        in_specs=[pl.BlockSpec((DEC_SEQ, ATTN_W), lambda b: (b, 0)),
                  pl.BlockSpec((None, None, wb * KVC, HEAD_DIM), lambda b: (layer, b, 0, 0)),
                  pl.BlockSpec((DEC_SEQ * KVC, HEAD_DIM), lambda b: (b, 0)),
                  pl.BlockSpec((None, KV_HEADS, R, HEAD_DIM), lambda b: (b, 0, 0, 0)),
                  pl.BlockSpec((None, KV_HEADS, R, HEAD_DIM), lambda b: (b, 0, 0, 0)),
                  pl.BlockSpec((DEC_SEQ, KV_HEADS * LANES), lambda b: (b, 0))],
        out_specs=pl.BlockSpec((DEC_SEQ, ATTN_W), lambda b: (b, 0)),
        out_shape=jax.ShapeDtypeStruct((nb * DEC_SEQ, ATTN_W), F32),
        compiler_params=_params(("parallel",), blk),
        name="nsa_sample_win_attention",
    )(q_s, win_buf, new_rows, o_c, o_s, gates_s)


def _half_ffn(h, g, w_gate, w_up, w_down, layer):
    n = _rmsnorm(h, g, CD)
    act = _swiglu(n, w_gate, w_up, layer)
    return _resid_proj(act, w_down, layer, h, 0.5, ROW_TILE // 2, COL_TILE // 2, "ffn_down")


def _prep_w_in(w_in):
    o1 = ATTN_W
    o4 = o1 + 6 * KV_W
    o5 = o4 + 3 * N_HEADS
    o6 = o5 + GM_WIDTH
    o7 = o6 + GM_WIDTH
    D = w_in.shape[1]
    wg = w_in[:, :, o4:o5].reshape(DEPTH, D, 3, KV_HEADS, GROUP).transpose(0, 1, 3, 2, 4)
    wg = jnp.pad(wg.reshape(DEPTH, D, KV_HEADS, 3 * GROUP), ((0, 0), (0, 0), (0, 0), (0, LANES - 3 * GROUP)))
    kv = [w_in[:, :, o1 + s * 2 * KV_W:o1 + (s + 1) * 2 * KV_W].astype(CD) for s in range(3)]
    return dict(q=w_in[:, :, :o1].astype(CD), kv=kv, gate=wg.reshape(DEPTH, D, KV_HEADS * LANES).astype(CD),
                u=w_in[:, :, o5:o6].astype(CD), v=w_in[:, :, o6:o7].astype(CD), mg=w_in[:, :, o7:].astype(CD))


def _layer(h, layer, W, small, caches, page_table, kv_bufs):
    (n1, nm, pe, w1, w2, ws, bs, n2) = small
    cache_cmp, cache_sel, cache_win = caches
    MP = BATCH * SEQ
    MS = DEC_BATCH * DEC_SEQ
    n_pages = page_table.shape[1]
    assert n_pages * PAGE_SIZE == PAST_LEN and PAST_LEN // SEL_LEN == LANES and DEC_SEQ <= SEL_LEN
    assert DEC_SEQ == SUBLANES and MP % MS == 0

    h = _half_ffn(h, n1, W["g1"], W["u1"], W["d1"], layer)

    n = _rmsnorm(h, nm, CD)
    q = _proj(n, W["q"], layer, None, CD, "proj_q")
    gates = _proj(n, W["gate"], layer, "sigmoid", F32, "proj_branch_gates")
    cmp_buf, kv_c = _kv_proj_prompt(n, W["kv"][0], layer, kv_bufs[0], F32, False, "proj_kv_cmp")
    sel_buf, kv_sel = _kv_proj_prompt(n, W["kv"][1], layer, kv_bufs[1], CD, False, "proj_kv_sel")
    win_buf, kv_win = _kv_proj_prompt(n, W["kv"][2], layer, kv_bufs[2], CD, True, "proj_kv_win")
    new_s = [_kv_proj_sample(n, W["kv"][s], layer, "proj_kv_sample") for s in range(3)]

    w1c = w1.astype(CD).reshape(2, CMP_LEN * HEAD_DIM, HEAD_DIM)
    w2c = w2.astype(CD)

    kc_p = _compress_prompt(kv_c, pe, w1c, w2c)
    o_p = _attn_prompt(q, kv_sel, kv_win, kc_p, gates)
    (gm_p,) = _gmlp(n, W["u"], W["v"], layer, ws, bs[:, :, None], MP, 0, 1024, GM_CHUNK, GM_CHUNK, False)

    q_s = q[MP:].astype(F32)
    gates_s = gates[MP:]
    kc_s = _compress_sample(cache_cmp, layer, page_table, pe, w1c, w2c)
    n_cmp_valid = (PAST_LEN + DEC_SEQ - CMP_LEN) // CMP_STRIDE + 1
    oc_s, sel_s = _sample_cmp(q_s, kc_s, n_cmp_valid)
    os_s = _sample_sel(q_s, sel_s, cache_sel, layer, page_table, new_s[1])
    o_s = _sample_win(q_s, cache_win, layer, new_s[2], oc_s, os_s, gates_s)
    cl = min(DEC_SEQ, GM_CHUNK)
    ws_s = jnp.tile(ws[:, :cl, :cl], (1, MS // cl, MS // cl))
    bs_s = jnp.tile(bs[:, :cl], (1, MS // cl))[:, :, None]
    gm_s, v_s = _gmlp(n, W["u"], W["v"], layer, ws_s, bs_s, MS, MP, MS, MS, cl, True)

    o = jnp.concatenate([o_p, o_s.astype(CD)], axis=0)
    gm = jnp.concatenate([gm_p, gm_s], axis=0)
    merged = _merge(n, o, gm, W["mg"], W["wua"], W["wug"], layer)
    h = _resid_proj(merged, W["wo"], layer, h, 1.0, ROW_TILE, COL_TILE, "mixer_out")
    h = _half_ffn(h, n2, W["g2"], W["u2"], W["d2"], layer)
    return h, (cmp_buf, sel_buf, win_buf), (new_s, v_s)


def kernel(x_prompt, x_sample, cache_cmp_kv, cache_sel_kv, cache_win_kv, page_table, norm_ffn1, ffn1_gate, ffn1_up, ffn1_down, norm_mix, w_in, cmp_pe, cmp_w1, cmp_w2, gm_ws, gm_bs, w_up_attn, w_up_gm, w_out, norm_ffn2, ffn2_gate, ffn2_up, ffn2_down, norm_final):
    MP = BATCH * SEQ
    MS = DEC_BATCH * DEC_SEQ
    kv_row = (2, KV_HEADS, HEAD_DIM)
    h = jnp.concatenate([x_prompt.reshape(MP, D_MODEL), x_sample.reshape(MS, D_MODEL)], axis=0)
    n_phys = cache_cmp_kv.shape[1]
    wb = cache_win_kv.shape[2]
    caches = (cache_cmp_kv.reshape(DEPTH, n_phys, PAGE_SIZE * KVC, HEAD_DIM),
              cache_sel_kv.reshape(DEPTH, n_phys, PAGE_SIZE * KVC, HEAD_DIM),
              cache_win_kv.reshape(DEPTH, DEC_BATCH, wb * KVC, HEAD_DIM))
    W = _prep_w_in(w_in)
    W.update(g1=ffn1_gate, u1=ffn1_up, d1=ffn1_down.astype(CD),
             g2=ffn2_gate, u2=ffn2_up, d2=ffn2_down.astype(CD),
             wua=w_up_attn.astype(CD), wug=w_up_gm.astype(CD), wo=w_out.astype(CD))
    kv_bufs = (None, None, None)
    sample_outs = []
    for l in range(DEPTH):
        small = (norm_ffn1[l], norm_mix[l], cmp_pe[l], cmp_w1[l], cmp_w2[l], gm_ws[l], gm_bs[l], norm_ffn2[l])
        h, kv_bufs, s_out = _layer(h, l, W, small, caches, page_table, kv_bufs)
        sample_outs.append(s_out)
    y_prompt = _rmsnorm(h, norm_final, F32, rows=MP, row_off=0).reshape(BATCH, SEQ, D_MODEL)
    y_sample = _rmsnorm(h, norm_final, F32, rows=MS, row_off=MP, tile=MS).reshape(DEC_BATCH, DEC_SEQ, D_MODEL)

    cmp_buf, sel_buf, win_buf = kv_bufs
    wk = min(WINDOW, SEQ)
    new = [jnp.stack([sample_outs[l][0][s] for l in range(DEPTH)]).reshape((DEPTH, DEC_BATCH, DEC_SEQ) + kv_row)
           for s in range(3)]
    win_all = jnp.concatenate([cache_win_kv, new[2]], axis=2)
    win_s = win_all[:, :, -min(WINDOW, win_all.shape[2]):]
    gm_v = jnp.stack([sample_outs[l][1] for l in range(DEPTH)]).reshape(DEPTH, DEC_BATCH, DEC_SEQ, GM_WIDTH)
    return (y_prompt, y_sample,
            cmp_buf.reshape((DEPTH, BATCH, SEQ) + kv_row), new[0],
            sel_buf.reshape((DEPTH, BATCH, SEQ) + kv_row), new[1],
            win_buf.reshape((DEPTH, BATCH, wk) + kv_row), win_s, gm_v)
```

```python
import functools

import jax
import jax.numpy as jnp
from jax import lax
from jax.experimental import pallas as pl
from jax.experimental.pallas import tpu as pltpu

D_MODEL = 4096
BATCH = 4
SEQ = 2048
DEPTH = 4
DEC_BATCH = 8
DEC_SEQ = 8
PAST_LEN = 8192
PAGE_SIZE = 128

N_HEADS = 16
KV_HEADS = 4
HEAD_DIM = 128
GROUP = N_HEADS // KV_HEADS
ATTN_W = N_HEADS * HEAD_DIM
KV_W = KV_HEADS * HEAD_DIM
CMP_LEN = 32
CMP_STRIDE = 16
SEL_LEN = 64
SEL_TOP = 16
WINDOW = 512
GM_WIDTH = D_MODEL // 2
GM_GROUPS = 4
GM_CHUNK = 128
D_FF = 11008
EPS = 1e-6
NEG_INF = -1e30
FORCE = 1e9
SCALE = HEAD_DIM ** -0.5

LANES = 128
SUBLANES = 8
VMEM_BUDGET = 56 * 1024 * 1024
CD = jnp.bfloat16
F32 = jnp.float32
PER_SEL = SEL_LEN // CMP_STRIDE
KVC = 2 * KV_HEADS

ROW_TILE = 1376
COL_TILE = 512
Q_TILE = 256
SEL_KEY_STEP = 512
KV_ROW_TILE = 512
SEL_PAGES = 8
CMP_PAGES = 4


def _pick_tile(n, target, mult):
    best = n
    for t in range(mult, min(n, target) + 1, mult):
        if n % t == 0:
            best = t
    return best


def _params(sem, block_bytes):
    limit = min(VMEM_BUDGET, max(16 * 1024 * 1024, int(block_bytes)))
    return pltpu.CompilerParams(dimension_semantics=sem, vmem_limit_bytes=limit)


def _nbytes(shape, dtype):
    n = 1
    for s in shape:
        n *= s
    return n * jnp.dtype(dtype).itemsize


def _dot(a, b):
    return jnp.dot(a, b, preferred_element_type=F32)


def _dot_nt(a, b):
    return lax.dot_general(a, b, (((1,), (1,)), ((), ())), preferred_element_type=F32)


def _rms_kernel(x_ref, g_ref, o_ref):
    x = x_ref[...]
    y = x * lax.rsqrt(jnp.mean(x * x, axis=-1, keepdims=True) + EPS)
    o_ref[...] = (y * g_ref[...]).astype(o_ref.dtype)


def _rmsnorm(x, g, out_dtype, rows=None, row_off=0, tile=None):
    M, D = x.shape
    rows = M if rows is None else rows
    tm = tile or _pick_tile(rows, 344, 16)
    assert rows % tm == 0 and row_off % tm == 0
    off = row_off // tm
    blk = 2 * _nbytes((tm, D), F32) * 2 + 2 * _nbytes((tm, D), out_dtype)
    return pl.pallas_call(
        _rms_kernel,
        grid=(rows // tm,),
        in_specs=[pl.BlockSpec((tm, D), lambda i: (i + off, 0)),
                  pl.BlockSpec((1, D), lambda i: (0, 0))],
        out_specs=pl.BlockSpec((tm, D), lambda i: (i, 0)),
        out_shape=jax.ShapeDtypeStruct((rows, D), out_dtype),
        compiler_params=_params(("parallel",), blk),
        name="rmsnorm",
    )(x, g.reshape(1, D))


def _inv_rms(ss_ref):
    return lax.rsqrt(ss_ref[...] * (1.0 / D_MODEL) + EPS)


def _ss_spec(ss, tm, off=0):
    return pl.BlockSpec((tm, 1), lambda i, *_: (i + off, 0))


def _norm_prep_kernel(x_ref, g_ref, hb_ref, ss_ref):
    x = x_ref[...]
    hb_ref[...] = (x * g_ref[...]).astype(hb_ref.dtype)
    ss_ref[...] = jnp.sum(x * x, axis=-1, keepdims=True)


def _norm_prep(x, g):
    M, D = x.shape
    tm = _pick_tile(M, 344, 16)
    blk = 4 * _nbytes((tm, D), F32) + 2 * _nbytes((tm, D), CD)
    return pl.pallas_call(
        _norm_prep_kernel,
        grid=(M // tm,),
        in_specs=[pl.BlockSpec((tm, D), lambda i: (i, 0)),
                  pl.BlockSpec((1, D), lambda i: (0, 0))],
        out_specs=[pl.BlockSpec((tm, D), lambda i: (i, 0)), pl.BlockSpec((tm, 1), lambda i: (i, 0))],
        out_shape=[jax.ShapeDtypeStruct((M, D), CD), jax.ShapeDtypeStruct((M, 1), F32)],
        compiler_params=_params(("parallel",), blk),
        name="norm_prep",
    )(x, g.reshape(1, D))


def _proj_kernel(x_ref, ss_ref, w_ref, o_ref, *, act):
    z = _inv_rms(ss_ref) * _dot(x_ref[...], w_ref[...])
    if act == "sigmoid":
        z = jax.nn.sigmoid(z)
    o_ref[...] = z.astype(o_ref.dtype)


def _proj(xn, w, layer, act, out_dtype, name):
    x, ss = xn
    M, K = x.shape
    N = w.shape[2]
    tm = _pick_tile(M, ROW_TILE, 16)
    tn = _pick_tile(N, COL_TILE, LANES)
    blk = 2 * (_nbytes((tm, K), x.dtype) + _nbytes((K, tn), w.dtype) + _nbytes((tm, tn), out_dtype)) \
        + 4 * _nbytes((tm, tn), F32)
    return pl.pallas_call(
        functools.partial(_proj_kernel, act=act),
        grid=(M // tm, N // tn),
        in_specs=[pl.BlockSpec((tm, K), lambda i, j: (i, 0)), _ss_spec(ss, tm),
                  pl.BlockSpec((None, K, tn), lambda i, j: (layer, 0, j))],
        out_specs=pl.BlockSpec((tm, tn), lambda i, j: (i, j)),
        out_shape=jax.ShapeDtypeStruct((M, N), out_dtype),
        compiler_params=_params(("parallel", "parallel"), blk),
        name=name,
    )(x, ss, w)


def _swiglu_kernel(x_ref, ss_ref, wg_ref, wu_ref, o_ref):
    x = x_ref[...]
    inv = _inv_rms(ss_ref)
    a = inv * _dot(x, wg_ref[...].astype(CD))
    b = inv * _dot(x, wu_ref[...].astype(CD))
    o_ref[...] = (jax.nn.silu(a) * b).astype(o_ref.dtype)


def _swiglu(xn, wg, wu, layer):
    x, ss = xn
    M, K = x.shape
    N = wg.shape[2]
    tm = _pick_tile(M, ROW_TILE, 16)
    tn = _pick_tile(N, COL_TILE // 2, LANES)
    blk = 2 * (_nbytes((tm, K), x.dtype) + 2 * _nbytes((K, tn), wg.dtype) + _nbytes((tm, tn), CD)) \
        + 2 * _nbytes((K, tn), CD) + 6 * _nbytes((tm, tn), F32)
    w_spec = pl.BlockSpec((None, K, tn), lambda i, j: (layer, 0, j))
    return pl.pallas_call(
        _swiglu_kernel,
        grid=(M // tm, N // tn),
        in_specs=[pl.BlockSpec((tm, K), lambda i, j: (i, 0)), _ss_spec(ss, tm), w_spec, w_spec],
        out_specs=pl.BlockSpec((tm, tn), lambda i, j: (i, j)),
        out_shape=jax.ShapeDtypeStruct((M, N), CD),
        compiler_params=_params(("parallel", "parallel"), blk),
        name="ffn_swiglu",
    )(x, ss, wg, wu)


def _resid_kernel(x_ref, w_ref, h_ref, *refs, scale, emit_norm):
    hn = h_ref[...] + scale * _dot(x_ref[...], w_ref[...])
    if emit_norm:
        g_ref, o_ref, hb_ref, ss_ref = refs
        hb_ref[...] = (hn * g_ref[...]).astype(hb_ref.dtype)
        part = jnp.sum(hn * hn, axis=-1, keepdims=True)
        j = pl.program_id(1)

        @pl.when(j == 0)
        def _():
            ss_ref[...] = part

        @pl.when(j > 0)
        def _():
            ss_ref[...] += part
    else:
        (o_ref,) = refs
    o_ref[...] = hn


def _resid_proj(x, w, layer, h, scale, g_next, tm_target, tn_target, name):
    M, K = x.shape
    N = w.shape[2]
    tm = _pick_tile(M, tm_target, 16)
    tn = _pick_tile(N, tn_target, LANES)
    nj = N // tn
    emit = g_next is not None
    blk = 2 * (_nbytes((tm, K), x.dtype) + _nbytes((K, tn), w.dtype) + 3 * _nbytes((tm, tn), F32)) \
        + 4 * _nbytes((tm, tn), F32)
    tile = pl.BlockSpec((tm, tn), lambda i, j: (i, j))
    in_specs = [pl.BlockSpec((tm, K), lambda i, j: (i, 0)),
                pl.BlockSpec((None, K, tn), lambda i, j: (layer, 0, j)), tile]
    args = [x, w, h]
    out_specs = [tile]
    out_shape = [jax.ShapeDtypeStruct((M, N), F32)]
    if emit:
        in_specs.append(pl.BlockSpec((1, tn), lambda i, j: (0, j)))
        args.append(g_next.reshape(1, N))
        out_specs += [tile, pl.BlockSpec((tm, 1), lambda i, j: (i, 0))]
        out_shape += [jax.ShapeDtypeStruct((M, N), CD), jax.ShapeDtypeStruct((M, 1), F32)]
    outs = pl.pallas_call(
        functools.partial(_resid_kernel, scale=scale, emit_norm=emit),
        grid=(M // tm, nj),
        in_specs=in_specs,
        out_specs=out_specs,
        out_shape=out_shape,
        compiler_params=_params(("parallel", "arbitrary"), blk),
        name=name,
    )(*args)
    return (outs[0], (outs[1], outs[2])) if emit else (outs[0], None)


def _merge_kernel(n_ref, ss_ref, o_ref, gm_ref, wma_ref, wmg_ref, wa_ref, wg_ref, out_ref):
    n = n_ref[...]
    inv = _inv_rms(ss_ref)
    ga = jax.nn.sigmoid(inv * _dot(n, wma_ref[...]))
    gg = jax.nn.sigmoid(inv * _dot(n, wmg_ref[...]))
    a = _dot(o_ref[...], wa_ref[...])
    g = _dot(gm_ref[...], wg_ref[...])
    out_ref[...] = (ga * a + gg * g).astype(out_ref.dtype)


def _merge(xn, o, gm, w_mg, wa, wg, layer):
    n, ss = xn
    M, K = n.shape
    Ka = o.shape[1]
    Kg = gm.shape[1]
    N = wa.shape[2]
    tm = _pick_tile(M, ROW_TILE // 2, 16)
    tn = _pick_tile(N, COL_TILE // 2, LANES)
    nj = N // tn
    blk = 2 * (_nbytes((tm, K + Ka + Kg), CD) + _nbytes((2 * K + Ka + Kg, tn), CD) + _nbytes((tm, tn), CD)) \
        + 8 * _nbytes((tm, tn), F32)
    return pl.pallas_call(
        _merge_kernel,
        grid=(M // tm, nj),
        in_specs=[pl.BlockSpec((tm, K), lambda i, j: (i, 0)), _ss_spec(ss, tm),
                  pl.BlockSpec((tm, Ka), lambda i, j: (i, 0)),
                  pl.BlockSpec((tm, Kg), lambda i, j: (i, 0)),
                  pl.BlockSpec((None, K, tn), lambda i, j: (layer, 0, j)),
                  pl.BlockSpec((None, K, tn), lambda i, j: (layer, 0, j + nj)),
                  pl.BlockSpec((None, Ka, tn), lambda i, j: (layer, 0, j)),
                  pl.BlockSpec((None, Kg, tn), lambda i, j: (layer, 0, j))],
        out_specs=pl.BlockSpec((tm, tn), lambda i, j: (i, j)),
        out_shape=jax.ShapeDtypeStruct((M, N), CD),
        compiler_params=_params(("parallel", "parallel"), blk),
        name="mixer_merge",
    )(n, ss, o, gm, w_mg, w_mg, wa, wg)


def _kv_kernel(*refs, has_prev, has_std, last_only):
    x_ref, ss_ref, w_ref = refs[0], refs[1], refs[2]
    il_ref = refs[3 + has_prev]
    tm = x_ref.shape[0]
    z = _inv_rms(ss_ref) * _dot(x_ref[...], w_ref[...])
    if has_std:
        std_ref = refs[4 + has_prev]
        std_ref[...] = z.astype(std_ref.dtype)

    def write():
        for c in range(KVC):
            il_ref[pl.ds(c, tm, stride=KVC), :] = z[:, c * HEAD_DIM:(c + 1) * HEAD_DIM]

    if last_only:
        tiles_per_batch = SEQ // tm
        pl.when(pl.program_id(0) % tiles_per_batch == tiles_per_batch - 1)(write)
    else:
        write()


def _kv_proj_prompt(xn, w, layer, prev, std_dtype, last_only, name):
    x, ss = xn
    K = x.shape[1]
    W = 2 * KV_W
    MP = BATCH * SEQ
    tm = KV_ROW_TILE
    assert SEQ % tm == 0 and (not last_only or tm == min(WINDOW, SEQ))
    if last_only:
        il_shape = (DEPTH, BATCH, tm * KVC, HEAD_DIM)
        il_spec = pl.BlockSpec((None, None, tm * KVC, HEAD_DIM), lambda i: (layer, i // (SEQ // tm), 0, 0))
    else:
        il_shape = (DEPTH, MP * KVC, HEAD_DIM)
        il_spec = pl.BlockSpec((None, tm * KVC, HEAD_DIM), lambda i: (layer, i, 0))
    in_specs = [pl.BlockSpec((tm, K), lambda i: (i, 0)), _ss_spec(ss, tm),
                pl.BlockSpec((None, K, W), lambda i: (layer, 0, 0))]
    args = [x, ss, w]
    aliases = {}
    if prev is not None:
        in_specs.append(pl.BlockSpec(memory_space=pl.ANY))
        args.append(prev)
        aliases = {3: 0}
    blk = 2 * (_nbytes((tm, K), CD) + _nbytes((K, W), CD) + 2 * _nbytes((tm, W), F32)) + 3 * _nbytes((tm, W), F32)
    return pl.pallas_call(
        functools.partial(_kv_kernel, has_prev=prev is not None, has_std=True, last_only=last_only),
        grid=(MP // tm,),
        in_specs=in_specs,
        out_specs=[il_spec, pl.BlockSpec((tm, W), lambda i: (i, 0))],
        out_shape=[jax.ShapeDtypeStruct(il_shape, F32), jax.ShapeDtypeStruct((MP, W), std_dtype)],
        input_output_aliases=aliases,
        compiler_params=_params(("arbitrary",), blk),
        name=name,
    )(*args)


def _kv_proj_sample(xn, w, layer, name):
    x, ss = xn
    K = x.shape[1]
    W = 2 * KV_W
    MP = BATCH * SEQ
    MS = DEC_BATCH * DEC_SEQ
    assert MP % MS == 0
    blk = 2 * (_nbytes((MS, K), CD) + _nbytes((K, W), CD) + _nbytes((MS, W), F32)) + 3 * _nbytes((MS, W), F32)
    return pl.pallas_call(
        functools.partial(_kv_kernel, has_prev=False, has_std=False, last_only=False),
        grid=(1,),
        in_specs=[pl.BlockSpec((MS, K), lambda i: (MP // MS, 0)), _ss_spec(ss, MS, MP // MS),
                  pl.BlockSpec((None, K, W), lambda i: (layer, 0, 0))],
        out_specs=pl.BlockSpec((MS * KVC, HEAD_DIM), lambda i: (0, 0)),
        out_shape=jax.ShapeDtypeStruct((MS * KVC, HEAD_DIM), F32),
        compiler_params=_params(("arbitrary",), blk),
        name=name,
    )(x, ss, w)


def _gmlp_kernel(x_ref, ss_ref, wu_ref, wv_ref, ws_ref, b_ref, *refs, chunk, has_prev, want_v):
    gm_ref = refs[has_prev]
    R = ws_ref.shape[1]
    tm = x_ref.shape[0]
    x = x_ref[...]
    inv = _inv_rms(ss_ref)
    u = jax.nn.gelu(inv * _dot(x, wu_ref[...]))
    v = jax.nn.gelu(inv * _dot(x, wv_ref[...]))
    if want_v:
        refs[has_prev + 1][...] = v
    i = lax.broadcasted_iota(jnp.int32, (R, R), 0)
    j = lax.broadcasted_iota(jnp.int32, (R, R), 1)
    w = jnp.where((j <= i) & (i // chunk == j // chunk), ws_ref[0], 0.0).astype(CD)
    vc = v.astype(CD)
    for c in range(tm // R):
        rows = slice(c * R, (c + 1) * R)
        mixed = _dot(w, vc[rows]) + b_ref[0]
        gm_ref[rows, :] = (u[rows] * mixed).astype(gm_ref.dtype)


def _gmlp(xn, wu, wv, layer, ws, bs, n_rows, row_off, tm, R, chunk, prev, want_v):
    x, ss = xn
    M, K = x.shape
    cw = GM_WIDTH // GM_GROUPS
    assert n_rows % tm == 0 and row_off % tm == 0 and tm % R == 0 and cw % LANES == 0
    off = row_off // tm
    blk = 2 * (_nbytes((tm, K), CD) + 2 * _nbytes((K, cw), CD) + 3 * _nbytes((tm, cw), F32)) \
        + 8 * _nbytes((tm, cw), F32)
    w_spec = pl.BlockSpec((None, K, cw), lambda i, g: (layer, 0, g))
    in_specs = [pl.BlockSpec((tm, K), lambda i, g: (i + off, 0)), _ss_spec(ss, tm, off), w_spec, w_spec,
                pl.BlockSpec((1, R, R), lambda i, g: (g, 0, 0)),
                pl.BlockSpec((1, R, 1), lambda i, g: (g, 0, 0))]
    args = [x, ss, wu, wv, ws, bs]
    aliases = {}
    if prev is not None:
        in_specs.append(pl.BlockSpec(memory_space=pl.ANY))
        args.append(prev)
        aliases = {6: 0}
    out_specs = [pl.BlockSpec((tm, cw), lambda i, g: (i + off, g))]
    out_shape = [jax.ShapeDtypeStruct((M, GM_WIDTH), CD)]
    if want_v:
        out_specs.append(pl.BlockSpec((tm, cw), lambda i, g: (i, g)))
        out_shape.append(jax.ShapeDtypeStruct((n_rows, GM_WIDTH), F32))
    return pl.pallas_call(
        functools.partial(_gmlp_kernel, chunk=chunk, has_prev=prev is not None, want_v=want_v),
        grid=(n_rows // tm, GM_GROUPS),
        in_specs=in_specs,
        out_specs=out_specs,
        out_shape=out_shape,
        input_output_aliases=aliases,
        compiler_params=_params(("parallel", "parallel"), blk),
        name="gmlp_branch",
    )(*args)


def _compress_rows(load_row, pe_l, w1_lo, w1_hi, w2, xlo, xhi, n_half):
    half = CMP_LEN // 2
    for l in range(half):
        xl = load_row(l)
        cols = slice(l * HEAD_DIM, (l + 1) * HEAD_DIM)
        xlo[:, cols] = (xl + pe_l(l)).astype(CD)
        xhi[:, cols] = (xl + pe_l(half + l)).astype(CD)
    hid = jax.nn.gelu(_dot(xlo[...], w1_lo) + pltpu.roll(_dot(xhi[...], w1_hi), n_half - 1, 0))
    return _dot(hid.astype(CD), w2)


def _compress_prompt_kernel(x_ref, pe_ref, w1_ref, w2_ref, o_ref, xlo, xhi, *, n_half):
    kh = (CMP_LEN // 2) * HEAD_DIM
    tok = _compress_rows(lambda l: x_ref[pl.ds(l, n_half, stride=CMP_STRIDE), :],
                         lambda l: pe_ref[0, l:l + 1, :], w1_ref[0, 0:kh, :], w1_ref[0, kh:2 * kh, :], w2_ref[0],
                         xlo, xhi, n_half)
    o_ref[0, 0] = tok.astype(o_ref.dtype)


def _compress_prompt(kv_c, pe, w1, w2):
    assert CMP_LEN == 2 * CMP_STRIDE and SEQ % (CMP_STRIDE * SUBLANES) == 0
    n_half = SEQ // CMP_STRIDE
    kh = (CMP_LEN // 2) * HEAD_DIM
    blk = 2 * _nbytes((SEQ, HEAD_DIM), F32) + 8 * _nbytes((n_half, HEAD_DIM), F32) \
        + 4 * _nbytes((2 * kh, HEAD_DIM), CD) + 2 * _nbytes((n_half, kh), CD)
    return pl.pallas_call(
        functools.partial(_compress_prompt_kernel, n_half=n_half),
        grid=(BATCH, KVC),
        in_specs=[pl.BlockSpec((SEQ, HEAD_DIM), lambda b, c: (b, c)),
                  pl.BlockSpec((1, CMP_LEN, HEAD_DIM), lambda b, c: (c // KV_HEADS, 0, 0)),
                  pl.BlockSpec((1, 2 * kh, HEAD_DIM), lambda b, c: (c // KV_HEADS, 0, 0)),
                  pl.BlockSpec((1, HEAD_DIM, HEAD_DIM), lambda b, c: (c // KV_HEADS, 0, 0))],
        out_specs=pl.BlockSpec((1, 1, n_half, HEAD_DIM), lambda b, c: (b, c, 0, 0)),
        out_shape=jax.ShapeDtypeStruct((BATCH, KVC, n_half, HEAD_DIM), CD),
        scratch_shapes=[pltpu.VMEM((n_half, kh), CD), pltpu.VMEM((n_half, kh), CD)],
        compiler_params=_params(("parallel", "parallel"), blk),
        name="nsa_compress_prompt",
    )(kv_c, pe, w1, w2)


def _compress_sample_kernel(pt_ref, *refs, n_half):
    page_refs = refs[:CMP_PAGES]
    pe_ref, w1_ref, w2_ref, o_ref, buf, xlo, xhi = refs[CMP_PAGES:]
    step = pl.program_id(1)
    page_rows = PAGE_SIZE * KVC
    kh = (CMP_LEN // 2) * HEAD_DIM
    for k, pr in enumerate(page_refs):
        start = pl.multiple_of((step * CMP_PAGES + k) * PAGE_SIZE, PAGE_SIZE)
        for c in range(KVC):
            buf[c, pl.ds(start, PAGE_SIZE), :] = _head_rows(pr, c, PAGE_SIZE)

    @pl.when(step == pl.num_programs(1) - 1)
    def _():
        def one_slot(c, carry):
            kv = c // KV_HEADS
            tok = _compress_rows(lambda l: buf[c, pl.ds(l, n_half, stride=CMP_STRIDE), :],
                                 lambda l: pe_ref[kv, pl.ds(l, 1), :], w1_ref[kv, pl.ds(0, kh), :],
                                 w1_ref[kv, pl.ds(kh, kh), :], w2_ref[kv], xlo, xhi, n_half)
            o_ref[c] = tok.astype(o_ref.dtype)
            return carry
        lax.fori_loop(0, KVC, one_slot, 0)


def _compress_sample(pool, layer, page_table, pe, w1, w2):
    nb, n_pages = page_table.shape
    assert n_pages % CMP_PAGES == 0 and PAST_LEN % (CMP_STRIDE * SUBLANES) == 0
    n_half = PAST_LEN // CMP_STRIDE
    page_rows = PAGE_SIZE * KVC
    kh = (CMP_LEN // 2) * HEAD_DIM
    blk = _nbytes((PAST_LEN * KVC, HEAD_DIM), F32) + 2 * CMP_PAGES * _nbytes((page_rows, HEAD_DIM), F32) \
        + 4 * _nbytes((2, 2 * kh, HEAD_DIM), CD) + 2 * _nbytes((KVC, n_half, HEAD_DIM), CD) \
        + 2 * _nbytes((n_half, kh), CD) + 10 * _nbytes((n_half, HEAD_DIM), F32)

    def page_spec(k):
        return pl.BlockSpec((None, None, page_rows, HEAD_DIM),
                            lambda b, s, pt: (layer, pt[b, s * CMP_PAGES + k], 0, 0))

    return pl.pallas_call(
        functools.partial(_compress_sample_kernel, n_half=n_half),
        grid_spec=pltpu.PrefetchScalarGridSpec(
            num_scalar_prefetch=1,
            grid=(nb, n_pages // CMP_PAGES),
            in_specs=[page_spec(k) for k in range(CMP_PAGES)]
            + [pl.BlockSpec((2, CMP_LEN, HEAD_DIM), lambda b, s, pt: (0, 0, 0)),
               pl.BlockSpec((2, 2 * kh, HEAD_DIM), lambda b, s, pt: (0, 0, 0)),
               pl.BlockSpec((2, HEAD_DIM, HEAD_DIM), lambda b, s, pt: (0, 0, 0))],
            out_specs=pl.BlockSpec((None, KVC, n_half, HEAD_DIM), lambda b, s, pt: (b, 0, 0, 0)),
            scratch_shapes=[pltpu.VMEM((KVC, PAST_LEN, HEAD_DIM), F32), pltpu.VMEM((n_half, kh), CD),
                            pltpu.VMEM((n_half, kh), CD)]),
        out_shape=jax.ShapeDtypeStruct((nb, KVC, n_half, HEAD_DIM), CD),
        compiler_params=_params(("parallel", "arbitrary"), blk),
        name="nsa_compress_sample",
    )(page_table, *([pool] * CMP_PAGES), pe, w1, w2)


def _masked_softmax(s, mask):
    s = jnp.where(mask, s, NEG_INF)
    m = jnp.max(s, axis=-1, keepdims=True)
    e = jnp.where(mask, jnp.exp(s - m), 0.0)
    d = jnp.sum(e, axis=-1, keepdims=True)
    return e * (1.0 / jnp.where(d > 0.0, d, 1.0))


def _attend_bias(qs, k, v, bias, tq):
    s = _dot_nt(qs, k)
    es, ds = [], []
    for r in range(GROUP):
        sr = s[r * tq:(r + 1) * tq] * SCALE + bias
        e = jnp.exp(sr - jnp.max(sr, axis=-1, keepdims=True))
        ds.append(jnp.sum(e, axis=-1, keepdims=True))
        es.append(e.astype(CD))
    o = _dot(jnp.concatenate(es, axis=0), v)
    return o * (1.0 / jnp.concatenate(ds, axis=0))


def _group_sum_lanes(p, n_out):
    n = p.shape[1]
    src = lax.broadcasted_iota(jnp.int32, (n, n_out), 0) // PER_SEL
    dst = lax.broadcasted_iota(jnp.int32, (n, n_out), 1)
    gmat = (src == dst).astype(CD)
    h1 = p.astype(CD)
    r1 = p - h1.astype(F32)
    h2 = r1.astype(CD)
    h3 = (r1 - h2.astype(F32)).astype(CD)
    return _dot(h1, gmat) + _dot(h2, gmat) + _dot(h3, gmat)


def _stack_heads(q, col0):
    return jnp.concatenate([q[:, col0 + r * HEAD_DIM: col0 + (r + 1) * HEAD_DIM] for r in range(GROUP)], axis=0)


def _gate_sum(gates, col0, o_c, o_s, o_w, r, rows):
    return (gates[:, col0 + r:col0 + r + 1] * o_c[rows]
            + gates[:, col0 + GROUP + r:col0 + GROUP + r + 1] * o_s[rows]
            + gates[:, col0 + 2 * GROUP + r:col0 + 2 * GROUP + r + 1] * o_w[rows])


def _attn_prompt_kernel(q_ref, ks_ref, vs_ref, kw_ref, vw_ref, kc_ref, vc_ref, g_ref, o_ref, *, tq, T):
    i = pl.program_id(2)
    n_sel = T // SEL_LEN
    n_cmp = T // CMP_STRIDE

    qs = _stack_heads(q_ref[...], 0)
    qpos1 = i * tq + lax.broadcasted_iota(jnp.int32, (tq, 1), 0)
    qpos = jnp.concatenate([qpos1] * GROUP, axis=0)

    span = WINDOW + tq
    start = pl.multiple_of(jnp.maximum(i * tq - WINDOW, 0), tq)
    kposw = start + lax.broadcasted_iota(jnp.int32, (1, span), 1)
    bias_w = jnp.where((kposw <= qpos1) & (kposw > qpos1 - WINDOW), 0.0, NEG_INF)
    o_w = _attend_bias(qs, kw_ref[pl.ds(start, span), :], vw_ref[pl.ds(start, span), :], bias_w, tq)

    s = _dot_nt(qs, kc_ref[0, 0]) * SCALE
    blk_end = lax.broadcasted_iota(jnp.int32, (1, n_cmp), 1) * CMP_STRIDE + (CMP_LEN - 1)
    p = _masked_softmax(s, (blk_end <= qpos) & (blk_end < T))
    o_c = _dot(p.astype(CD), vc_ref[0, 0])
    psum = p[0:tq]
    for r in range(1, GROUP):
        psum = psum + p[r * tq:(r + 1) * tq]
    imp = _group_sum_lanes(psum, LANES)

    j = lax.broadcasted_iota(jnp.int32, (1, LANES), 1)
    cur = qpos1 // SEL_LEN
    forced = (j == 0) | (j == cur) | (j == cur - 1)
    score = jnp.where(forced, FORCE, imp)
    score = jnp.where(j <= cur, score, -1.0)
    sc = score.T[:n_sel]
    jrow = lax.broadcasted_iota(jnp.int32, (n_sel, 1), 0)
    rank = jnp.zeros((n_sel, tq), jnp.int32)
    for a in range(n_sel):
        sa = sc[a:a + 1, :]
        rank = rank + ((sa > sc) | ((sa == sc) & (a < jrow))).astype(jnp.int32)
    sel_t = ((rank < SEL_TOP) & (sc >= 0.0)).astype(F32)
    if n_sel < LANES:
        sel_t = jnp.concatenate([sel_t, jnp.zeros((LANES - n_sel, tq), F32)], axis=0)
    sel = sel_t.T.astype(CD)

    def sel_branch(nk):
        def run():
            expand = (lax.broadcasted_iota(jnp.int32, (LANES, nk), 1) // SEL_LEN
                      == lax.broadcasted_iota(jnp.int32, (LANES, nk), 0)).astype(CD)
            key_sel = _dot(sel, expand)
            kpos = lax.broadcasted_iota(jnp.int32, (1, nk), 1)
            bias = jnp.where((key_sel > 0.5) & (kpos <= qpos1), 0.0, NEG_INF)
            return _attend_bias(qs, ks_ref[0:nk, :], vs_ref[0:nk, :], bias, tq)
        return run

    n_ext = T // SEL_KEY_STEP
    o_s = lax.switch((i * tq) // SEL_KEY_STEP, [sel_branch(SEL_KEY_STEP * (k + 1)) for k in range(n_ext)])

    g = g_ref[...]
    for r in range(GROUP):
        o = _gate_sum(g, 0, o_c, o_s, o_w, r, slice(r * tq, (r + 1) * tq))
        o_ref[:, r * HEAD_DIM:(r + 1) * HEAD_DIM] = o.astype(o_ref.dtype)


def _attn_prompt(q, kv_sel, kv_win, kc, gates):
    T = SEQ
    tq = Q_TILE
    assert T % tq == 0 and WINDOW % tq == 0 and WINDOW + tq <= T and T // SEL_LEN <= LANES
    assert T // CMP_STRIDE == LANES and T % SEL_KEY_STEP == 0 and SEL_KEY_STEP % tq == 0
    nq = T // tq
    gw = GROUP * HEAD_DIM
    blk = 2 * 4 * _nbytes((T, HEAD_DIM), CD) + 8 * _nbytes((GROUP * tq, T), F32) + 4 * _nbytes((tq, gw), F32)
    k_spec = pl.BlockSpec((T, HEAD_DIM), lambda b, g, i: (b, g))
    v_spec = pl.BlockSpec((T, HEAD_DIM), lambda b, g, i: (b, KV_HEADS + g))
    return pl.pallas_call(
        functools.partial(_attn_prompt_kernel, tq=tq, T=T),
        grid=(BATCH, KV_HEADS, nq),
        in_specs=[pl.BlockSpec((tq, gw), lambda b, g, i: (b * nq + i, g)),
                  k_spec, v_spec, k_spec, v_spec,
                  pl.BlockSpec((1, 1, T // CMP_STRIDE, HEAD_DIM), lambda b, g, i: (b, g, 0, 0)),
                  pl.BlockSpec((1, 1, T // CMP_STRIDE, HEAD_DIM), lambda b, g, i: (b, KV_HEADS + g, 0, 0)),
                  pl.BlockSpec((tq, LANES), lambda b, g, i: (b * nq + i, g))],
        out_specs=pl.BlockSpec((tq, gw), lambda b, g, i: (b * nq + i, g)),
        out_shape=jax.ShapeDtypeStruct((q.shape[0], ATTN_W), CD),
        compiler_params=_params(("parallel", "parallel", "parallel"), blk),
        name="nsa_prompt_attention",
    )(q, kv_sel, kv_sel, kv_win, kv_win, kc, kc, gates)


def _sample_cmp_kernel(q_ref, kc_ref, vc_ref, oc_ref, sel_ref, *, n_cmp_valid):
    T = DEC_SEQ
    n_cmp = kc_ref.shape[2]
    qs = _stack_heads(q_ref[...], 0).astype(CD)
    qpos1 = PAST_LEN + lax.broadcasted_iota(jnp.int32, (T, 1), 0)
    qpos = jnp.concatenate([qpos1] * GROUP, axis=0)
    s = _dot_nt(qs, kc_ref[0, 0]) * SCALE
    n = lax.broadcasted_iota(jnp.int32, (1, n_cmp), 1)
    p = _masked_softmax(s, (n * CMP_STRIDE + (CMP_LEN - 1) <= qpos) & (n < n_cmp_valid))
    oc_ref[0, 0] = _dot(p.astype(CD), vc_ref[0, 0])
    psum = p[0:T]
    for r in range(1, GROUP):
        psum = psum + p[r * T:(r + 1) * T]
    imp = _group_sum_lanes(psum, LANES)

    j = lax.broadcasted_iota(jnp.int32, (1, LANES), 1)
    cur = qpos1 // SEL_LEN
    score = jnp.where((j == 0) | (j == cur - 1), FORCE, imp)
    rank = jnp.zeros((T, LANES), jnp.int32)
    for sh in range(1, LANES):
        other = pltpu.roll(score, sh, 1)
        oj = jnp.where(j >= sh, j - sh, j - sh + LANES)
        rank = rank + ((other > score) | ((other == score) & (oj < j))).astype(jnp.int32)
    sel_ref[0, 0] = (rank < SEL_TOP - 1).astype(F32)


def _sample_cmp(q_s, kc_s, n_cmp_valid):
    nb = DEC_BATCH
    gw = GROUP * HEAD_DIM
    n_cmp = kc_s.shape[2]
    blk = 16 * _nbytes((GROUP * DEC_SEQ, n_cmp), F32) + 4 * _nbytes((n_cmp, HEAD_DIM), CD)
    return pl.pallas_call(
        functools.partial(_sample_cmp_kernel, n_cmp_valid=n_cmp_valid),
        grid=(nb, KV_HEADS),
        in_specs=[pl.BlockSpec((DEC_SEQ, gw), lambda b, g: (b, g)),
                  pl.BlockSpec((1, 1, n_cmp, HEAD_DIM), lambda b, g: (b, g, 0, 0)),
                  pl.BlockSpec((1, 1, n_cmp, HEAD_DIM), lambda b, g: (b, KV_HEADS + g, 0, 0))],
        out_specs=[pl.BlockSpec((1, 1, GROUP * DEC_SEQ, HEAD_DIM), lambda b, g: (b, g, 0, 0)),
                   pl.BlockSpec((1, 1, DEC_SEQ, LANES), lambda b, g: (b, g, 0, 0))],
        out_shape=[jax.ShapeDtypeStruct((nb, KV_HEADS, GROUP * DEC_SEQ, HEAD_DIM), F32),
                   jax.ShapeDtypeStruct((nb, KV_HEADS, DEC_SEQ, LANES), F32)],
        compiler_params=_params(("parallel", "parallel"), blk),
        name="nsa_sample_cmp_attention",
    )(q_s, kc_s, kc_s)


def _put_rows_kernel(src_ref, dst_any, o_ref):
    o_ref[...] = src_ref[...].astype(o_ref.dtype)


def _put_rows(dst, src, row_off):
    n, w = src.shape
    assert row_off % n == 0 and n % 16 == 0
    return pl.pallas_call(
        _put_rows_kernel,
        grid=(1,),
        in_specs=[pl.BlockSpec((n, w), lambda i: (0, 0)), pl.BlockSpec(memory_space=pl.ANY)],
        out_specs=pl.BlockSpec((n, w), lambda i: (row_off // n, 0)),
        out_shape=jax.ShapeDtypeStruct(dst.shape, dst.dtype),
        input_output_aliases={1: 0},
        compiler_params=_params(("arbitrary",), 4 * _nbytes((n, w), F32)),
        name="put_sample_rows",
    )(src, dst)


def _pad_rows(x, rows):
    return jnp.concatenate([x, jnp.zeros((rows - x.shape[0], x.shape[1]), x.dtype)], axis=0)


def _head_rows(ref, slot, n):
    return ref[pl.ds(slot, n, stride=KVC), :]


def _sample_sel_kernel(pt_ref, q_ref, sel_ref, *refs):
    page_refs = refs[:SEL_PAGES]
    new_ref, o_ref, m_sc, l_sc, acc_sc = refs[SEL_PAGES:]
    T = DEC_SEQ
    R = GROUP * T
    step = pl.program_id(1)
    nk = SEL_PAGES * PAGE_SIZE

    @pl.when(step == 0)
    def _():
        m_sc[...] = jnp.full(m_sc.shape, NEG_INF, F32)
        l_sc[...] = jnp.zeros(l_sc.shape, F32)
        acc_sc[...] = jnp.zeros(acc_sc.shape, F32)

    def update(g, s, mask, v):
        rows = slice(g * R, (g + 1) * R)
        m_prev = m_sc[rows]
        m_new = jnp.maximum(m_prev, jnp.max(jnp.where(mask, s, NEG_INF), axis=-1, keepdims=True))
        alpha = jnp.exp(m_prev - m_new)
        e = jnp.where(mask, jnp.exp(s - m_new), 0.0)
        l_sc[rows] = alpha * l_sc[rows] + jnp.sum(e, axis=-1, keepdims=True)
        acc_sc[rows] = alpha * acc_sc[rows] + _dot(e.astype(CD), v)
        m_sc[rows] = m_new

    q = q_ref[...]
    blk_of_key = step * (nk // SEL_LEN) + lax.broadcasted_iota(jnp.int32, (LANES, nk), 1) // SEL_LEN
    expand = (blk_of_key == lax.broadcasted_iota(jnp.int32, (LANES, nk), 0)).astype(CD)
    key_sel = _dot(sel_ref[...].reshape(KV_HEADS * T, LANES).astype(CD), expand)
    for g in range(KV_HEADS):
        qg = _stack_heads(q, g * GROUP * HEAD_DIM).astype(CD)
        k = jnp.concatenate([_head_rows(pr, g, PAGE_SIZE) for pr in page_refs], axis=0).astype(CD)
        v = jnp.concatenate([_head_rows(pr, KV_HEADS + g, PAGE_SIZE) for pr in page_refs], axis=0).astype(CD)
        mask = jnp.concatenate([key_sel[g * T:(g + 1) * T]] * GROUP, axis=0) > 0.5
        update(g, _dot_nt(qg, k) * SCALE, mask, v)

    @pl.when(step == pl.num_programs(1) - 1)
    def _():
        t = jnp.concatenate([lax.broadcasted_iota(jnp.int32, (T, 1), 0)] * GROUP, axis=0)
        i = lax.broadcasted_iota(jnp.int32, (1, LANES), 1)
        mask = (i < T) & (i <= t)
        for g in range(KV_HEADS):
            qg = _stack_heads(q, g * GROUP * HEAD_DIM).astype(CD)
            k = _pad_rows(_head_rows(new_ref, g, T), LANES).astype(CD)
            v = _pad_rows(_head_rows(new_ref, KV_HEADS + g, T), LANES).astype(CD)
            update(g, _dot_nt(qg, k) * SCALE, mask, v)
        o_ref[...] = (acc_sc[...] * (1.0 / l_sc[...])).reshape(o_ref.shape)


def _sample_sel(q_s, sel, pool, layer, page_table, new_rows):
    nb, n_pages = page_table.shape
    assert n_pages % SEL_PAGES == 0 and (SEL_PAGES * PAGE_SIZE) % SEL_LEN == 0
    R = GROUP * DEC_SEQ
    page_rows = PAGE_SIZE * KVC
    blk = 2 * SEL_PAGES * _nbytes((page_rows, HEAD_DIM), F32) + 16 * _nbytes((R, SEL_PAGES * PAGE_SIZE), F32) \
        + 8 * _nbytes((SEL_PAGES * PAGE_SIZE, HEAD_DIM), F32)

    def page_spec(k):
        return pl.BlockSpec((None, None, page_rows, HEAD_DIM),
                            lambda b, s, pt: (layer, pt[b, s * SEL_PAGES + k], 0, 0))

    return pl.pallas_call(
        _sample_sel_kernel,
        grid_spec=pltpu.PrefetchScalarGridSpec(
            num_scalar_prefetch=1,
            grid=(nb, n_pages // SEL_PAGES),
            in_specs=[pl.BlockSpec((DEC_SEQ, ATTN_W), lambda b, s, pt: (b, 0)),
                      pl.BlockSpec((None, KV_HEADS, DEC_SEQ, LANES), lambda b, s, pt: (b, 0, 0, 0))]
            + [page_spec(k) for k in range(SEL_PAGES)]
            + [pl.BlockSpec((DEC_SEQ * KVC, HEAD_DIM), lambda b, s, pt: (b, 0))],
            out_specs=pl.BlockSpec((None, KV_HEADS, R, HEAD_DIM), lambda b, s, pt: (b, 0, 0, 0)),
            scratch_shapes=[pltpu.VMEM((KV_HEADS * R, 1), F32), pltpu.VMEM((KV_HEADS * R, 1), F32),
                            pltpu.VMEM((KV_HEADS * R, HEAD_DIM), F32)]),
        out_shape=jax.ShapeDtypeStruct((nb, KV_HEADS, R, HEAD_DIM), F32),
        compiler_params=_params(("parallel", "arbitrary"), blk),
        name="nsa_sample_sel_attention",
    )(page_table, q_s, sel, *([pool] * SEL_PAGES), new_rows)


def _sample_win_kernel(q_ref, buf_ref, new_ref, oc_ref, os_ref, g_ref, o_ref, *, wb):
    T = DEC_SEQ
    q = q_ref[...]
    gates = g_ref[...]
    t = jnp.concatenate([lax.broadcasted_iota(jnp.int32, (T, 1), 0)] * GROUP, axis=0)
    qpos = PAST_LEN + t
    i = lax.broadcasted_iota(jnp.int32, (1, wb + LANES), 1)
    kpos = PAST_LEN - wb + i
    mask = (i < wb + T) & (kpos <= qpos) & (kpos > qpos - WINDOW)
    for g in range(KV_HEADS):
        qg = _stack_heads(q, g * GROUP * HEAD_DIM).astype(CD)
        k = jnp.concatenate([_head_rows(buf_ref, g, wb), _pad_rows(_head_rows(new_ref, g, T), LANES)],
                            axis=0).astype(CD)
        v = jnp.concatenate([_head_rows(buf_ref, KV_HEADS + g, wb),
                             _pad_rows(_head_rows(new_ref, KV_HEADS + g, T), LANES)], axis=0).astype(CD)
        p = _masked_softmax(_dot_nt(qg, k) * SCALE, mask)
        o_w = _dot(p.astype(CD), v)
        o_c = oc_ref[g]
        o_s = os_ref[g]
        for r in range(GROUP):
            o = _gate_sum(gates, g * LANES, o_c, o_s, o_w, r, slice(r * T, (r + 1) * T))
            col = (g * GROUP + r) * HEAD_DIM
            o_ref[:, col:col + HEAD_DIM] = o


def _sample_win(q_s, win_buf, layer, new_rows, o_c, o_s, gates_s):
    nb = DEC_BATCH
    wb = win_buf.shape[2] // KVC
    R = GROUP * DEC_SEQ
    assert wb % LANES == 0
    blk = 2 * _nbytes((wb * KVC, HEAD_DIM), F32) + 8 * _nbytes((R, wb + LANES), F32) \
        + 8 * _nbytes((wb + LANES, HEAD_DIM), F32)
    return pl.pallas_call(
        functools.partial(_sample_win_kernel, wb=wb),
        grid=(nb,),
        in_specs=[pl.BlockSpec((DEC_SEQ, ATTN_W), lambda b: (b, 0)),
                  pl.BlockSpec((None, None, wb * KVC, HEAD_DIM), lambda b: (layer, b, 0, 0)),
                  pl.BlockSpec((DEC_SEQ * KVC, HEAD_DIM), lambda b: (b, 0)),
                  pl.BlockSpec((None, KV_HEADS, R, HEAD_DIM), lambda b: (b, 0, 0, 0)),
                  pl.BlockSpec((None, KV_HEADS, R, HEAD_DIM), lambda b: (b, 0, 0, 0)),
                  pl.BlockSpec((DEC_SEQ, KV_HEADS * LANES), lambda b: (b, 0))],
        out_specs=pl.BlockSpec((DEC_SEQ, ATTN_W), lambda b: (b, 0)),
        out_shape=jax.ShapeDtypeStruct((nb * DEC_SEQ, ATTN_W), F32),
        compiler_params=_params(("parallel",), blk),
        name="nsa_sample_win_attention",
    )(q_s, win_buf, new_rows, o_c, o_s, gates_s)


def _half_ffn(h, hn, w_gate, w_up, w_down, layer, g_next):
    act = _swiglu(hn, w_gate, w_up, layer)
    return _resid_proj(act, w_down, layer, h, 0.5, g_next, ROW_TILE // 2, COL_TILE // 2, "ffn_down")


def _prep_w_in(w_in):
    o1 = ATTN_W
    o4 = o1 + 6 * KV_W
    o5 = o4 + 3 * N_HEADS
    o6 = o5 + GM_WIDTH
    o7 = o6 + GM_WIDTH
    D = w_in.shape[1]
    wg = w_in[:, :, o4:o5].reshape(DEPTH, D, 3, KV_HEADS, GROUP).transpose(0, 1, 3, 2, 4)
    wg = jnp.pad(wg.reshape(DEPTH, D, KV_HEADS, 3 * GROUP), ((0, 0), (0, 0), (0, 0), (0, LANES - 3 * GROUP)))
    kv = [w_in[:, :, o1 + s * 2 * KV_W:o1 + (s + 1) * 2 * KV_W].astype(CD) for s in range(3)]
    return dict(q=w_in[:, :, :o1].astype(CD), kv=kv, gate=wg.reshape(DEPTH, D, KV_HEADS * LANES).astype(CD),
                u=w_in[:, :, o5:o6].astype(CD), v=w_in[:, :, o6:o7].astype(CD), mg=w_in[:, :, o7:].astype(CD))


def _layer(h, hn, layer, W, small, caches, page_table, kv_bufs, g_after):
    (nm, pe, w1, w2, ws, bs, n2) = small
    cache_cmp, cache_sel, cache_win = caches
    MP = BATCH * SEQ
    MS = DEC_BATCH * DEC_SEQ
    n_pages = page_table.shape[1]
    assert n_pages * PAGE_SIZE == PAST_LEN and PAST_LEN // SEL_LEN == LANES and DEC_SEQ <= SEL_LEN
    assert DEC_SEQ == SUBLANES and MP % MS == 0

    h, n = _half_ffn(h, hn, W["g1"], W["u1"], W["d1"], layer, nm)

    q = _proj(n, W["q"], layer, None, CD, "proj_q")
    gates = _proj(n, W["gate"], layer, "sigmoid", F32, "proj_branch_gates")
    cmp_buf, kv_c = _kv_proj_prompt(n, W["kv"][0], layer, kv_bufs[0], F32, False, "proj_kv_cmp")
    sel_buf, kv_sel = _kv_proj_prompt(n, W["kv"][1], layer, kv_bufs[1], CD, False, "proj_kv_sel")
    win_buf, kv_win = _kv_proj_prompt(n, W["kv"][2], layer, kv_bufs[2], CD, True, "proj_kv_win")
    new_s = [_kv_proj_sample(n, W["kv"][s], layer, "proj_kv_sample") for s in range(3)]

    w1c = w1.astype(CD).reshape(2, CMP_LEN * HEAD_DIM, HEAD_DIM)
    w2c = w2.astype(CD)

    kc_p = _compress_prompt(kv_c, pe, w1c, w2c)
    o_p = _attn_prompt(q, kv_sel, kv_win, kc_p, gates)
    (gm,) = _gmlp(n, W["u"], W["v"], layer, ws, bs[:, :, None], MP, 0, 1024, GM_CHUNK, GM_CHUNK, None, False)

    q_s = q[MP:].astype(F32)
    gates_s = gates[MP:]
    kc_s = _compress_sample(cache_cmp, layer, page_table, pe, w1c, w2c)
    n_cmp_valid = (PAST_LEN + DEC_SEQ - CMP_LEN) // CMP_STRIDE + 1
    oc_s, sel_s = _sample_cmp(q_s, kc_s, n_cmp_valid)
    os_s = _sample_sel(q_s, sel_s, cache_sel, layer, page_table, new_s[1])
    o_s = _sample_win(q_s, cache_win, layer, new_s[2], oc_s, os_s, gates_s)
    cl = min(DEC_SEQ, GM_CHUNK)
    ws_s = jnp.tile(ws[:, :cl, :cl], (1, MS // cl, MS // cl))
    bs_s = jnp.tile(bs[:, :cl], (1, MS // cl))[:, :, None]
    gm, v_s = _gmlp(n, W["u"], W["v"], layer, ws_s, bs_s, MS, MP, MS, MS, cl, gm, True)

    o = _put_rows(o_p, o_s, MP)
    merged = _merge(n, o, gm, W["mg"], W["wua"], W["wug"], layer)
    h, n = _resid_proj(merged, W["wo"], layer, h, 1.0, n2, ROW_TILE, COL_TILE, "mixer_out")
    h, hn = _half_ffn(h, n, W["g2"], W["u2"], W["d2"], layer, g_after)
    return h, hn, (cmp_buf, sel_buf, win_buf), (new_s, v_s)


def kernel(x_prompt, x_sample, cache_cmp_kv, cache_sel_kv, cache_win_kv, page_table, norm_ffn1, ffn1_gate, ffn1_up, ffn1_down, norm_mix, w_in, cmp_pe, cmp_w1, cmp_w2, gm_ws, gm_bs, w_up_attn, w_up_gm, w_out, norm_ffn2, ffn2_gate, ffn2_up, ffn2_down, norm_final):
    MP = BATCH * SEQ
    MS = DEC_BATCH * DEC_SEQ
    kv_row = (2, KV_HEADS, HEAD_DIM)
    h = jnp.concatenate([x_prompt.reshape(MP, D_MODEL), x_sample.reshape(MS, D_MODEL)], axis=0)
    n_phys = cache_cmp_kv.shape[1]
    wb = cache_win_kv.shape[2]
    caches = (cache_cmp_kv.reshape(DEPTH, n_phys, PAGE_SIZE * KVC, HEAD_DIM),
              cache_sel_kv.reshape(DEPTH, n_phys, PAGE_SIZE * KVC, HEAD_DIM),
              cache_win_kv.reshape(DEPTH, DEC_BATCH, wb * KVC, HEAD_DIM))
    W = _prep_w_in(w_in)
    W.update(g1=ffn1_gate, u1=ffn1_up, d1=ffn1_down.astype(CD),
             g2=ffn2_gate, u2=ffn2_up, d2=ffn2_down.astype(CD),
             wua=w_up_attn.astype(CD), wug=w_up_gm.astype(CD), wo=w_out.astype(CD))
    kv_bufs = (None, None, None)
    sample_outs = []
    hn = _norm_prep(h, norm_ffn1[0])
    for l in range(DEPTH):
        small = (norm_mix[l], cmp_pe[l], cmp_w1[l], cmp_w2[l], gm_ws[l], gm_bs[l], norm_ffn2[l])
        g_after = norm_ffn1[l + 1] if l + 1 < DEPTH else None
        h, hn, kv_bufs, s_out = _layer(h, hn, l, W, small, caches, page_table, kv_bufs, g_after)
        sample_outs.append(s_out)
    y_prompt = _rmsnorm(h, norm_final, F32, rows=MP, row_off=0).reshape(BATCH, SEQ, D_MODEL)
    y_sample = _rmsnorm(h, norm_final, F32, rows=MS, row_off=MP, tile=MS).reshape(DEC_BATCH, DEC_SEQ, D_MODEL)

    cmp_buf, sel_buf, win_buf = kv_bufs
    wk = min(WINDOW, SEQ)
    new = [jnp.stack([sample_outs[l][0][s] for l in range(DEPTH)]).reshape((DEPTH, DEC_BATCH, DEC_SEQ) + kv_row)
           for s in range(3)]
    win_all = jnp.concatenate([cache_win_kv, new[2]], axis=2)
    win_s = win_all[:, :, -min(WINDOW, win_all.shape[2]):]
    gm_v = jnp.stack([sample_outs[l][1] for l in range(DEPTH)]).reshape(DEPTH, DEC_BATCH, DEC_SEQ, GM_WIDTH)
    return (y_prompt, y_sample,
            cmp_buf.reshape((DEPTH, BATCH, SEQ) + kv_row), new[0],
            sel_buf.reshape((DEPTH, BATCH, SEQ) + kv_row), new[1],
            win_buf.reshape((DEPTH, BATCH, wk) + kv_row), win_s, gm_v)
```

```python
import functools

import jax
import jax.numpy as jnp
from jax import lax
from jax.experimental import pallas as pl
from jax.experimental.pallas import tpu as pltpu

D_MODEL = 4096
BATCH = 4
SEQ = 2048
DEPTH = 4
DEC_BATCH = 8
DEC_SEQ = 8
PAST_LEN = 8192
PAGE_SIZE = 128

N_HEADS = 16
KV_HEADS = 4
HEAD_DIM = 128
GROUP = N_HEADS // KV_HEADS
ATTN_W = N_HEADS * HEAD_DIM
KV_W = KV_HEADS * HEAD_DIM
CMP_LEN = 32
CMP_STRIDE = 16
SEL_LEN = 64
SEL_TOP = 16
WINDOW = 512
GM_WIDTH = D_MODEL // 2
GM_GROUPS = 4
GM_CHUNK = 128
D_FF = 11008
EPS = 1e-6
NEG_INF = -1e30
FORCE = 1e9
SCALE = HEAD_DIM ** -0.5
LOG2E = 1.4426950408889634

LANES = 128
SUBLANES = 8
VMEM_BUDGET = 56 * 1024 * 1024
CD = jnp.bfloat16
F32 = jnp.float32
PER_SEL = SEL_LEN // CMP_STRIDE
KVC = 2 * KV_HEADS

ROW_TILE = 1376
COL_TILE = 512
Q_TILE = 256
SEL_KEY_STEP = 256
KV_ROW_TILE = 512
SEL_PAGES = 8
CMP_PAGES = 8


def _pick_tile(n, target, mult):
    best = n
    for t in range(mult, min(n, target) + 1, mult):
        if n % t == 0:
            best = t
    return best


def _params(sem, block_bytes):
    limit = min(VMEM_BUDGET, max(16 * 1024 * 1024, int(block_bytes)))
    return pltpu.CompilerParams(dimension_semantics=sem, vmem_limit_bytes=limit)


def _nbytes(shape, dtype):
    n = 1
    for s in shape:
        n *= s
    return n * jnp.dtype(dtype).itemsize


def _dot(a, b):
    return jnp.dot(a, b, preferred_element_type=F32)


def _dot_nt(a, b):
    return lax.dot_general(a, b, (((1,), (1,)), ((), ())), preferred_element_type=F32)


def _rms_kernel(x_ref, g_ref, o_ref):
    x = x_ref[...]
    y = x * lax.rsqrt(jnp.mean(x * x, axis=-1, keepdims=True) + EPS)
    o_ref[...] = (y * g_ref[...]).astype(o_ref.dtype)


def _rmsnorm(x, g, out_dtype, rows=None, row_off=0, tile=None):
    M, D = x.shape
    rows = M if rows is None else rows
    tm = tile or _pick_tile(rows, 344, 16)
    assert rows % tm == 0 and row_off % tm == 0
    off = row_off // tm
    blk = 2 * _nbytes((tm, D), F32) * 2 + 2 * _nbytes((tm, D), out_dtype)
    return pl.pallas_call(
        _rms_kernel,
        grid=(rows // tm,),
        in_specs=[pl.BlockSpec((tm, D), lambda i: (i + off, 0)),
                  pl.BlockSpec((1, D), lambda i: (0, 0))],
        out_specs=pl.BlockSpec((tm, D), lambda i: (i, 0)),
        out_shape=jax.ShapeDtypeStruct((rows, D), out_dtype),
        compiler_params=_params(("parallel",), blk),
        name="rmsnorm",
    )(x, g.reshape(1, D))


def _inv_rms(ss_ref):
    return lax.rsqrt(ss_ref[...] * (1.0 / D_MODEL) + EPS)


def _ss_spec(ss, tm, off=0):
    return pl.BlockSpec((tm, 1), lambda i, *_: (i + off, 0))


def _norm_prep_kernel(x_ref, g_ref, hb_ref, ss_ref):
    x = x_ref[...]
    hb_ref[...] = (x * g_ref[...]).astype(hb_ref.dtype)
    ss_ref[...] = jnp.sum(x * x, axis=-1, keepdims=True)


def _norm_prep(x, g):
    M, D = x.shape
    tm = _pick_tile(M, 344, 16)
    blk = 4 * _nbytes((tm, D), F32) + 2 * _nbytes((tm, D), CD)
    return pl.pallas_call(
        _norm_prep_kernel,
        grid=(M // tm,),
        in_specs=[pl.BlockSpec((tm, D), lambda i: (i, 0)),
                  pl.BlockSpec((1, D), lambda i: (0, 0))],
        out_specs=[pl.BlockSpec((tm, D), lambda i: (i, 0)), pl.BlockSpec((tm, 1), lambda i: (i, 0))],
        out_shape=[jax.ShapeDtypeStruct((M, D), CD), jax.ShapeDtypeStruct((M, 1), F32)],
        compiler_params=_params(("parallel",), blk),
        name="norm_prep",
    )(x, g.reshape(1, D))


def _proj_kernel(x_ref, ss_ref, w_ref, o_ref, *, act):
    z = _inv_rms(ss_ref) * _dot(x_ref[...], w_ref[...])
    if act == "sigmoid":
        z = jax.nn.sigmoid(z)
    o_ref[...] = z.astype(o_ref.dtype)


def _proj(xn, w, layer, act, out_dtype, name):
    x, ss = xn
    M, K = x.shape
    N = w.shape[2]
    tm = _pick_tile(M, ROW_TILE, 16)
    tn = _pick_tile(N, COL_TILE, LANES)
    blk = 2 * (_nbytes((tm, K), x.dtype) + _nbytes((K, tn), w.dtype) + _nbytes((tm, tn), out_dtype)) \
        + 4 * _nbytes((tm, tn), F32)
    return pl.pallas_call(
        functools.partial(_proj_kernel, act=act),
        grid=(M // tm, N // tn),
        in_specs=[pl.BlockSpec((tm, K), lambda i, j: (i, 0)), _ss_spec(ss, tm),
                  pl.BlockSpec((None, K, tn), lambda i, j: (layer, 0, j))],
        out_specs=pl.BlockSpec((tm, tn), lambda i, j: (i, j)),
        out_shape=jax.ShapeDtypeStruct((M, N), out_dtype),
        compiler_params=_params(("parallel", "parallel"), blk),
        name=name,
    )(x, ss, w)


def _swiglu_kernel(x_ref, ss_ref, wg_ref, wu_ref, o_ref):
    x = x_ref[...]
    inv = _inv_rms(ss_ref)
    a = inv * _dot(x, wg_ref[...].astype(CD))
    b = inv * _dot(x, wu_ref[...].astype(CD))
    o_ref[...] = (jax.nn.silu(a) * b).astype(o_ref.dtype)


def _swiglu(xn, wg, wu, layer):
    x, ss = xn
    M, K = x.shape
    N = wg.shape[2]
    tm = _pick_tile(M, ROW_TILE, 16)
    tn = _pick_tile(N, COL_TILE // 2, LANES)
    blk = 2 * (_nbytes((tm, K), x.dtype) + 2 * _nbytes((K, tn), wg.dtype) + _nbytes((tm, tn), CD)) \
        + 2 * _nbytes((K, tn), CD) + 6 * _nbytes((tm, tn), F32)
    w_spec = pl.BlockSpec((None, K, tn), lambda i, j: (layer, 0, j))
    return pl.pallas_call(
        _swiglu_kernel,
        grid=(M // tm, N // tn),
        in_specs=[pl.BlockSpec((tm, K), lambda i, j: (i, 0)), _ss_spec(ss, tm), w_spec, w_spec],
        out_specs=pl.BlockSpec((tm, tn), lambda i, j: (i, j)),
        out_shape=jax.ShapeDtypeStruct((M, N), CD),
        compiler_params=_params(("parallel", "parallel"), blk),
        name="ffn_swiglu",
    )(x, ss, wg, wu)


def _resid_kernel(x_ref, w_ref, h_ref, *refs, scale, emit_norm):
    hn = h_ref[...] + scale * _dot(x_ref[...], w_ref[...])
    if emit_norm:
        g_ref, o_ref, hb_ref, ss_ref = refs
        hb_ref[...] = (hn * g_ref[...]).astype(hb_ref.dtype)
        part = jnp.sum(hn * hn, axis=-1, keepdims=True)
        j = pl.program_id(1)

        @pl.when(j == 0)
        def _():
            ss_ref[...] = part

        @pl.when(j > 0)
        def _():
            ss_ref[...] += part
    else:
        (o_ref,) = refs
    o_ref[...] = hn


def _resid_proj(x, w, layer, h, scale, g_next, tm_target, tn_target, name):
    M, K = x.shape
    N = w.shape[2]
    tm = _pick_tile(M, tm_target, 16)
    tn = _pick_tile(N, tn_target, LANES)
    nj = N // tn
    emit = g_next is not None
    blk = 2 * (_nbytes((tm, K), x.dtype) + _nbytes((K, tn), w.dtype) + 3 * _nbytes((tm, tn), F32)) \
        + 4 * _nbytes((tm, tn), F32)
    tile = pl.BlockSpec((tm, tn), lambda i, j: (i, j))
    in_specs = [pl.BlockSpec((tm, K), lambda i, j: (i, 0)),
                pl.BlockSpec((None, K, tn), lambda i, j: (layer, 0, j)), tile]
    args = [x, w, h]
    out_specs = [tile]
    out_shape = [jax.ShapeDtypeStruct((M, N), F32)]
    if emit:
        in_specs.append(pl.BlockSpec((1, tn), lambda i, j: (0, j)))
        args.append(g_next.reshape(1, N))
        out_specs += [tile, pl.BlockSpec((tm, 1), lambda i, j: (i, 0))]
        out_shape += [jax.ShapeDtypeStruct((M, N), CD), jax.ShapeDtypeStruct((M, 1), F32)]
    outs = pl.pallas_call(
        functools.partial(_resid_kernel, scale=scale, emit_norm=emit),
        grid=(M // tm, nj),
        in_specs=in_specs,
        out_specs=out_specs,
        out_shape=out_shape,
        compiler_params=_params(("parallel", "arbitrary"), blk),
        name=name,
    )(*args)
    return (outs[0], (outs[1], outs[2])) if emit else (outs[0], None)


def _merge_kernel(n_ref, ss_ref, o_ref, gm_ref, wma_ref, wmg_ref, wa_ref, wg_ref, out_ref):
    n = n_ref[...]
    inv = _inv_rms(ss_ref)
    ga = jax.nn.sigmoid(inv * _dot(n, wma_ref[...]))
    gg = jax.nn.sigmoid(inv * _dot(n, wmg_ref[...]))
    a = _dot(o_ref[...], wa_ref[...])
    g = _dot(gm_ref[...], wg_ref[...])
    out_ref[...] = (ga * a + gg * g).astype(out_ref.dtype)


def _merge(xn, o, gm, w_mg, wa, wg, layer):
    n, ss = xn
    M, K = n.shape
    Ka = o.shape[1]
    Kg = gm.shape[1]
    N = wa.shape[2]
    tm = _pick_tile(M, ROW_TILE // 2, 16)
    tn = _pick_tile(N, COL_TILE // 2, LANES)
    nj = N // tn
    blk = 2 * (_nbytes((tm, K + Ka + Kg), CD) + _nbytes((2 * K + Ka + Kg, tn), CD) + _nbytes((tm, tn), CD)) \
        + 8 * _nbytes((tm, tn), F32)
    return pl.pallas_call(
        _merge_kernel,
        grid=(M // tm, nj),
        in_specs=[pl.BlockSpec((tm, K), lambda i, j: (i, 0)), _ss_spec(ss, tm),
                  pl.BlockSpec((tm, Ka), lambda i, j: (i, 0)),
                  pl.BlockSpec((tm, Kg), lambda i, j: (i, 0)),
                  pl.BlockSpec((None, K, tn), lambda i, j: (layer, 0, j)),
                  pl.BlockSpec((None, K, tn), lambda i, j: (layer, 0, j + nj)),
                  pl.BlockSpec((None, Ka, tn), lambda i, j: (layer, 0, j)),
                  pl.BlockSpec((None, Kg, tn), lambda i, j: (layer, 0, j))],
        out_specs=pl.BlockSpec((tm, tn), lambda i, j: (i, j)),
        out_shape=jax.ShapeDtypeStruct((M, N), CD),
        compiler_params=_params(("parallel", "parallel"), blk),
        name="mixer_merge",
    )(n, ss, o, gm, w_mg, w_mg, wa, wg)


def _kv_kernel(*refs, has_prev, has_std, last_only):
    x_ref, ss_ref, w_ref = refs[0], refs[1], refs[2]
    il_ref = refs[3 + has_prev]
    tm = x_ref.shape[0]
    z = _inv_rms(ss_ref) * _dot(x_ref[...], w_ref[...])
    if has_std:
        std_ref = refs[4 + has_prev]
        std_ref[...] = z.astype(std_ref.dtype)

    def write():
        for c in range(KVC):
            il_ref[pl.ds(c, tm, stride=KVC), :] = z[:, c * HEAD_DIM:(c + 1) * HEAD_DIM]

    if last_only:
        tiles_per_batch = SEQ // tm
        pl.when(pl.program_id(0) % tiles_per_batch == tiles_per_batch - 1)(write)
    else:
        write()


def _kv_proj_prompt(xn, w, layer, prev, std_dtype, last_only, name):
    x, ss = xn
    K = x.shape[1]
    W = 2 * KV_W
    MP = BATCH * SEQ
    tm = KV_ROW_TILE
    assert SEQ % tm == 0 and (not last_only or tm == min(WINDOW, SEQ))
    if last_only:
        il_shape = (DEPTH, BATCH, tm * KVC, HEAD_DIM)
        il_spec = pl.BlockSpec((None, None, tm * KVC, HEAD_DIM), lambda i: (layer, i // (SEQ // tm), 0, 0))
    else:
        il_shape = (DEPTH, MP * KVC, HEAD_DIM)
        il_spec = pl.BlockSpec((None, tm * KVC, HEAD_DIM), lambda i: (layer, i, 0))
    in_specs = [pl.BlockSpec((tm, K), lambda i: (i, 0)), _ss_spec(ss, tm),
                pl.BlockSpec((None, K, W), lambda i: (layer, 0, 0))]
    args = [x, ss, w]
    aliases = {}
    if prev is not None:
        in_specs.append(pl.BlockSpec(memory_space=pl.ANY))
        args.append(prev)
        aliases = {3: 0}
    blk = 2 * (_nbytes((tm, K), CD) + _nbytes((K, W), CD) + 2 * _nbytes((tm, W), F32)) + 3 * _nbytes((tm, W), F32)
    return pl.pallas_call(
        functools.partial(_kv_kernel, has_prev=prev is not None, has_std=True, last_only=last_only),
        grid=(MP // tm,),
        in_specs=in_specs,
        out_specs=[il_spec, pl.BlockSpec((tm, W), lambda i: (i, 0))],
        out_shape=[jax.ShapeDtypeStruct(il_shape, F32), jax.ShapeDtypeStruct((MP, W), std_dtype)],
        input_output_aliases=aliases,
        compiler_params=_params(("arbitrary",), blk),
        name=name,
    )(*args)


def _kv_proj_sample(xn, w, layer, name):
    x, ss = xn
    K = x.shape[1]
    W = 2 * KV_W
    MP = BATCH * SEQ
    MS = DEC_BATCH * DEC_SEQ
    assert MP % MS == 0
    blk = 2 * (_nbytes((MS, K), CD) + _nbytes((K, W), CD) + _nbytes((MS, W), F32)) + 3 * _nbytes((MS, W), F32)
    return pl.pallas_call(
        functools.partial(_kv_kernel, has_prev=False, has_std=False, last_only=False),
        grid=(1,),
        in_specs=[pl.BlockSpec((MS, K), lambda i: (MP // MS, 0)), _ss_spec(ss, MS, MP // MS),
                  pl.BlockSpec((None, K, W), lambda i: (layer, 0, 0))],
        out_specs=pl.BlockSpec((MS * KVC, HEAD_DIM), lambda i: (0, 0)),
        out_shape=jax.ShapeDtypeStruct((MS * KVC, HEAD_DIM), F32),
        compiler_params=_params(("arbitrary",), blk),
        name=name,
    )(x, ss, w)


def _gmlp_kernel(x_ref, ss_ref, wu_ref, wv_ref, ws_ref, b_ref, *refs, chunk, has_prev, want_v):
    gm_ref = refs[has_prev]
    R = ws_ref.shape[1]
    tm = x_ref.shape[0]
    x = x_ref[...]
    inv = _inv_rms(ss_ref)
    u = jax.nn.gelu(inv * _dot(x, wu_ref[...]))
    v = jax.nn.gelu(inv * _dot(x, wv_ref[...]))
    if want_v:
        refs[has_prev + 1][...] = v
    i = lax.broadcasted_iota(jnp.int32, (R, R), 0)
    j = lax.broadcasted_iota(jnp.int32, (R, R), 1)
    w = jnp.where((j <= i) & (i // chunk == j // chunk), ws_ref[0], 0.0).astype(CD)
    vc = v.astype(CD)
    for c in range(tm // R):
        rows = slice(c * R, (c + 1) * R)
        mixed = _dot(w, vc[rows]) + b_ref[0]
        gm_ref[rows, :] = (u[rows] * mixed).astype(gm_ref.dtype)


def _gmlp(xn, wu, wv, layer, ws, bs, n_rows, row_off, tm, R, chunk, prev, want_v):
    x, ss = xn
    M, K = x.shape
    cw = GM_WIDTH // GM_GROUPS
    assert n_rows % tm == 0 and row_off % tm == 0 and tm % R == 0 and cw % LANES == 0
    off = row_off // tm
    blk = 2 * (_nbytes((tm, K), CD) + 2 * _nbytes((K, cw), CD) + 3 * _nbytes((tm, cw), F32)) \
        + 8 * _nbytes((tm, cw), F32)
    w_spec = pl.BlockSpec((None, K, cw), lambda i, g: (layer, 0, g))
    in_specs = [pl.BlockSpec((tm, K), lambda i, g: (i + off, 0)), _ss_spec(ss, tm, off), w_spec, w_spec,
                pl.BlockSpec((1, R, R), lambda i, g: (g, 0, 0)),
                pl.BlockSpec((1, R, 1), lambda i, g: (g, 0, 0))]
    args = [x, ss, wu, wv, ws, bs]
    aliases = {}
    if prev is not None:
        in_specs.append(pl.BlockSpec(memory_space=pl.ANY))
        args.append(prev)
        aliases = {6: 0}
    out_specs = [pl.BlockSpec((tm, cw), lambda i, g: (i + off, g))]
    out_shape = [jax.ShapeDtypeStruct((M, GM_WIDTH), CD)]
    if want_v:
        out_specs.append(pl.BlockSpec((tm, cw), lambda i, g: (i, g)))
        out_shape.append(jax.ShapeDtypeStruct((n_rows, GM_WIDTH), F32))
    return pl.pallas_call(
        functools.partial(_gmlp_kernel, chunk=chunk, has_prev=prev is not None, want_v=want_v),
        grid=(n_rows // tm, GM_GROUPS),
        in_specs=in_specs,
        out_specs=out_specs,
        out_shape=out_shape,
        input_output_aliases=aliases,
        compiler_params=_params(("parallel", "parallel"), blk),
        name="gmlp_branch",
    )(*args)


def _compress_rows(load_row, pe_l, w1_lo, w1_hi, w2, xlo, xhi, n_half):
    half = CMP_LEN // 2
    for l in range(half):
        xl = load_row(l)
        cols = slice(l * HEAD_DIM, (l + 1) * HEAD_DIM)
        xlo[:, cols] = (xl + pe_l(l)).astype(CD)
        xhi[:, cols] = (xl + pe_l(half + l)).astype(CD)
    hid = jax.nn.gelu(_dot(xlo[...], w1_lo) + pltpu.roll(_dot(xhi[...], w1_hi), n_half - 1, 0))
    return _dot(hid.astype(CD), w2)


def _compress_prompt_kernel(x_ref, pe_ref, w1_ref, w2_ref, o_ref, xlo, xhi, *, n_half):
    kh = (CMP_LEN // 2) * HEAD_DIM
    tok = _compress_rows(lambda l: x_ref[pl.ds(l, n_half, stride=CMP_STRIDE), :],
                         lambda l: pe_ref[0, l:l + 1, :], w1_ref[0, 0:kh, :], w1_ref[0, kh:2 * kh, :], w2_ref[0],
                         xlo, xhi, n_half)
    o_ref[0, 0] = tok.astype(o_ref.dtype)


def _compress_prompt(kv_c, pe, w1, w2):
    assert CMP_LEN == 2 * CMP_STRIDE and SEQ % (CMP_STRIDE * SUBLANES) == 0
    n_half = SEQ // CMP_STRIDE
    kh = (CMP_LEN // 2) * HEAD_DIM
    blk = 2 * _nbytes((SEQ, HEAD_DIM), F32) + 8 * _nbytes((n_half, HEAD_DIM), F32) \
        + 4 * _nbytes((2 * kh, HEAD_DIM), CD) + 2 * _nbytes((n_half, kh), CD)
    return pl.pallas_call(
        functools.partial(_compress_prompt_kernel, n_half=n_half),
        grid=(BATCH, KVC),
        in_specs=[pl.BlockSpec((SEQ, HEAD_DIM), lambda b, c: (b, c)),
                  pl.BlockSpec((1, CMP_LEN, HEAD_DIM), lambda b, c: (c // KV_HEADS, 0, 0)),
                  pl.BlockSpec((1, 2 * kh, HEAD_DIM), lambda b, c: (c // KV_HEADS, 0, 0)),
                  pl.BlockSpec((1, HEAD_DIM, HEAD_DIM), lambda b, c: (c // KV_HEADS, 0, 0))],
        out_specs=pl.BlockSpec((1, 1, n_half, HEAD_DIM), lambda b, c: (b, c, 0, 0)),
        out_shape=jax.ShapeDtypeStruct((BATCH, KVC, n_half, HEAD_DIM), CD),
        scratch_shapes=[pltpu.VMEM((n_half, kh), CD), pltpu.VMEM((n_half, kh), CD)],
        compiler_params=_params(("parallel", "parallel"), blk),
        name="nsa_compress_prompt",
    )(kv_c, pe, w1, w2)


def _compress_sample_kernel(pt_ref, *refs, n_half):
    page_refs = refs[:CMP_PAGES]
    pe_ref, w1_ref, w2_ref, o_ref, buf, xlo, xhi = refs[CMP_PAGES:]
    step = pl.program_id(1)
    page_rows = PAGE_SIZE * KVC
    kh = (CMP_LEN // 2) * HEAD_DIM
    for k, pr in enumerate(page_refs):
        start = pl.multiple_of((step * CMP_PAGES + k) * PAGE_SIZE, PAGE_SIZE)
        for c in range(KVC):
            buf[c, pl.ds(start, PAGE_SIZE), :] = _head_rows(pr, c, PAGE_SIZE)

    @pl.when(step == pl.num_programs(1) - 1)
    def _():
        def one_slot(c, carry):
            kv = c // KV_HEADS
            tok = _compress_rows(lambda l: buf[c, pl.ds(l, n_half, stride=CMP_STRIDE), :],
                                 lambda l: pe_ref[kv, pl.ds(l, 1), :], w1_ref[kv, pl.ds(0, kh), :],
                                 w1_ref[kv, pl.ds(kh, kh), :], w2_ref[kv], xlo, xhi, n_half)
            o_ref[c] = tok.astype(o_ref.dtype)
            return carry
        lax.fori_loop(0, KVC, one_slot, 0)


def _compress_sample(pool, layer, page_table, pe, w1, w2):
    nb, n_pages = page_table.shape
    assert n_pages % CMP_PAGES == 0 and PAST_LEN % (CMP_STRIDE * SUBLANES) == 0
    n_half = PAST_LEN // CMP_STRIDE
    page_rows = PAGE_SIZE * KVC
    kh = (CMP_LEN // 2) * HEAD_DIM
    blk = _nbytes((PAST_LEN * KVC, HEAD_DIM), F32) + 2 * CMP_PAGES * _nbytes((page_rows, HEAD_DIM), F32) \
        + 4 * _nbytes((2, 2 * kh, HEAD_DIM), CD) + 2 * _nbytes((KVC, n_half, HEAD_DIM), CD) \
        + 2 * _nbytes((n_half, kh), CD) + 10 * _nbytes((n_half, HEAD_DIM), F32)

    def page_spec(k):
        return pl.BlockSpec((None, None, page_rows, HEAD_DIM),
                            lambda b, s, pt: (layer, pt[b, s * CMP_PAGES + k], 0, 0))

    return pl.pallas_call(
        functools.partial(_compress_sample_kernel, n_half=n_half),
        grid_spec=pltpu.PrefetchScalarGridSpec(
            num_scalar_prefetch=1,
            grid=(nb, n_pages // CMP_PAGES),
            in_specs=[page_spec(k) for k in range(CMP_PAGES)]
            + [pl.BlockSpec((2, CMP_LEN, HEAD_DIM), lambda b, s, pt: (0, 0, 0)),
               pl.BlockSpec((2, 2 * kh, HEAD_DIM), lambda b, s, pt: (0, 0, 0)),
               pl.BlockSpec((2, HEAD_DIM, HEAD_DIM), lambda b, s, pt: (0, 0, 0))],
            out_specs=pl.BlockSpec((None, KVC, n_half, HEAD_DIM), lambda b, s, pt: (b, 0, 0, 0)),
            scratch_shapes=[pltpu.VMEM((KVC, PAST_LEN, HEAD_DIM), F32), pltpu.VMEM((n_half, kh), CD),
                            pltpu.VMEM((n_half, kh), CD)]),
        out_shape=jax.ShapeDtypeStruct((nb, KVC, n_half, HEAD_DIM), CD),
        compiler_params=_params(("parallel", "arbitrary"), blk),
        name="nsa_compress_sample",
    )(page_table, *([pool] * CMP_PAGES), pe, w1, w2)


def _masked_softmax(s, mask):
    s = jnp.where(mask, s, NEG_INF)
    m = jnp.max(s, axis=-1, keepdims=True)
    e = jnp.where(mask, jnp.exp(s - m), 0.0)
    d = jnp.sum(e, axis=-1, keepdims=True)
    return e * (1.0 / jnp.where(d > 0.0, d, 1.0))


def _attend_bias(qs, k, v, bias, tq):
    os = []
    for r in range(GROUP):
        t = _dot_nt(qs[r * tq:(r + 1) * tq], k) + bias
        e = jnp.exp2((t - jnp.max(t, axis=-1, keepdims=True)) * (SCALE * LOG2E))
        d = jnp.sum(e, axis=-1, keepdims=True)
        os.append(_dot(e.astype(CD), v) * (1.0 / d))
    return jnp.concatenate(os, axis=0)


def _group_sum_lanes(p, n_out):
    n = p.shape[1]
    src = lax.broadcasted_iota(jnp.int32, (n, n_out), 0) // PER_SEL
    dst = lax.broadcasted_iota(jnp.int32, (n, n_out), 1)
    gmat = (src == dst).astype(CD)
    h1 = p.astype(CD)
    r1 = p - h1.astype(F32)
    h2 = r1.astype(CD)
    h3 = (r1 - h2.astype(F32)).astype(CD)
    return _dot(h1, gmat) + _dot(h2, gmat) + _dot(h3, gmat)


def _stack_heads(q, col0):
    return jnp.concatenate([q[:, col0 + r * HEAD_DIM: col0 + (r + 1) * HEAD_DIM] for r in range(GROUP)], axis=0)


def _gate_sum(gates, col0, o_c, o_s, o_w, r, rows):
    return (gates[:, col0 + r:col0 + r + 1] * o_c[rows]
            + gates[:, col0 + GROUP + r:col0 + GROUP + r + 1] * o_s[rows]
            + gates[:, col0 + 2 * GROUP + r:col0 + 2 * GROUP + r + 1] * o_w[rows])


def _attn_prompt_kernel(q_ref, ks_ref, vs_ref, kw_ref, vw_ref, kc_ref, vc_ref, g_ref, o_ref, *, tq, T):
    i = pl.program_id(2)
    n_sel = T // SEL_LEN
    n_cmp = T // CMP_STRIDE

    qs = _stack_heads(q_ref[...], 0)
    qpos1 = i * tq + lax.broadcasted_iota(jnp.int32, (tq, 1), 0)
    qpos = jnp.concatenate([qpos1] * GROUP, axis=0)

    span = WINDOW + tq
    start = pl.multiple_of(jnp.maximum(i * tq - WINDOW, 0), tq)
    kposw = start + lax.broadcasted_iota(jnp.int32, (1, span), 1)
    bias_w = jnp.where((kposw <= qpos1) & (kposw > qpos1 - WINDOW), 0.0, NEG_INF)
    o_w = _attend_bias(qs, kw_ref[pl.ds(start, span), :], vw_ref[pl.ds(start, span), :], bias_w, tq)

    s = _dot_nt(qs, kc_ref[0, 0]) * SCALE
    blk_end = lax.broadcasted_iota(jnp.int32, (1, n_cmp), 1) * CMP_STRIDE + (CMP_LEN - 1)
    p = _masked_softmax(s, (blk_end <= qpos) & (blk_end < T))
    o_c = _dot(p.astype(CD), vc_ref[0, 0])
    psum = p[0:tq]
    for r in range(1, GROUP):
        psum = psum + p[r * tq:(r + 1) * tq]
    imp = _group_sum_lanes(psum, LANES)

    j = lax.broadcasted_iota(jnp.int32, (1, LANES), 1)
    cur = qpos1 // SEL_LEN
    forced = (j == 0) | (j == cur) | (j == cur - 1)
    score = jnp.where(forced, FORCE, imp)
    score = jnp.where(j <= cur, score, -1.0)
    sc = score.T[:n_sel]
    jrow = lax.broadcasted_iota(jnp.int32, (n_sel, 1), 0)
    rank = jnp.zeros((n_sel, tq), jnp.int32)
    for a in range(n_sel):
        sa = sc[a:a + 1, :]
        rank = rank + ((sa > sc) | ((sa == sc) & (a < jrow))).astype(jnp.int32)
    sel_t = ((rank < SEL_TOP) & (sc >= 0.0)).astype(F32)
    if n_sel < LANES:
        sel_t = jnp.concatenate([sel_t, jnp.zeros((LANES - n_sel, tq), F32)], axis=0)
    sel = sel_t.T.astype(CD)

    def sel_branch(nk):
        def run():
            expand = (lax.broadcasted_iota(jnp.int32, (LANES, nk), 1) // SEL_LEN
                      == lax.broadcasted_iota(jnp.int32, (LANES, nk), 0)).astype(CD)
            key_sel = _dot(sel, expand)
            kpos = lax.broadcasted_iota(jnp.int32, (1, nk), 1)
            bias = jnp.where((key_sel > 0.5) & (kpos <= qpos1), 0.0, NEG_INF)
            return _attend_bias(qs, ks_ref[0:nk, :], vs_ref[0:nk, :], bias, tq)
        return run

    n_ext = T // SEL_KEY_STEP
    o_s = lax.switch((i * tq) // SEL_KEY_STEP, [sel_branch(SEL_KEY_STEP * (k + 1)) for k in range(n_ext)])

    g = g_ref[...]
    for r in range(GROUP):
        o = _gate_sum(g, 0, o_c, o_s, o_w, r, slice(r * tq, (r + 1) * tq))
        o_ref[:, r * HEAD_DIM:(r + 1) * HEAD_DIM] = o.astype(o_ref.dtype)


def _attn_prompt(q, kv_sel, kv_win, kc, gates):
    T = SEQ
    tq = Q_TILE
    assert T % tq == 0 and WINDOW % tq == 0 and WINDOW + tq <= T and T // SEL_LEN <= LANES
    assert T // CMP_STRIDE == LANES and T % SEL_KEY_STEP == 0 and SEL_KEY_STEP % tq == 0
    nq = T // tq
    gw = GROUP * HEAD_DIM
    blk = 2 * 4 * _nbytes((T, HEAD_DIM), CD) + 8 * _nbytes((GROUP * tq, T), F32) + 4 * _nbytes((tq, gw), F32)
    k_spec = pl.BlockSpec((T, HEAD_DIM), lambda b, g, i: (b, g))
    v_spec = pl.BlockSpec((T, HEAD_DIM), lambda b, g, i: (b, KV_HEADS + g))
    return pl.pallas_call(
        functools.partial(_attn_prompt_kernel, tq=tq, T=T),
        grid=(BATCH, KV_HEADS, nq),
        in_specs=[pl.BlockSpec((tq, gw), lambda b, g, i: (b * nq + i, g)),
                  k_spec, v_spec, k_spec, v_spec,
                  pl.BlockSpec((1, 1, T // CMP_STRIDE, HEAD_DIM), lambda b, g, i: (b, g, 0, 0)),
                  pl.BlockSpec((1, 1, T // CMP_STRIDE, HEAD_DIM), lambda b, g, i: (b, KV_HEADS + g, 0, 0)),
                  pl.BlockSpec((tq, LANES), lambda b, g, i: (b * nq + i, g))],
        out_specs=pl.BlockSpec((tq, gw), lambda b, g, i: (b * nq + i, g)),
        out_shape=jax.ShapeDtypeStruct((q.shape[0], ATTN_W), CD),
        compiler_params=_params(("parallel", "parallel", "parallel"), blk),
        name="nsa_prompt_attention",
    )(q, kv_sel, kv_sel, kv_win, kv_win, kc, kc, gates)


def _sample_cmp_kernel(q_ref, kc_ref, vc_ref, oc_ref, sel_ref, *, n_cmp_valid):
    T = DEC_SEQ
    n_cmp = kc_ref.shape[2]
    qs = _stack_heads(q_ref[...], 0).astype(CD)
    qpos1 = PAST_LEN + lax.broadcasted_iota(jnp.int32, (T, 1), 0)
    qpos = jnp.concatenate([qpos1] * GROUP, axis=0)
    s = _dot_nt(qs, kc_ref[0, 0]) * SCALE
    n = lax.broadcasted_iota(jnp.int32, (1, n_cmp), 1)
    p = _masked_softmax(s, (n * CMP_STRIDE + (CMP_LEN - 1) <= qpos) & (n < n_cmp_valid))
    oc_ref[0, 0] = _dot(p.astype(CD), vc_ref[0, 0])
    psum = p[0:T]
    for r in range(1, GROUP):
        psum = psum + p[r * T:(r + 1) * T]
    imp = _group_sum_lanes(psum, LANES)

    j = lax.broadcasted_iota(jnp.int32, (1, LANES), 1)
    cur = qpos1 // SEL_LEN
    score = jnp.where((j == 0) | (j == cur - 1), FORCE, imp)
    rank = jnp.zeros((T, LANES), jnp.int32)
    for sh in range(1, LANES):
        other = pltpu.roll(score, sh, 1)
        oj = jnp.where(j >= sh, j - sh, j - sh + LANES)
        rank = rank + ((other > score) | ((other == score) & (oj < j))).astype(jnp.int32)
    sel_ref[0, 0] = (rank < SEL_TOP - 1).astype(F32)


def _sample_cmp(q_s, kc_s, n_cmp_valid):
    nb = DEC_BATCH
    gw = GROUP * HEAD_DIM
    n_cmp = kc_s.shape[2]
    blk = 16 * _nbytes((GROUP * DEC_SEQ, n_cmp), F32) + 4 * _nbytes((n_cmp, HEAD_DIM), CD)
    return pl.pallas_call(
        functools.partial(_sample_cmp_kernel, n_cmp_valid=n_cmp_valid),
        grid=(nb, KV_HEADS),
        in_specs=[pl.BlockSpec((DEC_SEQ, gw), lambda b, g: (b, g)),
                  pl.BlockSpec((1, 1, n_cmp, HEAD_DIM), lambda b, g: (b, g, 0, 0)),
                  pl.BlockSpec((1, 1, n_cmp, HEAD_DIM), lambda b, g: (b, KV_HEADS + g, 0, 0))],
        out_specs=[pl.BlockSpec((1, 1, GROUP * DEC_SEQ, HEAD_DIM), lambda b, g: (b, g, 0, 0)),
                   pl.BlockSpec((1, 1, DEC_SEQ, LANES), lambda b, g: (b, g, 0, 0))],
        out_shape=[jax.ShapeDtypeStruct((nb, KV_HEADS, GROUP * DEC_SEQ, HEAD_DIM), F32),
                   jax.ShapeDtypeStruct((nb, KV_HEADS, DEC_SEQ, LANES), F32)],
        compiler_params=_params(("parallel", "parallel"), blk),
        name="nsa_sample_cmp_attention",
    )(q_s, kc_s, kc_s)


def _put_rows_kernel(src_ref, dst_any, o_ref):
    o_ref[...] = src_ref[...].astype(o_ref.dtype)


def _put_rows(dst, src, row_off):
    n, w = src.shape
    assert row_off % n == 0 and n % 16 == 0
    return pl.pallas_call(
        _put_rows_kernel,
        grid=(1,),
        in_specs=[pl.BlockSpec((n, w), lambda i: (0, 0)), pl.BlockSpec(memory_space=pl.ANY)],
        out_specs=pl.BlockSpec((n, w), lambda i: (row_off // n, 0)),
        out_shape=jax.ShapeDtypeStruct(dst.shape, dst.dtype),
        input_output_aliases={1: 0},
        compiler_params=_params(("arbitrary",), 4 * _nbytes((n, w), F32)),
        name="put_sample_rows",
    )(src, dst)


def _pad_rows(x, rows):
    return jnp.concatenate([x, jnp.zeros((rows - x.shape[0], x.shape[1]), x.dtype)], axis=0)


def _head_rows(ref, slot, n):
    return ref[pl.ds(slot, n, stride=KVC), :]


def _sample_sel_kernel(pt_ref, q_ref, sel_ref, *refs):
    page_refs = refs[:SEL_PAGES]
    new_ref, o_ref, m_sc, l_sc, acc_sc = refs[SEL_PAGES:]
    T = DEC_SEQ
    R = GROUP * T
    step = pl.program_id(1)
    nk = SEL_PAGES * PAGE_SIZE

    @pl.when(step == 0)
    def _():
        m_sc[...] = jnp.full(m_sc.shape, NEG_INF, F32)
        l_sc[...] = jnp.zeros(l_sc.shape, F32)
        acc_sc[...] = jnp.zeros(acc_sc.shape, F32)

    def update(g, s, mask, v):
        rows = slice(g * R, (g + 1) * R)
        m_prev = m_sc[rows]
        m_new = jnp.maximum(m_prev, jnp.max(jnp.where(mask, s, NEG_INF), axis=-1, keepdims=True))
        alpha = jnp.exp(m_prev - m_new)
        e = jnp.where(mask, jnp.exp(s - m_new), 0.0)
        l_sc[rows] = alpha * l_sc[rows] + jnp.sum(e, axis=-1, keepdims=True)
        acc_sc[rows] = alpha * acc_sc[rows] + _dot(e.astype(CD), v)
        m_sc[rows] = m_new

    q = q_ref[...]
    blk_of_key = step * (nk // SEL_LEN) + lax.broadcasted_iota(jnp.int32, (LANES, nk), 1) // SEL_LEN
    expand = (blk_of_key == lax.broadcasted_iota(jnp.int32, (LANES, nk), 0)).astype(CD)
    key_sel = _dot(sel_ref[...].reshape(KV_HEADS * T, LANES).astype(CD), expand)
    for g in range(KV_HEADS):
        qg = _stack_heads(q, g * GROUP * HEAD_DIM).astype(CD)
        k = jnp.concatenate([_head_rows(pr, g, PAGE_SIZE) for pr in page_refs], axis=0).astype(CD)
        v = jnp.concatenate([_head_rows(pr, KV_HEADS + g, PAGE_SIZE) for pr in page_refs], axis=0).astype(CD)
        mask = jnp.concatenate([key_sel[g * T:(g + 1) * T]] * GROUP, axis=0) > 0.5
        update(g, _dot_nt(qg, k) * SCALE, mask, v)

    @pl.when(step == pl.num_programs(1) - 1)
    def _():
        t = jnp.concatenate([lax.broadcasted_iota(jnp.int32, (T, 1), 0)] * GROUP, axis=0)
        i = lax.broadcasted_iota(jnp.int32, (1, LANES), 1)
        mask = (i < T) & (i <= t)
        for g in range(KV_HEADS):
            qg = _stack_heads(q, g * GROUP * HEAD_DIM).astype(CD)
            k = _pad_rows(_head_rows(new_ref, g, T), LANES).astype(CD)
            v = _pad_rows(_head_rows(new_ref, KV_HEADS + g, T), LANES).astype(CD)
            update(g, _dot_nt(qg, k) * SCALE, mask, v)
        o_ref[...] = (acc_sc[...] * (1.0 / l_sc[...])).reshape(o_ref.shape)


def _sample_sel(q_s, sel, pool, layer, page_table, new_rows):
    nb, n_pages = page_table.shape
    assert n_pages % SEL_PAGES == 0 and (SEL_PAGES * PAGE_SIZE) % SEL_LEN == 0
    R = GROUP * DEC_SEQ
    page_rows = PAGE_SIZE * KVC
    blk = 2 * SEL_PAGES * _nbytes((page_rows, HEAD_DIM), F32) + 16 * _nbytes((R, SEL_PAGES * PAGE_SIZE), F32) \
        + 8 * _nbytes((SEL_PAGES * PAGE_SIZE, HEAD_DIM), F32)

    def page_spec(k):
        return pl.BlockSpec((None, None, page_rows, HEAD_DIM),
                            lambda b, s, pt: (layer, pt[b, s * SEL_PAGES + k], 0, 0))

    return pl.pallas_call(
        _sample_sel_kernel,
        grid_spec=pltpu.PrefetchScalarGridSpec(
            num_scalar_prefetch=1,
            grid=(nb, n_pages // SEL_PAGES),
            in_specs=[pl.BlockSpec((DEC_SEQ, ATTN_W), lambda b, s, pt: (b, 0)),
                      pl.BlockSpec((None, KV_HEADS, DEC_SEQ, LANES), lambda b, s, pt: (b, 0, 0, 0))]
            + [page_spec(k) for k in range(SEL_PAGES)]
            + [pl.BlockSpec((DEC_SEQ * KVC, HEAD_DIM), lambda b, s, pt: (b, 0))],
            out_specs=pl.BlockSpec((None, KV_HEADS, R, HEAD_DIM), lambda b, s, pt: (b, 0, 0, 0)),
            scratch_shapes=[pltpu.VMEM((KV_HEADS * R, 1), F32), pltpu.VMEM((KV_HEADS * R, 1), F32),
                            pltpu.VMEM((KV_HEADS * R, HEAD_DIM), F32)]),
        out_shape=jax.ShapeDtypeStruct((nb, KV_HEADS, R, HEAD_DIM), F32),
        compiler_params=_params(("parallel", "arbitrary"), blk),
        name="nsa_sample_sel_attention",
    )(page_table, q_s, sel, *([pool] * SEL_PAGES), new_rows)


def _sample_win_kernel(q_ref, buf_ref, new_ref, oc_ref, os_ref, g_ref, o_ref, *, wb):
    T = DEC_SEQ
    q = q_ref[...]
    gates = g_ref[...]
    t = jnp.concatenate([lax.broadcasted_iota(jnp.int32, (T, 1), 0)] * GROUP, axis=0)
    qpos = PAST_LEN + t
    i = lax.broadcasted_iota(jnp.int32, (1, wb + LANES), 1)
    kpos = PAST_LEN - wb + i
    mask = (i < wb + T) & (kpos <= qpos) & (kpos > qpos - WINDOW)
    for g in range(KV_HEADS):
        qg = _stack_heads(q, g * GROUP * HEAD_DIM).astype(CD)
        k = jnp.concatenate([_head_rows(buf_ref, g, wb), _pad_rows(_head_rows(new_ref, g, T), LANES)],
                            axis=0).astype(CD)
        v = jnp.concatenate([_head_rows(buf_ref, KV_HEADS + g, wb),
                             _pad_rows(_head_rows(new_ref, KV_HEADS + g, T), LANES)], axis=0).astype(CD)
        p = _masked_softmax(_dot_nt(qg, k) * SCALE, mask)
        o_w = _dot(p.astype(CD), v)
        o_c = oc_ref[g]
        o_s = os_ref[g]
        for r in range(GROUP):
            o = _gate_sum(gates, g * LANES, o_c, o_s, o_w, r, slice(r * T, (r + 1) * T))
            col = (g * GROUP + r) * HEAD_DIM
            o_ref[:, col:col + HEAD_DIM] = o


def _sample_win(q_s, win_buf, layer, new_rows, o_c, o_s, gates_s):
    nb = DEC_BATCH
    wb = win_buf.shape[2] // KVC
    R = GROUP * DEC_SEQ
    assert wb % LANES == 0
    blk = 2 * _nbytes((wb * KVC, HEAD_DIM), F32) + 8 * _nbytes((R, wb + LANES), F32) \
        + 8 * _nbytes((wb + LANES, HEAD_DIM), F32)
    return pl.pallas_call(
        functools.partial(_sample_win_kernel, wb=wb),
        grid=(nb,),
        in_specs=[pl.BlockSpec((DEC_SEQ, ATTN_W), lambda b: (b, 0)),
                  pl.BlockSpec((None, None, wb * KVC, HEAD_DIM), lambda b: (layer, b, 0, 0)),
                  pl.BlockSpec((DEC_SEQ * KVC, HEAD_DIM), lambda b: (b, 0)),
                  pl.BlockSpec((None, KV_HEADS, R, HEAD_DIM), lambda b: (b, 0, 0, 0)),
                  pl.BlockSpec((None, KV_HEADS, R, HEAD_DIM), lambda b: (b, 0, 0, 0)),
                  pl.BlockSpec((DEC_SEQ, KV_HEADS * LANES), lambda b: (b, 0))],
        out_specs=pl.BlockSpec((DEC_SEQ, ATTN_W), lambda b: (b, 0)),
        out_shape=jax.ShapeDtypeStruct((nb * DEC_SEQ, ATTN_W), F32),
        compiler_params=_params(("parallel",), blk),
        name="nsa_sample_win_attention",
    )(q_s, win_buf, new_rows, o_c, o_s, gates_s)


def _half_ffn(h, hn, w_gate, w_up, w_down, layer, g_next):
    act = _swiglu(hn, w_gate, w_up, layer)
    return _resid_proj(act, w_down, layer, h, 0.5, g_next, ROW_TILE // 2, COL_TILE // 2, "ffn_down")


def _prep_w_in(w_in):
    o1 = ATTN_W
    o4 = o1 + 6 * KV_W
    o5 = o4 + 3 * N_HEADS
    o6 = o5 + GM_WIDTH
    o7 = o6 + GM_WIDTH
    D = w_in.shape[1]
    wg = w_in[:, :, o4:o5].reshape(DEPTH, D, 3, KV_HEADS, GROUP).transpose(0, 1, 3, 2, 4)
    wg = jnp.pad(wg.reshape(DEPTH, D, KV_HEADS, 3 * GROUP), ((0, 0), (0, 0), (0, 0), (0, LANES - 3 * GROUP)))
    kv = [w_in[:, :, o1 + s * 2 * KV_W:o1 + (s + 1) * 2 * KV_W].astype(CD) for s in range(3)]
    return dict(q=w_in[:, :, :o1].astype(CD), kv=kv, gate=wg.reshape(DEPTH, D, KV_HEADS * LANES).astype(CD),
                u=w_in[:, :, o5:o6].astype(CD), v=w_in[:, :, o6:o7].astype(CD), mg=w_in[:, :, o7:].astype(CD))


def _layer(h, hn, layer, W, small, caches, page_table, kv_bufs, g_after):
    (nm, pe, w1, w2, ws, bs, n2) = small
    cache_cmp, cache_sel, cache_win = caches
    MP = BATCH * SEQ
    MS = DEC_BATCH * DEC_SEQ
    n_pages = page_table.shape[1]
    assert n_pages * PAGE_SIZE == PAST_LEN and PAST_LEN // SEL_LEN == LANES and DEC_SEQ <= SEL_LEN
    assert DEC_SEQ == SUBLANES and MP % MS == 0

    h, n = _half_ffn(h, hn, W["g1"], W["u1"], W["d1"], layer, nm)

    q = _proj(n, W["q"], layer, None, CD, "proj_q")
    gates = _proj(n, W["gate"], layer, "sigmoid", F32, "proj_branch_gates")
    cmp_buf, kv_c = _kv_proj_prompt(n, W["kv"][0], layer, kv_bufs[0], F32, False, "proj_kv_cmp")
    sel_buf, kv_sel = _kv_proj_prompt(n, W["kv"][1], layer, kv_bufs[1], CD, False, "proj_kv_sel")
    win_buf, kv_win = _kv_proj_prompt(n, W["kv"][2], layer, kv_bufs[2], CD, True, "proj_kv_win")
    new_s = [_kv_proj_sample(n, W["kv"][s], layer, "proj_kv_sample") for s in range(3)]

    w1c = w1.astype(CD).reshape(2, CMP_LEN * HEAD_DIM, HEAD_DIM)
    w2c = w2.astype(CD)

    kc_p = _compress_prompt(kv_c, pe, w1c, w2c)
    o_p = _attn_prompt(q, kv_sel, kv_win, kc_p, gates)
    (gm,) = _gmlp(n, W["u"], W["v"], layer, ws, bs[:, :, None], MP, 0, 1024, GM_CHUNK, GM_CHUNK, None, False)

    q_s = q[MP:].astype(F32)
    gates_s = gates[MP:]
    kc_s = _compress_sample(cache_cmp, layer, page_table, pe, w1c, w2c)
    n_cmp_valid = (PAST_LEN + DEC_SEQ - CMP_LEN) // CMP_STRIDE + 1
    oc_s, sel_s = _sample_cmp(q_s, kc_s, n_cmp_valid)
    os_s = _sample_sel(q_s, sel_s, cache_sel, layer, page_table, new_s[1])
    o_s = _sample_win(q_s, cache_win, layer, new_s[2], oc_s, os_s, gates_s)
    cl = min(DEC_SEQ, GM_CHUNK)
    ws_s = jnp.tile(ws[:, :cl, :cl], (1, MS // cl, MS // cl))
    bs_s = jnp.tile(bs[:, :cl], (1, MS // cl))[:, :, None]
    gm, v_s = _gmlp(n, W["u"], W["v"], layer, ws_s, bs_s, MS, MP, MS, MS, cl, gm, True)

    o = _put_rows(o_p, o_s, MP)
    merged = _merge(n, o, gm, W["mg"], W["wua"], W["wug"], layer)
    h, n = _resid_proj(merged, W["wo"], layer, h, 1.0, n2, ROW_TILE, COL_TILE, "mixer_out")
    h, hn = _half_ffn(h, n, W["g2"], W["u2"], W["d2"], layer, g_after)
    return h, hn, (cmp_buf, sel_buf, win_buf), (new_s, v_s)


def kernel(x_prompt, x_sample, cache_cmp_kv, cache_sel_kv, cache_win_kv, page_table, norm_ffn1, ffn1_gate, ffn1_up, ffn1_down, norm_mix, w_in, cmp_pe, cmp_w1, cmp_w2, gm_ws, gm_bs, w_up_attn, w_up_gm, w_out, norm_ffn2, ffn2_gate, ffn2_up, ffn2_down, norm_final):
    MP = BATCH * SEQ
    MS = DEC_BATCH * DEC_SEQ
    kv_row = (2, KV_HEADS, HEAD_DIM)
    h = jnp.concatenate([x_prompt.reshape(MP, D_MODEL), x_sample.reshape(MS, D_MODEL)], axis=0)
    n_phys = cache_cmp_kv.shape[1]
    wb = cache_win_kv.shape[2]
    caches = (cache_cmp_kv.reshape(DEPTH, n_phys, PAGE_SIZE * KVC, HEAD_DIM),
              cache_sel_kv.reshape(DEPTH, n_phys, PAGE_SIZE * KVC, HEAD_DIM),
              cache_win_kv.reshape(DEPTH, DEC_BATCH, wb * KVC, HEAD_DIM))
    W = _prep_w_in(w_in)
    W.update(g1=ffn1_gate, u1=ffn1_up, d1=ffn1_down.astype(CD),
             g2=ffn2_gate, u2=ffn2_up, d2=ffn2_down.astype(CD),
             wua=w_up_attn.astype(CD), wug=w_up_gm.astype(CD), wo=w_out.astype(CD))
    kv_bufs = (None, None, None)
    sample_outs = []
    hn = _norm_prep(h, norm_ffn1[0])
    for l in range(DEPTH):
        small = (norm_mix[l], cmp_pe[l], cmp_w1[l], cmp_w2[l], gm_ws[l], gm_bs[l], norm_ffn2[l])
        g_after = norm_ffn1[l + 1] if l + 1 < DEPTH else None
        h, hn, kv_bufs, s_out = _layer(h, hn, l, W, small, caches, page_table, kv_bufs, g_after)
        sample_outs.append(s_out)
    y_prompt = _rmsnorm(h, norm_final, F32, rows=MP, row_off=0).reshape(BATCH, SEQ, D_MODEL)
    y_sample = _rmsnorm(h, norm_final, F32, rows=MS, row_off=MP, tile=MS).reshape(DEC_BATCH, DEC_SEQ, D_MODEL)

    cmp_buf, sel_buf, win_buf = kv_bufs
    wk = min(WINDOW, SEQ)
    new = [jnp.stack([sample_outs[l][0][s] for l in range(DEPTH)]).reshape((DEPTH, DEC_BATCH, DEC_SEQ) + kv_row)
           for s in range(3)]
    win_all = jnp.concatenate([cache_win_kv, new[2]], axis=2)
    win_s = win_all[:, :, -min(WINDOW, win_all.shape[2]):]
    gm_v = jnp.stack([sample_outs[l][1] for l in range(DEPTH)]).reshape(DEPTH, DEC_BATCH, DEC_SEQ, GM_WIDTH)
    return (y_prompt, y_sample,
            cmp_buf.reshape((DEPTH, BATCH, SEQ) + kv_row), new[0],
            sel_buf.reshape((DEPTH, BATCH, SEQ) + kv_row), new[1],
            win_buf.reshape((DEPTH, BATCH, wk) + kv_row), win_s, gm_v)
```

```python
import functools

import jax
import jax.numpy as jnp
from jax import lax
from jax.experimental import pallas as pl
from jax.experimental.pallas import tpu as pltpu

D_MODEL = 4096
BATCH = 4
SEQ = 2048
DEPTH = 4
DEC_BATCH = 8
DEC_SEQ = 8
PAST_LEN = 8192
PAGE_SIZE = 128

N_HEADS = 16
KV_HEADS = 4
HEAD_DIM = 128
GROUP = N_HEADS // KV_HEADS
ATTN_W = N_HEADS * HEAD_DIM
KV_W = KV_HEADS * HEAD_DIM
CMP_LEN = 32
CMP_STRIDE = 16
SEL_LEN = 64
SEL_TOP = 16
WINDOW = 512
GM_WIDTH = D_MODEL // 2
GM_GROUPS = 4
GM_CHUNK = 128
D_FF = 11008
EPS = 1e-6
NEG_INF = -1e30
FORCE = 1e9
SCALE = HEAD_DIM ** -0.5
LOG2E = 1.4426950408889634

LANES = 128
SUBLANES = 8
VMEM_BUDGET = 56 * 1024 * 1024
CD = jnp.bfloat16
F32 = jnp.float32
PER_SEL = SEL_LEN // CMP_STRIDE
KVC = 2 * KV_HEADS

ROW_TILE = 1376
COL_TILE = 512
Q_TILE = 256
SEL_KEY_STEP = 256
KV_ROW_TILE = 512
SEL_PAGES = 8
CMP_PAGES = 8


def _pick_tile(n, target, mult):
    best = n
    for t in range(mult, min(n, target) + 1, mult):
        if n % t == 0:
            best = t
    return best


def _params(sem, block_bytes):
    limit = min(VMEM_BUDGET, max(16 * 1024 * 1024, int(block_bytes)))
    return pltpu.CompilerParams(dimension_semantics=sem, vmem_limit_bytes=limit)


def _nbytes(shape, dtype):
    n = 1
    for s in shape:
        n *= s
    return n * jnp.dtype(dtype).itemsize


def _dot(a, b):
    return jnp.dot(a, b, preferred_element_type=F32)


def _dot_nt(a, b):
    return lax.dot_general(a, b, (((1,), (1,)), ((), ())), preferred_element_type=F32)


def _rms_kernel(x_ref, g_ref, o_ref):
    x = x_ref[...]
    y = x * lax.rsqrt(jnp.mean(x * x, axis=-1, keepdims=True) + EPS)
    o_ref[...] = (y * g_ref[...]).astype(o_ref.dtype)


def _rmsnorm(x, g, out_dtype, rows=None, row_off=0, tile=None):
    M, D = x.shape
    rows = M if rows is None else rows
    tm = tile or _pick_tile(rows, 344, 16)
    assert rows % tm == 0 and row_off % tm == 0
    off = row_off // tm
    blk = 2 * _nbytes((tm, D), F32) * 2 + 2 * _nbytes((tm, D), out_dtype)
    return pl.pallas_call(
        _rms_kernel,
        grid=(rows // tm,),
        in_specs=[pl.BlockSpec((tm, D), lambda i: (i + off, 0)),
                  pl.BlockSpec((1, D), lambda i: (0, 0))],
        out_specs=pl.BlockSpec((tm, D), lambda i: (i, 0)),
        out_shape=jax.ShapeDtypeStruct((rows, D), out_dtype),
        compiler_params=_params(("parallel",), blk),
        name="rmsnorm",
    )(x, g.reshape(1, D))


def _inv_rms(ss_ref):
    return lax.rsqrt(ss_ref[...] * (1.0 / D_MODEL) + EPS)


def _ss_spec(ss, tm, off=0):
    return pl.BlockSpec((tm, 1), lambda i, *_: (i + off, 0))


def _norm_prep_kernel(x_ref, g_ref, hb_ref, ss_ref):
    x = x_ref[...]
    hb_ref[...] = (x * g_ref[...]).astype(hb_ref.dtype)
    ss_ref[...] = jnp.sum(x * x, axis=-1, keepdims=True)


def _norm_prep(x, g):
    M, D = x.shape
    tm = _pick_tile(M, 344, 16)
    blk = 4 * _nbytes((tm, D), F32) + 2 * _nbytes((tm, D), CD)
    return pl.pallas_call(
        _norm_prep_kernel,
        grid=(M // tm,),
        in_specs=[pl.BlockSpec((tm, D), lambda i: (i, 0)),
                  pl.BlockSpec((1, D), lambda i: (0, 0))],
        out_specs=[pl.BlockSpec((tm, D), lambda i: (i, 0)), pl.BlockSpec((tm, 1), lambda i: (i, 0))],
        out_shape=[jax.ShapeDtypeStruct((M, D), CD), jax.ShapeDtypeStruct((M, 1), F32)],
        compiler_params=_params(("parallel",), blk),
        name="norm_prep",
    )(x, g.reshape(1, D))


def _proj_kernel(x_ref, ss_ref, w_ref, o_ref, *, act):
    z = _inv_rms(ss_ref) * _dot_nt(x_ref[...], w_ref[...])
    if act == "sigmoid":
        z = jax.nn.sigmoid(z)
    o_ref[...] = z.astype(o_ref.dtype)


def _proj(xn, wt, row0, N, layer, act, out_dtype, name):
    x, ss = xn
    M, K = x.shape
    tm = _pick_tile(M, ROW_TILE, 16)
    tn = _pick_tile(N, COL_TILE, LANES)
    assert row0 % tn == 0
    blk = 2 * (_nbytes((tm, K), x.dtype) + _nbytes((K, tn), wt.dtype) + _nbytes((tm, tn), out_dtype)) \
        + 4 * _nbytes((tm, tn), F32)
    return pl.pallas_call(
        functools.partial(_proj_kernel, act=act),
        grid=(M // tm, N // tn),
        in_specs=[pl.BlockSpec((tm, K), lambda i, j: (i, 0)), _ss_spec(ss, tm),
                  pl.BlockSpec((None, tn, K), lambda i, j: (layer, row0 // tn + j, 0))],
        out_specs=pl.BlockSpec((tm, tn), lambda i, j: (i, j)),
        out_shape=jax.ShapeDtypeStruct((M, N), out_dtype),
        compiler_params=_params(("parallel", "parallel"), blk),
        name=name,
    )(x, ss, wt)


def _swiglu_kernel(x_ref, ss_ref, wg_ref, wu_ref, o_ref):
    x = x_ref[...]
    inv = _inv_rms(ss_ref)
    a = inv * _dot(x, wg_ref[...].astype(CD))
    b = inv * _dot(x, wu_ref[...].astype(CD))
    o_ref[...] = (jax.nn.silu(a) * b).astype(o_ref.dtype)


def _swiglu(xn, wg, wu, layer):
    x, ss = xn
    M, K = x.shape
    N = wg.shape[2]
    tm = _pick_tile(M, ROW_TILE, 16)
    tn = _pick_tile(N, COL_TILE // 2, LANES)
    blk = 2 * (_nbytes((tm, K), x.dtype) + 2 * _nbytes((K, tn), wg.dtype) + _nbytes((tm, tn), CD)) \
        + 2 * _nbytes((K, tn), CD) + 6 * _nbytes((tm, tn), F32)
    w_spec = pl.BlockSpec((None, K, tn), lambda i, j: (layer, 0, j))
    return pl.pallas_call(
        _swiglu_kernel,
        grid=(M // tm, N // tn),
        in_specs=[pl.BlockSpec((tm, K), lambda i, j: (i, 0)), _ss_spec(ss, tm), w_spec, w_spec],
        out_specs=pl.BlockSpec((tm, tn), lambda i, j: (i, j)),
        out_shape=jax.ShapeDtypeStruct((M, N), CD),
        compiler_params=_params(("parallel", "parallel"), blk),
        name="ffn_swiglu",
    )(x, ss, wg, wu)


def _resid_kernel(x_ref, w_ref, h_ref, *refs, scale, emit_norm):
    hn = h_ref[...] + scale * _dot(x_ref[...], w_ref[...].astype(CD))
    if emit_norm:
        g_ref, o_ref, hb_ref, ss_ref = refs
        hb_ref[...] = (hn * g_ref[...]).astype(hb_ref.dtype)
        part = jnp.sum(hn * hn, axis=-1, keepdims=True)
        j = pl.program_id(1)

        @pl.when(j == 0)
        def _():
            ss_ref[...] = part

        @pl.when(j > 0)
        def _():
            ss_ref[...] += part
    else:
        (o_ref,) = refs
    o_ref[...] = hn


def _resid_proj(x, w, layer, h, scale, g_next, tm_target, tn_target, name):
    M, K = x.shape
    N = w.shape[2]
    tm = _pick_tile(M, tm_target, 16)
    tn = _pick_tile(N, tn_target, LANES)
    nj = N // tn
    emit = g_next is not None
    blk = 2 * (_nbytes((tm, K), x.dtype) + _nbytes((K, tn), w.dtype) + 3 * _nbytes((tm, tn), F32)) \
        + (w.dtype != CD) * _nbytes((K, tn), CD) + 4 * _nbytes((tm, tn), F32)
    tile = pl.BlockSpec((tm, tn), lambda i, j: (i, j))
    in_specs = [pl.BlockSpec((tm, K), lambda i, j: (i, 0)),
                pl.BlockSpec((None, K, tn), lambda i, j: (layer, 0, j)), tile]
    args = [x, w, h]
    out_specs = [tile]
    out_shape = [jax.ShapeDtypeStruct((M, N), F32)]
    if emit:
        in_specs.append(pl.BlockSpec((1, tn), lambda i, j: (0, j)))
        args.append(g_next.reshape(1, N))
        out_specs += [tile, pl.BlockSpec((tm, 1), lambda i, j: (i, 0))]
        out_shape += [jax.ShapeDtypeStruct((M, N), CD), jax.ShapeDtypeStruct((M, 1), F32)]
    outs = pl.pallas_call(
        functools.partial(_resid_kernel, scale=scale, emit_norm=emit),
        grid=(M // tm, nj),
        in_specs=in_specs,
        out_specs=out_specs,
        out_shape=out_shape,
        compiler_params=_params(("parallel", "arbitrary"), blk),
        name=name,
    )(*args)
    return (outs[0], (outs[1], outs[2])) if emit else (outs[0], None)


def _merge_kernel(n_ref, ss_ref, o_ref, gm_ref, wma_ref, wmg_ref, wa_ref, wg_ref, out_ref):
    n = n_ref[...]
    inv = _inv_rms(ss_ref)
    ga = jax.nn.sigmoid(inv * _dot_nt(n, wma_ref[...]))
    gg = jax.nn.sigmoid(inv * _dot_nt(n, wmg_ref[...]))
    a = _dot(o_ref[...], wa_ref[...].astype(CD))
    g = _dot(gm_ref[...], wg_ref[...].astype(CD))
    out_ref[...] = (ga * a + gg * g).astype(out_ref.dtype)


def _merge(xn, o, gm, w_mg, mg_row0, wa, wg, layer):
    n, ss = xn
    M, K = n.shape
    Ka = o.shape[1]
    Kg = gm.shape[1]
    N = wa.shape[2]
    tm = _pick_tile(M, ROW_TILE // 2, 16)
    tn = _pick_tile(N, COL_TILE // 2, LANES)
    nj = N // tn
    blk = 2 * (_nbytes((tm, K + Ka + Kg), CD) + _nbytes((2 * K, tn), CD) + _nbytes((Ka + Kg, tn), wa.dtype)
               + _nbytes((tm, tn), CD)) + _nbytes((Ka + Kg, tn), CD) + 8 * _nbytes((tm, tn), F32)
    return pl.pallas_call(
        _merge_kernel,
        grid=(M // tm, nj),
        in_specs=[pl.BlockSpec((tm, K), lambda i, j: (i, 0)), _ss_spec(ss, tm),
                  pl.BlockSpec((tm, Ka), lambda i, j: (i, 0)),
                  pl.BlockSpec((tm, Kg), lambda i, j: (i, 0)),
                  pl.BlockSpec((None, tn, K), lambda i, j: (layer, mg_row0 // tn + j, 0)),
                  pl.BlockSpec((None, tn, K), lambda i, j: (layer, mg_row0 // tn + j + nj, 0)),
                  pl.BlockSpec((None, Ka, tn), lambda i, j: (layer, 0, j)),
                  pl.BlockSpec((None, Kg, tn), lambda i, j: (layer, 0, j))],
        out_specs=pl.BlockSpec((tm, tn), lambda i, j: (i, j)),
        out_shape=jax.ShapeDtypeStruct((M, N), CD),
        compiler_params=_params(("parallel", "parallel"), blk),
        name="mixer_merge",
    )(n, ss, o, gm, w_mg, w_mg, wa, wg)


def _kv_kernel(*refs, has_prev, has_std, last_only):
    x_ref, ss_ref, w_ref = refs[0], refs[1], refs[2]
    il_ref = refs[3 + has_prev]
    tm = x_ref.shape[0]
    z = _inv_rms(ss_ref) * _dot_nt(x_ref[...], w_ref[...])
    if has_std:
        std_ref = refs[4 + has_prev]
        std_ref[...] = z.astype(std_ref.dtype)

    def write():
        for c in range(KVC):
            il_ref[pl.ds(c, tm, stride=KVC), :] = z[:, c * HEAD_DIM:(c + 1) * HEAD_DIM]

    if last_only:
        tiles_per_batch = SEQ // tm
        pl.when(pl.program_id(0) % tiles_per_batch == tiles_per_batch - 1)(write)
    else:
        write()


def _kv_proj_prompt(xn, w, row0, layer, prev, std_dtype, last_only, name):
    x, ss = xn
    K = x.shape[1]
    W = 2 * KV_W
    MP = BATCH * SEQ
    tm = KV_ROW_TILE
    assert SEQ % tm == 0 and (not last_only or tm == min(WINDOW, SEQ))
    if last_only:
        il_shape = (DEPTH, BATCH, tm * KVC, HEAD_DIM)
        il_spec = pl.BlockSpec((None, None, tm * KVC, HEAD_DIM), lambda i: (layer, i // (SEQ // tm), 0, 0))
    else:
        il_shape = (DEPTH, MP * KVC, HEAD_DIM)
        il_spec = pl.BlockSpec((None, tm * KVC, HEAD_DIM), lambda i: (layer, i, 0))
    in_specs = [pl.BlockSpec((tm, K), lambda i: (i, 0)), _ss_spec(ss, tm),
                pl.BlockSpec((None, W, K), lambda i: (layer, row0 // W, 0))]
    args = [x, ss, w]
    aliases = {}
    if prev is not None:
        in_specs.append(pl.BlockSpec(memory_space=pl.ANY))
        args.append(prev)
        aliases = {3: 0}
    blk = 2 * (_nbytes((tm, K), CD) + _nbytes((K, W), CD) + 2 * _nbytes((tm, W), F32)) + 3 * _nbytes((tm, W), F32)
    return pl.pallas_call(
        functools.partial(_kv_kernel, has_prev=prev is not None, has_std=True, last_only=last_only),
        grid=(MP // tm,),
        in_specs=in_specs,
        out_specs=[il_spec, pl.BlockSpec((tm, W), lambda i: (i, 0))],
        out_shape=[jax.ShapeDtypeStruct(il_shape, F32), jax.ShapeDtypeStruct((MP, W), std_dtype)],
        input_output_aliases=aliases,
        compiler_params=_params(("arbitrary",), blk),
        name=name,
    )(*args)


def _kv_proj_sample(xn, w, row0, layer, name):
    x, ss = xn
    K = x.shape[1]
    W = 2 * KV_W
    MP = BATCH * SEQ
    MS = DEC_BATCH * DEC_SEQ
    assert MP % MS == 0
    blk = 2 * (_nbytes((MS, K), CD) + _nbytes((K, W), CD) + _nbytes((MS, W), F32)) + 3 * _nbytes((MS, W), F32)
    return pl.pallas_call(
        functools.partial(_kv_kernel, has_prev=False, has_std=False, last_only=False),
        grid=(1,),
        in_specs=[pl.BlockSpec((MS, K), lambda i: (MP // MS, 0)), _ss_spec(ss, MS, MP // MS),
                  pl.BlockSpec((None, W, K), lambda i: (layer, row0 // W, 0))],
        out_specs=pl.BlockSpec((MS * KVC, HEAD_DIM), lambda i: (0, 0)),
        out_shape=jax.ShapeDtypeStruct((MS * KVC, HEAD_DIM), F32),
        compiler_params=_params(("arbitrary",), blk),
        name=name,
    )(x, ss, w)


def _gmlp_kernel(x_ref, ss_ref, wu_ref, wv_ref, ws_ref, b_ref, *refs, chunk, has_prev, want_v):
    gm_ref = refs[has_prev]
    R = ws_ref.shape[1]
    tm = x_ref.shape[0]
    x = x_ref[...]
    inv = _inv_rms(ss_ref)
    u = jax.nn.gelu(inv * _dot_nt(x, wu_ref[...]))
    v = jax.nn.gelu(inv * _dot_nt(x, wv_ref[...]))
    if want_v:
        refs[has_prev + 1][...] = v
    i = lax.broadcasted_iota(jnp.int32, (R, R), 0)
    j = lax.broadcasted_iota(jnp.int32, (R, R), 1)
    w = jnp.where((j <= i) & (i // chunk == j // chunk), ws_ref[0], 0.0).astype(CD)
    vc = v.astype(CD)
    for c in range(tm // R):
        rows = slice(c * R, (c + 1) * R)
        mixed = _dot(w, vc[rows]) + b_ref[0]
        gm_ref[rows, :] = (u[rows] * mixed).astype(gm_ref.dtype)


def _gmlp(xn, w_uv, uv_row0, layer, ws, bs, n_rows, row_off, tm, R, chunk, prev, want_v):
    x, ss = xn
    M, K = x.shape
    cw = GM_WIDTH // GM_GROUPS
    assert n_rows % tm == 0 and row_off % tm == 0 and tm % R == 0 and cw % LANES == 0
    off = row_off // tm
    blk = 2 * (_nbytes((tm, K), CD) + 2 * _nbytes((K, cw), CD) + 3 * _nbytes((tm, cw), F32)) \
        + 8 * _nbytes((tm, cw), F32)
    assert uv_row0 % cw == 0
    u_spec = pl.BlockSpec((None, cw, K), lambda i, g: (layer, uv_row0 // cw + g, 0))
    v_spec = pl.BlockSpec((None, cw, K), lambda i, g: (layer, uv_row0 // cw + GM_GROUPS + g, 0))
    in_specs = [pl.BlockSpec((tm, K), lambda i, g: (i + off, 0)), _ss_spec(ss, tm, off), u_spec, v_spec,
                pl.BlockSpec((1, R, R), lambda i, g: (g, 0, 0)),
                pl.BlockSpec((1, R, 1), lambda i, g: (g, 0, 0))]
    args = [x, ss, w_uv, w_uv, ws, bs]
    aliases = {}
    if prev is not None:
        in_specs.append(pl.BlockSpec(memory_space=pl.ANY))
        args.append(prev)
        aliases = {6: 0}
    out_specs = [pl.BlockSpec((tm, cw), lambda i, g: (i + off, g))]
    out_shape = [jax.ShapeDtypeStruct((M, GM_WIDTH), CD)]
    if want_v:
        out_specs.append(pl.BlockSpec((tm, cw), lambda i, g: (i, g)))
        out_shape.append(jax.ShapeDtypeStruct((n_rows, GM_WIDTH), F32))
    return pl.pallas_call(
        functools.partial(_gmlp_kernel, chunk=chunk, has_prev=prev is not None, want_v=want_v),
        grid=(n_rows // tm, GM_GROUPS),
        in_specs=in_specs,
        out_specs=out_specs,
        out_shape=out_shape,
        input_output_aliases=aliases,
        compiler_params=_params(("parallel", "parallel"), blk),
        name="gmlp_branch",
    )(*args)


def _compress_rows(load_row, pe_l, w1_lo, w1_hi, w2, xlo, xhi, n_half):
    half = CMP_LEN // 2
    for l in range(half):
        xl = load_row(l)
        cols = slice(l * HEAD_DIM, (l + 1) * HEAD_DIM)
        xlo[:, cols] = (xl + pe_l(l)).astype(CD)
        xhi[:, cols] = (xl + pe_l(half + l)).astype(CD)
    hid = jax.nn.gelu(_dot(xlo[...], w1_lo) + pltpu.roll(_dot(xhi[...], w1_hi), n_half - 1, 0))
    return _dot(hid.astype(CD), w2)


def _compress_prompt_kernel(x_ref, pe_ref, w1_ref, w2_ref, o_ref, xlo, xhi, *, n_half):
    kh = (CMP_LEN // 2) * HEAD_DIM
    tok = _compress_rows(lambda l: x_ref[pl.ds(l, n_half, stride=CMP_STRIDE), :],
                         lambda l: pe_ref[0, l:l + 1, :], w1_ref[0, 0:kh, :], w1_ref[0, kh:2 * kh, :], w2_ref[0],
                         xlo, xhi, n_half)
    o_ref[0, 0] = tok.astype(o_ref.dtype)


def _compress_prompt(kv_c, pe, w1, w2):
    assert CMP_LEN == 2 * CMP_STRIDE and SEQ % (CMP_STRIDE * SUBLANES) == 0
    n_half = SEQ // CMP_STRIDE
    kh = (CMP_LEN // 2) * HEAD_DIM
    blk = 2 * _nbytes((SEQ, HEAD_DIM), F32) + 8 * _nbytes((n_half, HEAD_DIM), F32) \
        + 4 * _nbytes((2 * kh, HEAD_DIM), CD) + 2 * _nbytes((n_half, kh), CD)
    return pl.pallas_call(
        functools.partial(_compress_prompt_kernel, n_half=n_half),
        grid=(BATCH, KVC),
        in_specs=[pl.BlockSpec((SEQ, HEAD_DIM), lambda b, c: (b, c)),
                  pl.BlockSpec((1, CMP_LEN, HEAD_DIM), lambda b, c: (c // KV_HEADS, 0, 0)),
                  pl.BlockSpec((1, 2 * kh, HEAD_DIM), lambda b, c: (c // KV_HEADS, 0, 0)),
                  pl.BlockSpec((1, HEAD_DIM, HEAD_DIM), lambda b, c: (c // KV_HEADS, 0, 0))],
        out_specs=pl.BlockSpec((1, 1, n_half, HEAD_DIM), lambda b, c: (b, c, 0, 0)),
        out_shape=jax.ShapeDtypeStruct((BATCH, KVC, n_half, HEAD_DIM), CD),
        scratch_shapes=[pltpu.VMEM((n_half, kh), CD), pltpu.VMEM((n_half, kh), CD)],
        compiler_params=_params(("parallel", "parallel"), blk),
        name="nsa_compress_prompt",
    )(kv_c, pe, w1, w2)


def _compress_sample_kernel(pt_ref, *refs, n_half):
    page_refs = refs[:CMP_PAGES]
    pe_ref, w1_ref, w2_ref, o_ref, buf, xlo, xhi = refs[CMP_PAGES:]
    step = pl.program_id(1)
    page_rows = PAGE_SIZE * KVC
    kh = (CMP_LEN // 2) * HEAD_DIM
    for k, pr in enumerate(page_refs):
        start = pl.multiple_of((step * CMP_PAGES + k) * PAGE_SIZE, PAGE_SIZE)
        for c in range(KVC):
            buf[c, pl.ds(start, PAGE_SIZE), :] = _head_rows(pr, c, PAGE_SIZE)

    @pl.when(step == pl.num_programs(1) - 1)
    def _():
        def one_slot(c, carry):
            kv = c // KV_HEADS
            tok = _compress_rows(lambda l: buf[c, pl.ds(l, n_half, stride=CMP_STRIDE), :],
                                 lambda l: pe_ref[kv, pl.ds(l, 1), :], w1_ref[kv, pl.ds(0, kh), :],
                                 w1_ref[kv, pl.ds(kh, kh), :], w2_ref[kv], xlo, xhi, n_half)
            o_ref[c] = tok.astype(o_ref.dtype)
            return carry
        lax.fori_loop(0, KVC, one_slot, 0)


def _compress_sample(pool, layer, page_table, pe, w1, w2):
    nb, n_pages = page_table.shape
    assert n_pages % CMP_PAGES == 0 and PAST_LEN % (CMP_STRIDE * SUBLANES) == 0
    n_half = PAST_LEN // CMP_STRIDE
    page_rows = PAGE_SIZE * KVC
    kh = (CMP_LEN // 2) * HEAD_DIM
    blk = _nbytes((PAST_LEN * KVC, HEAD_DIM), F32) + 2 * CMP_PAGES * _nbytes((page_rows, HEAD_DIM), F32) \
        + 4 * _nbytes((2, 2 * kh, HEAD_DIM), CD) + 2 * _nbytes((KVC, n_half, HEAD_DIM), CD) \
        + 2 * _nbytes((n_half, kh), CD) + 10 * _nbytes((n_half, HEAD_DIM), F32)

    def page_spec(k):
        return pl.BlockSpec((None, None, page_rows, HEAD_DIM),
                            lambda b, s, pt: (layer, pt[b, s * CMP_PAGES + k], 0, 0))

    return pl.pallas_call(
        functools.partial(_compress_sample_kernel, n_half=n_half),
        grid_spec=pltpu.PrefetchScalarGridSpec(
            num_scalar_prefetch=1,
            grid=(nb, n_pages // CMP_PAGES),
            in_specs=[page_spec(k) for k in range(CMP_PAGES)]
            + [pl.BlockSpec((2, CMP_LEN, HEAD_DIM), lambda b, s, pt: (0, 0, 0)),
               pl.BlockSpec((2, 2 * kh, HEAD_DIM), lambda b, s, pt: (0, 0, 0)),
               pl.BlockSpec((2, HEAD_DIM, HEAD_DIM), lambda b, s, pt: (0, 0, 0))],
            out_specs=pl.BlockSpec((None, KVC, n_half, HEAD_DIM), lambda b, s, pt: (b, 0, 0, 0)),
            scratch_shapes=[pltpu.VMEM((KVC, PAST_LEN, HEAD_DIM), F32), pltpu.VMEM((n_half, kh), CD),
                            pltpu.VMEM((n_half, kh), CD)]),
        out_shape=jax.ShapeDtypeStruct((nb, KVC, n_half, HEAD_DIM), CD),
        compiler_params=_params(("parallel", "arbitrary"), blk),
        name="nsa_compress_sample",
    )(page_table, *([pool] * CMP_PAGES), pe, w1, w2)


def _masked_softmax(s, mask):
    s = jnp.where(mask, s, NEG_INF)
    m = jnp.max(s, axis=-1, keepdims=True)
    e = jnp.where(mask, jnp.exp(s - m), 0.0)
    d = jnp.sum(e, axis=-1, keepdims=True)
    return e * (1.0 / jnp.where(d > 0.0, d, 1.0))


def _attend_bias(qs, k, v, bias, tq):
    os = []
    for r in range(GROUP):
        t = _dot_nt(qs[r * tq:(r + 1) * tq], k) + bias
        e = jnp.exp2((t - jnp.max(t, axis=-1, keepdims=True)) * (SCALE * LOG2E))
        d = jnp.sum(e, axis=-1, keepdims=True)
        os.append(_dot(e.astype(CD), v) * (1.0 / d))
    return jnp.concatenate(os, axis=0)


def _group_sum_lanes(p, n_out):
    n = p.shape[1]
    src = lax.broadcasted_iota(jnp.int32, (n, n_out), 0) // PER_SEL
    dst = lax.broadcasted_iota(jnp.int32, (n, n_out), 1)
    gmat = (src == dst).astype(CD)
    h1 = p.astype(CD)
    r1 = p - h1.astype(F32)
    h2 = r1.astype(CD)
    h3 = (r1 - h2.astype(F32)).astype(CD)
    return _dot(h1, gmat) + _dot(h2, gmat) + _dot(h3, gmat)


def _stack_heads(q, col0):
    return jnp.concatenate([q[:, col0 + r * HEAD_DIM: col0 + (r + 1) * HEAD_DIM] for r in range(GROUP)], axis=0)


def _gate_sum(gates, col0, o_c, o_s, o_w, r, rows):
    return (gates[:, col0 + r:col0 + r + 1] * o_c[rows]
            + gates[:, col0 + GROUP + r:col0 + GROUP + r + 1] * o_s[rows]
            + gates[:, col0 + 2 * GROUP + r:col0 + 2 * GROUP + r + 1] * o_w[rows])


def _attn_prompt_kernel(q_ref, ks_ref, vs_ref, kw_ref, vw_ref, kc_ref, vc_ref, g_ref, o_ref, *, tq, T):
    i = pl.program_id(2)
    n_sel = T // SEL_LEN
    n_cmp = T // CMP_STRIDE

    qs = _stack_heads(q_ref[...], 0)
    qpos1 = i * tq + lax.broadcasted_iota(jnp.int32, (tq, 1), 0)
    qpos = jnp.concatenate([qpos1] * GROUP, axis=0)

    span = WINDOW + tq
    start = pl.multiple_of(jnp.maximum(i * tq - WINDOW, 0), tq)
    kposw = start + lax.broadcasted_iota(jnp.int32, (1, span), 1)
    bias_w = jnp.where((kposw <= qpos1) & (kposw > qpos1 - WINDOW), 0.0, NEG_INF)
    o_w = _attend_bias(qs, kw_ref[pl.ds(start, span), :], vw_ref[pl.ds(start, span), :], bias_w, tq)

    s = _dot_nt(qs, kc_ref[0, 0]) * SCALE
    blk_end = lax.broadcasted_iota(jnp.int32, (1, n_cmp), 1) * CMP_STRIDE + (CMP_LEN - 1)
    p = _masked_softmax(s, (blk_end <= qpos) & (blk_end < T))
    o_c = _dot(p.astype(CD), vc_ref[0, 0])
    psum = p[0:tq]
    for r in range(1, GROUP):
        psum = psum + p[r * tq:(r + 1) * tq]
    imp = _group_sum_lanes(psum, LANES)

    j = lax.broadcasted_iota(jnp.int32, (1, LANES), 1)
    cur = qpos1 // SEL_LEN
    forced = (j == 0) | (j == cur) | (j == cur - 1)
    score = jnp.where(forced, FORCE, imp)
    score = jnp.where(j <= cur, score, -1.0)
    sc = score.T[:n_sel]
    jrow = lax.broadcasted_iota(jnp.int32, (n_sel, 1), 0)
    rank = jnp.zeros((n_sel, tq), jnp.int32)
    for a in range(n_sel):
        sa = sc[a:a + 1, :]
        rank = rank + ((sa > sc) | ((sa == sc) & (a < jrow))).astype(jnp.int32)
    sel_t = ((rank < SEL_TOP) & (sc >= 0.0)).astype(F32)
    if n_sel < LANES:
        sel_t = jnp.concatenate([sel_t, jnp.zeros((LANES - n_sel, tq), F32)], axis=0)
    sel = sel_t.T.astype(CD)

    def sel_branch(nk):
        def run():
            expand = (lax.broadcasted_iota(jnp.int32, (LANES, nk), 1) // SEL_LEN
                      == lax.broadcasted_iota(jnp.int32, (LANES, nk), 0)).astype(CD)
            key_sel = _dot(sel, expand)
            kpos = lax.broadcasted_iota(jnp.int32, (1, nk), 1)
            bias = jnp.where((key_sel > 0.5) & (kpos <= qpos1), 0.0, NEG_INF)
            return _attend_bias(qs, ks_ref[0:nk, :], vs_ref[0:nk, :], bias, tq)
        return run

    n_ext = T // SEL_KEY_STEP
    o_s = lax.switch((i * tq) // SEL_KEY_STEP, [sel_branch(SEL_KEY_STEP * (k + 1)) for k in range(n_ext)])

    g = g_ref[...]
    for r in range(GROUP):
        o = _gate_sum(g, 0, o_c, o_s, o_w, r, slice(r * tq, (r + 1) * tq))
        o_ref[:, r * HEAD_DIM:(r + 1) * HEAD_DIM] = o.astype(o_ref.dtype)


def _attn_prompt(q, kv_sel, kv_win, kc, gates):
    T = SEQ
    tq = Q_TILE
    assert T % tq == 0 and WINDOW % tq == 0 and WINDOW + tq <= T and T // SEL_LEN <= LANES
    assert T // CMP_STRIDE == LANES and T % SEL_KEY_STEP == 0 and SEL_KEY_STEP % tq == 0
    nq = T // tq
    gw = GROUP * HEAD_DIM
    blk = 2 * 4 * _nbytes((T, HEAD_DIM), CD) + 8 * _nbytes((GROUP * tq, T), F32) + 4 * _nbytes((tq, gw), F32)
    k_spec = pl.BlockSpec((T, HEAD_DIM), lambda b, g, i: (b, g))
    v_spec = pl.BlockSpec((T, HEAD_DIM), lambda b, g, i: (b, KV_HEADS + g))
    return pl.pallas_call(
        functools.partial(_attn_prompt_kernel, tq=tq, T=T),
        grid=(BATCH, KV_HEADS, nq),
        in_specs=[pl.BlockSpec((tq, gw), lambda b, g, i: (b * nq + i, g)),
                  k_spec, v_spec, k_spec, v_spec,
                  pl.BlockSpec((1, 1, T // CMP_STRIDE, HEAD_DIM), lambda b, g, i: (b, g, 0, 0)),
                  pl.BlockSpec((1, 1, T // CMP_STRIDE, HEAD_DIM), lambda b, g, i: (b, KV_HEADS + g, 0, 0)),
                  pl.BlockSpec((tq, LANES), lambda b, g, i: (b * nq + i, g))],
        out_specs=pl.BlockSpec((tq, gw), lambda b, g, i: (b * nq + i, g)),
        out_shape=jax.ShapeDtypeStruct((q.shape[0], ATTN_W), CD),
        compiler_params=_params(("parallel", "parallel", "parallel"), blk),
        name="nsa_prompt_attention",
    )(q, kv_sel, kv_sel, kv_win, kv_win, kc, kc, gates)


def _sample_cmp_kernel(q_ref, kc_ref, vc_ref, oc_ref, sel_ref, *, n_cmp_valid):
    T = DEC_SEQ
    n_cmp = kc_ref.shape[2]
    qs = _stack_heads(q_ref[...], 0).astype(CD)
    qpos1 = PAST_LEN + lax.broadcasted_iota(jnp.int32, (T, 1), 0)
    qpos = jnp.concatenate([qpos1] * GROUP, axis=0)
    s = _dot_nt(qs, kc_ref[0, 0]) * SCALE
    n = lax.broadcasted_iota(jnp.int32, (1, n_cmp), 1)
    p = _masked_softmax(s, (n * CMP_STRIDE + (CMP_LEN - 1) <= qpos) & (n < n_cmp_valid))
    oc_ref[0, 0] = _dot(p.astype(CD), vc_ref[0, 0])
    psum = p[0:T]
    for r in range(1, GROUP):
        psum = psum + p[r * T:(r + 1) * T]
    imp = _group_sum_lanes(psum, LANES)

    j = lax.broadcasted_iota(jnp.int32, (1, LANES), 1)
    cur = qpos1 // SEL_LEN
    score = jnp.where((j == 0) | (j == cur - 1), FORCE, imp)
    rank = jnp.zeros((T, LANES), jnp.int32)
    for sh in range(1, LANES):
        other = pltpu.roll(score, sh, 1)
        oj = jnp.where(j >= sh, j - sh, j - sh + LANES)
        rank = rank + ((other > score) | ((other == score) & (oj < j))).astype(jnp.int32)
    sel_ref[0, 0] = (rank < SEL_TOP - 1).astype(F32)


def _sample_cmp(q_s, kc_s, n_cmp_valid):
    nb = DEC_BATCH
    gw = GROUP * HEAD_DIM
    n_cmp = kc_s.shape[2]
    blk = 16 * _nbytes((GROUP * DEC_SEQ, n_cmp), F32) + 4 * _nbytes((n_cmp, HEAD_DIM), CD)
    return pl.pallas_call(
        functools.partial(_sample_cmp_kernel, n_cmp_valid=n_cmp_valid),
        grid=(nb, KV_HEADS),
        in_specs=[pl.BlockSpec((DEC_SEQ, gw), lambda b, g: (b, g)),
                  pl.BlockSpec((1, 1, n_cmp, HEAD_DIM), lambda b, g: (b, g, 0, 0)),
                  pl.BlockSpec((1, 1, n_cmp, HEAD_DIM), lambda b, g: (b, KV_HEADS + g, 0, 0))],
        out_specs=[pl.BlockSpec((1, 1, GROUP * DEC_SEQ, HEAD_DIM), lambda b, g: (b, g, 0, 0)),
                   pl.BlockSpec((1, 1, DEC_SEQ, LANES), lambda b, g: (b, g, 0, 0))],
        out_shape=[jax.ShapeDtypeStruct((nb, KV_HEADS, GROUP * DEC_SEQ, HEAD_DIM), F32),
                   jax.ShapeDtypeStruct((nb, KV_HEADS, DEC_SEQ, LANES), F32)],
        compiler_params=_params(("parallel", "parallel"), blk),
        name="nsa_sample_cmp_attention",
    )(q_s, kc_s, kc_s)


def _put_rows_kernel(src_ref, dst_any, o_ref):
    o_ref[...] = src_ref[...].astype(o_ref.dtype)


def _put_rows(dst, src, row_off):
    n, w = src.shape
    assert row_off % n == 0 and n % 16 == 0
    return pl.pallas_call(
        _put_rows_kernel,
        grid=(1,),
        in_specs=[pl.BlockSpec((n, w), lambda i: (0, 0)), pl.BlockSpec(memory_space=pl.ANY)],
        out_specs=pl.BlockSpec((n, w), lambda i: (row_off // n, 0)),
        out_shape=jax.ShapeDtypeStruct(dst.shape, dst.dtype),
        input_output_aliases={1: 0},
        compiler_params=_params(("arbitrary",), 4 * _nbytes((n, w), F32)),
        name="put_sample_rows",
    )(src, dst)


def _pad_rows(x, rows):
    return jnp.concatenate([x, jnp.zeros((rows - x.shape[0], x.shape[1]), x.dtype)], axis=0)


def _head_rows(ref, slot, n):
    return ref[pl.ds(slot, n, stride=KVC), :]


def _sample_sel_kernel(pt_ref, q_ref, sel_ref, *refs):
    page_refs = refs[:SEL_PAGES]
    new_ref, o_ref, m_sc, l_sc, acc_sc = refs[SEL_PAGES:]
    T = DEC_SEQ
    R = GROUP * T
    step = pl.program_id(1)
    nk = SEL_PAGES * PAGE_SIZE

    @pl.when(step == 0)
    def _():
        m_sc[...] = jnp.full(m_sc.shape, NEG_INF, F32)
        l_sc[...] = jnp.zeros(l_sc.shape, F32)
        acc_sc[...] = jnp.zeros(acc_sc.shape, F32)

    def update(g, s, mask, v):
        rows = slice(g * R, (g + 1) * R)
        m_prev = m_sc[rows]
        m_new = jnp.maximum(m_prev, jnp.max(jnp.where(mask, s, NEG_INF), axis=-1, keepdims=True))
        alpha = jnp.exp(m_prev - m_new)
        e = jnp.where(mask, jnp.exp(s - m_new), 0.0)
        l_sc[rows] = alpha * l_sc[rows] + jnp.sum(e, axis=-1, keepdims=True)
        acc_sc[rows] = alpha * acc_sc[rows] + _dot(e.astype(CD), v)
        m_sc[rows] = m_new

    q = q_ref[...]
    blk_of_key = step * (nk // SEL_LEN) + lax.broadcasted_iota(jnp.int32, (LANES, nk), 1) // SEL_LEN
    expand = (blk_of_key == lax.broadcasted_iota(jnp.int32, (LANES, nk), 0)).astype(CD)
    key_sel = _dot(sel_ref[...].reshape(KV_HEADS * T, LANES).astype(CD), expand)
    for g in range(KV_HEADS):
        qg = _stack_heads(q, g * GROUP * HEAD_DIM).astype(CD)
        k = jnp.concatenate([_head_rows(pr, g, PAGE_SIZE) for pr in page_refs], axis=0).astype(CD)
        v = jnp.concatenate([_head_rows(pr, KV_HEADS + g, PAGE_SIZE) for pr in page_refs], axis=0).astype(CD)
        mask = jnp.concatenate([key_sel[g * T:(g + 1) * T]] * GROUP, axis=0) > 0.5
        update(g, _dot_nt(qg, k) * SCALE, mask, v)

    @pl.when(step == pl.num_programs(1) - 1)
    def _():
        t = jnp.concatenate([lax.broadcasted_iota(jnp.int32, (T, 1), 0)] * GROUP, axis=0)
        i = lax.broadcasted_iota(jnp.int32, (1, LANES), 1)
        mask = (i < T) & (i <= t)
        for g in range(KV_HEADS):
            qg = _stack_heads(q, g * GROUP * HEAD_DIM).astype(CD)
            k = _pad_rows(_head_rows(new_ref, g, T), LANES).astype(CD)
            v = _pad_rows(_head_rows(new_ref, KV_HEADS + g, T), LANES).astype(CD)
            update(g, _dot_nt(qg, k) * SCALE, mask, v)
        o_ref[...] = (acc_sc[...] * (1.0 / l_sc[...])).reshape(o_ref.shape)


def _sample_sel(q_s, sel, pool, layer, page_table, new_rows):
    nb, n_pages = page_table.shape
    assert n_pages % SEL_PAGES == 0 and (SEL_PAGES * PAGE_SIZE) % SEL_LEN == 0
    R = GROUP * DEC_SEQ
    page_rows = PAGE_SIZE * KVC
    blk = 2 * SEL_PAGES * _nbytes((page_rows, HEAD_DIM), F32) + 16 * _nbytes((R, SEL_PAGES * PAGE_SIZE), F32) \
        + 8 * _nbytes((SEL_PAGES * PAGE_SIZE, HEAD_DIM), F32)

    def page_spec(k):
        return pl.BlockSpec((None, None, page_rows, HEAD_DIM),
                            lambda b, s, pt: (layer, pt[b, s * SEL_PAGES + k], 0, 0))

    return pl.pallas_call(
        _sample_sel_kernel,
        grid_spec=pltpu.PrefetchScalarGridSpec(
            num_scalar_prefetch=1,
            grid=(nb, n_pages // SEL_PAGES),
            in_specs=[pl.BlockSpec((DEC_SEQ, ATTN_W), lambda b, s, pt: (b, 0)),
                      pl.BlockSpec((None, KV_HEADS, DEC_SEQ, LANES), lambda b, s, pt: (b, 0, 0, 0))]
            + [page_spec(k) for k in range(SEL_PAGES)]
            + [pl.BlockSpec((DEC_SEQ * KVC, HEAD_DIM), lambda b, s, pt: (b, 0))],
            out_specs=pl.BlockSpec((None, KV_HEADS, R, HEAD_DIM), lambda b, s, pt: (b, 0, 0, 0)),
            scratch_shapes=[pltpu.VMEM((KV_HEADS * R, 1), F32), pltpu.VMEM((KV_HEADS * R, 1), F32),
                            pltpu.VMEM((KV_HEADS * R, HEAD_DIM), F32)]),
        out_shape=jax.ShapeDtypeStruct((nb, KV_HEADS, R, HEAD_DIM), F32),
        compiler_params=_params(("parallel", "arbitrary"), blk),
        name="nsa_sample_sel_attention",
    )(page_table, q_s, sel, *([pool] * SEL_PAGES), new_rows)


def _sample_win_kernel(q_ref, buf_ref, new_ref, oc_ref, os_ref, g_ref, o_ref, *, wb):
    T = DEC_SEQ
    q = q_ref[...]
    gates = g_ref[...]
    t = jnp.concatenate([lax.broadcasted_iota(jnp.int32, (T, 1), 0)] * GROUP, axis=0)
    qpos = PAST_LEN + t
    i = lax.broadcasted_iota(jnp.int32, (1, wb + LANES), 1)
    kpos = PAST_LEN - wb + i
    mask = (i < wb + T) & (kpos <= qpos) & (kpos > qpos - WINDOW)
    for g in range(KV_HEADS):
        qg = _stack_heads(q, g * GROUP * HEAD_DIM).astype(CD)
        k = jnp.concatenate([_head_rows(buf_ref, g, wb), _pad_rows(_head_rows(new_ref, g, T), LANES)],
                            axis=0).astype(CD)
        v = jnp.concatenate([_head_rows(buf_ref, KV_HEADS + g, wb),
                             _pad_rows(_head_rows(new_ref, KV_HEADS + g, T), LANES)], axis=0).astype(CD)
        p = _masked_softmax(_dot_nt(qg, k) * SCALE, mask)
        o_w = _dot(p.astype(CD), v)
        o_c = oc_ref[g]
        o_s = os_ref[g]
        for r in range(GROUP):
            o = _gate_sum(gates, g * LANES, o_c, o_s, o_w, r, slice(r * T, (r + 1) * T))
            col = (g * GROUP + r) * HEAD_DIM
            o_ref[:, col:col + HEAD_DIM] = o


def _sample_win(q_s, win_buf, layer, new_rows, o_c, o_s, gates_s):
    nb = DEC_BATCH
    wb = win_buf.shape[2] // KVC
    R = GROUP * DEC_SEQ
    assert wb % LANES == 0
    blk = 2 * _nbytes((wb * KVC, HEAD_DIM), F32) + 8 * _nbytes((R, wb + LANES), F32) \
        + 8 * _nbytes((wb + LANES, HEAD_DIM), F32)
    return pl.pallas_call(
        functools.partial(_sample_win_kernel, wb=wb),
        grid=(nb,),
        in_specs=[pl.BlockSpec((DEC_SEQ, ATTN_W), lambda b: (b, 0)),
                  pl.BlockSpec((None, None, wb * KVC, HEAD_DIM), lambda b: (layer, b, 0, 0)),
                  pl.BlockSpec((DEC_SEQ * KVC, HEAD_DIM), lambda b: (b, 0)),
                  pl.BlockSpec((None, KV_HEADS, R, HEAD_DIM), lambda b: (b, 0, 0, 0)),
                  pl.BlockSpec((None, KV_HEADS, R, HEAD_DIM), lambda b: (b, 0, 0, 0)),
                  pl.BlockSpec((DEC_SEQ, KV_HEADS * LANES), lambda b: (b, 0))],
        out_specs=pl.BlockSpec((DEC_SEQ, ATTN_W), lambda b: (b, 0)),
        out_shape=jax.ShapeDtypeStruct((nb * DEC_SEQ, ATTN_W), F32),
        compiler_params=_params(("parallel",), blk),
        name="nsa_sample_win_attention",
    )(q_s, win_buf, new_rows, o_c, o_s, gates_s)


def _half_ffn(h, hn, w_gate, w_up, w_down, layer, g_next):
    act = _swiglu(hn, w_gate, w_up, layer)
    return _resid_proj(act, w_down, layer, h, 0.5, g_next, ROW_TILE // 2, COL_TILE // 2, "ffn_down")


W_IN_ROW_TILE = 512


def _w_in_cast_kernel(w_ref, o_ref):
    o_ref[...] = w_ref[0].astype(o_ref.dtype)


def _prep_w_in(w_in):
    o1 = ATTN_W
    o4 = o1 + 6 * KV_W
    n_gate = 3 * N_HEADS
    wt = jnp.swapaxes(w_in, 1, 2)
    L, n_in, D = wt.shape
    n_out = n_in - n_gate
    tr = W_IN_ROW_TILE
    assert o4 % tr == 0 and n_out % tr == 0 and n_gate % 16 == 0

    def src_row(j):
        return pl.multiple_of(j * tr + jnp.where(j * tr >= o4, n_gate, 0), 16)

    blk = 2 * (_nbytes((tr, D), F32) + _nbytes((tr, D), CD)) + 2 * _nbytes((tr, D), F32)
    main = pl.pallas_call(
        _w_in_cast_kernel,
        grid=(L, n_out // tr),
        in_specs=[pl.BlockSpec((pl.Element(1), pl.Element(tr), pl.Element(D)), lambda l, j: (l, src_row(j), 0))],
        out_specs=pl.BlockSpec((None, tr, D), lambda l, j: (l, j, 0)),
        out_shape=jax.ShapeDtypeStruct((L, n_out, D), CD),
        compiler_params=_params(("parallel", "parallel"), blk),
        name="w_in_cast",
    )(wt)
    wg = wt[:, o4:o4 + n_gate].reshape(L, 3, KV_HEADS, GROUP, D).transpose(0, 2, 1, 3, 4)
    wg = jnp.pad(wg.reshape(L, KV_HEADS, 3 * GROUP, D), ((0, 0), (0, 0), (0, LANES - 3 * GROUP), (0, 0)))
    return dict(main=main, gate=wg.reshape(L, KV_HEADS * LANES, D).astype(CD))


def _layer(h, hn, layer, W, small, caches, page_table, kv_bufs, g_after):
    (nm, pe, w1, w2, ws, bs, n2) = small
    cache_cmp, cache_sel, cache_win = caches
    MP = BATCH * SEQ
    MS = DEC_BATCH * DEC_SEQ
    n_pages = page_table.shape[1]
    assert n_pages * PAGE_SIZE == PAST_LEN and PAST_LEN // SEL_LEN == LANES and DEC_SEQ <= SEL_LEN
    assert DEC_SEQ == SUBLANES and MP % MS == 0

    h, n = _half_ffn(h, hn, W["g1"], W["u1"], W["d1"], layer, nm)

    wm = W["main"]
    kv0 = ATTN_W
    uv0 = kv0 + 6 * KV_W
    mg0 = uv0 + 2 * GM_WIDTH
    q = _proj(n, wm, 0, ATTN_W, layer, None, CD, "proj_q")
    gates = _proj(n, W["gate"], 0, KV_HEADS * LANES, layer, "sigmoid", F32, "proj_branch_gates")
    cmp_buf, kv_c = _kv_proj_prompt(n, wm, kv0, layer, kv_bufs[0], F32, False, "proj_kv_cmp")
    sel_buf, kv_sel = _kv_proj_prompt(n, wm, kv0 + 2 * KV_W, layer, kv_bufs[1], CD, False, "proj_kv_sel")
    win_buf, kv_win = _kv_proj_prompt(n, wm, kv0 + 4 * KV_W, layer, kv_bufs[2], CD, True, "proj_kv_win")
    new_s = [_kv_proj_sample(n, wm, kv0 + s * 2 * KV_W, layer, "proj_kv_sample") for s in range(3)]

    w1c = w1.astype(CD).reshape(2, CMP_LEN * HEAD_DIM, HEAD_DIM)
    w2c = w2.astype(CD)

    kc_p = _compress_prompt(kv_c, pe, w1c, w2c)
    o_p = _attn_prompt(q, kv_sel, kv_win, kc_p, gates)
    (gm,) = _gmlp(n, wm, uv0, layer, ws, bs[:, :, None], MP, 0, 1024, GM_CHUNK, GM_CHUNK, None, False)

    q_s = q[MP:].astype(F32)
    gates_s = gates[MP:]
    kc_s = _compress_sample(cache_cmp, layer, page_table, pe, w1c, w2c)
    n_cmp_valid = (PAST_LEN + DEC_SEQ - CMP_LEN) // CMP_STRIDE + 1
    oc_s, sel_s = _sample_cmp(q_s, kc_s, n_cmp_valid)
    os_s = _sample_sel(q_s, sel_s, cache_sel, layer, page_table, new_s[1])
    o_s = _sample_win(q_s, cache_win, layer, new_s[2], oc_s, os_s, gates_s)
    cl = min(DEC_SEQ, GM_CHUNK)
    ws_s = jnp.tile(ws[:, :cl, :cl], (1, MS // cl, MS // cl))
    bs_s = jnp.tile(bs[:, :cl], (1, MS // cl))[:, :, None]
    gm, v_s = _gmlp(n, wm, uv0, layer, ws_s, bs_s, MS, MP, MS, MS, cl, gm, True)

    o = _put_rows(o_p, o_s, MP)
    merged = _merge(n, o, gm, wm, mg0, W["wua"], W["wug"], layer)
    h, n = _resid_proj(merged, W["wo"], layer, h, 1.0, n2, ROW_TILE, COL_TILE // 2, "mixer_out")
    h, hn = _half_ffn(h, n, W["g2"], W["u2"], W["d2"], layer, g_after)
    return h, hn, (cmp_buf, sel_buf, win_buf), (new_s, v_s)


def kernel(x_prompt, x_sample, cache_cmp_kv, cache_sel_kv, cache_win_kv, page_table, norm_ffn1, ffn1_gate, ffn1_up, ffn1_down, norm_mix, w_in, cmp_pe, cmp_w1, cmp_w2, gm_ws, gm_bs, w_up_attn, w_up_gm, w_out, norm_ffn2, ffn2_gate, ffn2_up, ffn2_down, norm_final):
    MP = BATCH * SEQ
    MS = DEC_BATCH * DEC_SEQ
    kv_row = (2, KV_HEADS, HEAD_DIM)
    h = jnp.concatenate([x_prompt.reshape(MP, D_MODEL), x_sample.reshape(MS, D_MODEL)], axis=0)
    n_phys = cache_cmp_kv.shape[1]
    wb = cache_win_kv.shape[2]
    caches = (cache_cmp_kv.reshape(DEPTH, n_phys, PAGE_SIZE * KVC, HEAD_DIM),
              cache_sel_kv.reshape(DEPTH, n_phys, PAGE_SIZE * KVC, HEAD_DIM),
              cache_win_kv.reshape(DEPTH, DEC_BATCH, wb * KVC, HEAD_DIM))
    W = _prep_w_in(w_in)
    W.update(g1=ffn1_gate, u1=ffn1_up, d1=ffn1_down.astype(CD),
             g2=ffn2_gate, u2=ffn2_up, d2=ffn2_down.astype(CD),
             wua=w_up_attn, wug=w_up_gm, wo=w_out)
    kv_bufs = (None, None, None)
    sample_outs = []
    hn = _norm_prep(h, norm_ffn1[0])
    for l in range(DEPTH):
        small = (norm_mix[l], cmp_pe[l], cmp_w1[l], cmp_w2[l], gm_ws[l], gm_bs[l], norm_ffn2[l])
        g_after = norm_ffn1[l + 1] if l + 1 < DEPTH else None
        h, hn, kv_bufs, s_out = _layer(h, hn, l, W, small, caches, page_table, kv_bufs, g_after)
        sample_outs.append(s_out)
    y_prompt = _rmsnorm(h, norm_final, F32, rows=MP, row_off=0).reshape(BATCH, SEQ, D_MODEL)
    y_sample = _rmsnorm(h, norm_final, F32, rows=MS, row_off=MP, tile=MS).reshape(DEC_BATCH, DEC_SEQ, D_MODEL)

    cmp_buf, sel_buf, win_buf = kv_bufs
    wk = min(WINDOW, SEQ)
    new = [jnp.stack([sample_outs[l][0][s] for l in range(DEPTH)]).reshape((DEPTH, DEC_BATCH, DEC_SEQ) + kv_row)
           for s in range(3)]
    win_all = jnp.concatenate([cache_win_kv, new[2]], axis=2)
    win_s = win_all[:, :, -min(WINDOW, win_all.shape[2]):]
    gm_v = jnp.stack([sample_outs[l][1] for l in range(DEPTH)]).reshape(DEPTH, DEC_BATCH, DEC_SEQ, GM_WIDTH)
    return (y_prompt, y_sample,
            cmp_buf.reshape((DEPTH, BATCH, SEQ) + kv_row), new[0],
            sel_buf.reshape((DEPTH, BATCH, SEQ) + kv_row), new[1],
            win_buf.reshape((DEPTH, BATCH, wk) + kv_row), win_s, gm_v)
```

```python
import functools

import jax
import jax.numpy as jnp
from jax import lax
from jax.experimental import pallas as pl
from jax.experimental.pallas import tpu as pltpu

D_MODEL = 4096
BATCH = 4
SEQ = 2048
DEPTH = 4
DEC_BATCH = 8
DEC_SEQ = 8
PAST_LEN = 8192
PAGE_SIZE = 128

N_HEADS = 16
KV_HEADS = 4
HEAD_DIM = 128
GROUP = N_HEADS // KV_HEADS
ATTN_W = N_HEADS * HEAD_DIM
KV_W = KV_HEADS * HEAD_DIM
CMP_LEN = 32
CMP_STRIDE = 16
SEL_LEN = 64
SEL_TOP = 16
WINDOW = 512
GM_WIDTH = D_MODEL // 2
GM_GROUPS = 4
GM_CHUNK = 128
D_FF = 11008
EPS = 1e-6
NEG_INF = -1e30
FORCE = 1e9
SCALE = HEAD_DIM ** -0.5
LOG2E = 1.4426950408889634

LANES = 128
SUBLANES = 8
VMEM_BUDGET = 56 * 1024 * 1024
CD = jnp.bfloat16
F32 = jnp.float32
PER_SEL = SEL_LEN // CMP_STRIDE
KVC = 2 * KV_HEADS

ROW_TILE = 1376
COL_TILE = 512
Q_TILE = 512
SEL_KEY_STEP = 512
KV_ROW_TILE = 512
SEL_PAGES = 8
CMP_PAGES = 8


def _pick_tile(n, target, mult):
    best = n
    for t in range(mult, min(n, target) + 1, mult):
        if n % t == 0:
            best = t
    return best


def _params(sem, block_bytes):
    limit = min(VMEM_BUDGET, max(16 * 1024 * 1024, int(block_bytes)))
    return pltpu.CompilerParams(dimension_semantics=sem, vmem_limit_bytes=limit)


def _nbytes(shape, dtype):
    n = 1
    for s in shape:
        n *= s
    return n * jnp.dtype(dtype).itemsize


def _dot(a, b):
    return jnp.dot(a, b, preferred_element_type=F32)


def _dot_nt(a, b):
    return lax.dot_general(a, b, (((1,), (1,)), ((), ())), preferred_element_type=F32)


def _rms_kernel(x_ref, g_ref, o_ref):
    x = x_ref[...]
    y = x * lax.rsqrt(jnp.mean(x * x, axis=-1, keepdims=True) + EPS)
    o_ref[...] = (y * g_ref[...]).astype(o_ref.dtype)


def _rmsnorm(x, g, out_dtype, rows=None, row_off=0, tile=None):
    M, D = x.shape
    rows = M if rows is None else rows
    tm = tile or _pick_tile(rows, 344, 16)
    assert rows % tm == 0 and row_off % tm == 0
    off = row_off // tm
    blk = 2 * _nbytes((tm, D), F32) * 2 + 2 * _nbytes((tm, D), out_dtype)
    return pl.pallas_call(
        _rms_kernel,
        grid=(rows // tm,),
        in_specs=[pl.BlockSpec((tm, D), lambda i: (i + off, 0)),
                  pl.BlockSpec((1, D), lambda i: (0, 0))],
        out_specs=pl.BlockSpec((tm, D), lambda i: (i, 0)),
        out_shape=jax.ShapeDtypeStruct((rows, D), out_dtype),
        compiler_params=_params(("parallel",), blk),
        name="rmsnorm",
    )(x, g.reshape(1, D))


def _inv_rms(ss_ref):
    return lax.rsqrt(ss_ref[...] * (1.0 / D_MODEL) + EPS)


def _ss_spec(ss, tm, off=0):
    return pl.BlockSpec((tm, 1), lambda i, *_: (i + off, 0))


def _norm_prep_kernel(x_ref, g_ref, hb_ref, ss_ref):
    x = x_ref[...]
    hb_ref[...] = (x * g_ref[...]).astype(hb_ref.dtype)
    ss_ref[...] = jnp.sum(x * x, axis=-1, keepdims=True)


def _norm_prep(x, g):
    M, D = x.shape
    tm = _pick_tile(M, 344, 16)
    blk = 4 * _nbytes((tm, D), F32) + 2 * _nbytes((tm, D), CD)
    return pl.pallas_call(
        _norm_prep_kernel,
        grid=(M // tm,),
        in_specs=[pl.BlockSpec((tm, D), lambda i: (i, 0)),
                  pl.BlockSpec((1, D), lambda i: (0, 0))],
        out_specs=[pl.BlockSpec((tm, D), lambda i: (i, 0)), pl.BlockSpec((tm, 1), lambda i: (i, 0))],
        out_shape=[jax.ShapeDtypeStruct((M, D), CD), jax.ShapeDtypeStruct((M, 1), F32)],
        compiler_params=_params(("parallel",), blk),
        name="norm_prep",
    )(x, g.reshape(1, D))


def _proj_kernel(x_ref, ss_ref, w_ref, o_ref, *, act):
    z = _inv_rms(ss_ref) * _dot_nt(x_ref[...], w_ref[...])
    if act == "sigmoid":
        z = jax.nn.sigmoid(z)
    o_ref[...] = z.astype(o_ref.dtype)


def _proj(xn, wt, row0, N, layer, act, out_dtype, name):
    x, ss = xn
    M, K = x.shape
    tm = _pick_tile(M, ROW_TILE, 16)
    tn = _pick_tile(N, COL_TILE, LANES)
    assert row0 % tn == 0
    blk = 2 * (_nbytes((tm, K), x.dtype) + _nbytes((K, tn), wt.dtype) + _nbytes((tm, tn), out_dtype)) \
        + 4 * _nbytes((tm, tn), F32)
    return pl.pallas_call(
        functools.partial(_proj_kernel, act=act),
        grid=(M // tm, N // tn),
        in_specs=[pl.BlockSpec((tm, K), lambda i, j: (i, 0)), _ss_spec(ss, tm),
                  pl.BlockSpec((None, tn, K), lambda i, j: (layer, row0 // tn + j, 0))],
        out_specs=pl.BlockSpec((tm, tn), lambda i, j: (i, j)),
        out_shape=jax.ShapeDtypeStruct((M, N), out_dtype),
        compiler_params=_params(("parallel", "parallel"), blk),
        name=name,
    )(x, ss, wt)


def _swiglu_kernel(x_ref, ss_ref, wg_ref, wu_ref, o_ref):
    x = x_ref[...]
    inv = _inv_rms(ss_ref)
    a = inv * _dot(x, wg_ref[...].astype(CD))
    b = inv * _dot(x, wu_ref[...].astype(CD))
    o_ref[...] = (jax.nn.silu(a) * b).astype(o_ref.dtype)


def _swiglu(xn, wg, wu, layer):
    x, ss = xn
    M, K = x.shape
    N = wg.shape[2]
    tm = _pick_tile(M, ROW_TILE, 16)
    tn = _pick_tile(N, COL_TILE // 2, LANES)
    blk = 2 * (_nbytes((tm, K), x.dtype) + 2 * _nbytes((K, tn), wg.dtype) + _nbytes((tm, tn), CD)) \
        + 2 * _nbytes((K, tn), CD) + 6 * _nbytes((tm, tn), F32)
    w_spec = pl.BlockSpec((None, K, tn), lambda i, j: (layer, 0, j))
    return pl.pallas_call(
        _swiglu_kernel,
        grid=(M // tm, N // tn),
        in_specs=[pl.BlockSpec((tm, K), lambda i, j: (i, 0)), _ss_spec(ss, tm), w_spec, w_spec],
        out_specs=pl.BlockSpec((tm, tn), lambda i, j: (i, j)),
        out_shape=jax.ShapeDtypeStruct((M, N), CD),
        compiler_params=_params(("parallel", "parallel"), blk),
        name="ffn_swiglu",
    )(x, ss, wg, wu)


def _resid_kernel(x_ref, w_ref, h_ref, *refs, scale, emit_norm):
    hn = h_ref[...] + scale * _dot(x_ref[...], w_ref[...].astype(CD))
    if emit_norm:
        g_ref, o_ref, hb_ref, ss_ref = refs
        hb_ref[...] = (hn * g_ref[...]).astype(hb_ref.dtype)
        part = jnp.sum(hn * hn, axis=-1, keepdims=True)
        j = pl.program_id(1)

        @pl.when(j == 0)
        def _():
            ss_ref[...] = part

        @pl.when(j > 0)
        def _():
            ss_ref[...] += part
    else:
        (o_ref,) = refs
    o_ref[...] = hn


def _resid_proj(x, w, layer, h, scale, g_next, tm_target, tn_target, name):
    M, K = x.shape
    N = w.shape[2]
    tm = _pick_tile(M, tm_target, 16)
    tn = _pick_tile(N, tn_target, LANES)
    nj = N // tn
    emit = g_next is not None
    blk = 2 * (_nbytes((tm, K), x.dtype) + _nbytes((K, tn), w.dtype) + 3 * _nbytes((tm, tn), F32)) \
        + (w.dtype != CD) * _nbytes((K, tn), CD) + 4 * _nbytes((tm, tn), F32)
    tile = pl.BlockSpec((tm, tn), lambda i, j: (i, j))
    in_specs = [pl.BlockSpec((tm, K), lambda i, j: (i, 0)),
                pl.BlockSpec((None, K, tn), lambda i, j: (layer, 0, j)), tile]
    args = [x, w, h]
    out_specs = [tile]
    out_shape = [jax.ShapeDtypeStruct((M, N), F32)]
    if emit:
        in_specs.append(pl.BlockSpec((1, tn), lambda i, j: (0, j)))
        args.append(g_next.reshape(1, N))
        out_specs += [tile, pl.BlockSpec((tm, 1), lambda i, j: (i, 0))]
        out_shape += [jax.ShapeDtypeStruct((M, N), CD), jax.ShapeDtypeStruct((M, 1), F32)]
    outs = pl.pallas_call(
        functools.partial(_resid_kernel, scale=scale, emit_norm=emit),
        grid=(M // tm, nj),
        in_specs=in_specs,
        out_specs=out_specs,
        out_shape=out_shape,
        compiler_params=_params(("parallel", "arbitrary"), blk),
        name=name,
    )(*args)
    return (outs[0], (outs[1], outs[2])) if emit else (outs[0], None)


def _merge_kernel(n_ref, ss_ref, o_ref, gm_ref, wma_ref, wmg_ref, wa_ref, wg_ref, out_ref):
    n = n_ref[...]
    inv = _inv_rms(ss_ref)
    ga = jax.nn.sigmoid(inv * _dot_nt(n, wma_ref[...]))
    gg = jax.nn.sigmoid(inv * _dot_nt(n, wmg_ref[...]))
    a = _dot(o_ref[...], wa_ref[...].astype(CD))
    g = _dot(gm_ref[...], wg_ref[...].astype(CD))
    out_ref[...] = (ga * a + gg * g).astype(out_ref.dtype)


def _merge(xn, o, gm, w_mg, mg_row0, wa, wg, layer):
    n, ss = xn
    M, K = n.shape
    Ka = o.shape[1]
    Kg = gm.shape[1]
    N = wa.shape[2]
    tm = _pick_tile(M, ROW_TILE // 2, 16)
    tn = _pick_tile(N, COL_TILE // 2, LANES)
    nj = N // tn
    blk = 2 * (_nbytes((tm, K + Ka + Kg), CD) + _nbytes((2 * K, tn), CD) + _nbytes((Ka + Kg, tn), wa.dtype)
               + _nbytes((tm, tn), CD)) + _nbytes((Ka + Kg, tn), CD) + 8 * _nbytes((tm, tn), F32)
    return pl.pallas_call(
        _merge_kernel,
        grid=(M // tm, nj),
        in_specs=[pl.BlockSpec((tm, K), lambda i, j: (i, 0)), _ss_spec(ss, tm),
                  pl.BlockSpec((tm, Ka), lambda i, j: (i, 0)),
                  pl.BlockSpec((tm, Kg), lambda i, j: (i, 0)),
                  pl.BlockSpec((None, tn, K), lambda i, j: (layer, mg_row0 // tn + j, 0)),
                  pl.BlockSpec((None, tn, K), lambda i, j: (layer, mg_row0 // tn + j + nj, 0)),
                  pl.BlockSpec((None, Ka, tn), lambda i, j: (layer, 0, j)),
                  pl.BlockSpec((None, Kg, tn), lambda i, j: (layer, 0, j))],
        out_specs=pl.BlockSpec((tm, tn), lambda i, j: (i, j)),
        out_shape=jax.ShapeDtypeStruct((M, N), CD),
        compiler_params=_params(("parallel", "parallel"), blk),
        name="mixer_merge",
    )(n, ss, o, gm, w_mg, w_mg, wa, wg)


def _kv_kernel(*refs, has_prev, has_std, last_only):
    x_ref, ss_ref, w_ref = refs[0], refs[1], refs[2]
    il_ref = refs[3 + has_prev]
    tm = x_ref.shape[0]
    z = _inv_rms(ss_ref) * _dot_nt(x_ref[...], w_ref[...])
    if has_std:
        std_ref = refs[4 + has_prev]
        std_ref[...] = z.astype(std_ref.dtype)

    def write():
        for c in range(KVC):
            il_ref[pl.ds(c, tm, stride=KVC), :] = z[:, c * HEAD_DIM:(c + 1) * HEAD_DIM]

    if last_only:
        tiles_per_batch = SEQ // tm
        pl.when(pl.program_id(0) % tiles_per_batch == tiles_per_batch - 1)(write)
    else:
        write()


def _kv_proj_prompt(xn, w, row0, layer, prev, std_dtype, last_only, name):
    x, ss = xn
    K = x.shape[1]
    W = 2 * KV_W
    MP = BATCH * SEQ
    tm = KV_ROW_TILE
    assert SEQ % tm == 0 and (not last_only or tm == min(WINDOW, SEQ))
    if last_only:
        il_shape = (DEPTH, BATCH, tm * KVC, HEAD_DIM)
        il_spec = pl.BlockSpec((None, None, tm * KVC, HEAD_DIM), lambda i: (layer, i // (SEQ // tm), 0, 0))
    else:
        il_shape = (DEPTH, MP * KVC, HEAD_DIM)
        il_spec = pl.BlockSpec((None, tm * KVC, HEAD_DIM), lambda i: (layer, i, 0))
    in_specs = [pl.BlockSpec((tm, K), lambda i: (i, 0)), _ss_spec(ss, tm),
                pl.BlockSpec((None, W, K), lambda i: (layer, row0 // W, 0))]
    args = [x, ss, w]
    aliases = {}
    if prev is not None:
        in_specs.append(pl.BlockSpec(memory_space=pl.ANY))
        args.append(prev)
        aliases = {3: 0}
    blk = 2 * (_nbytes((tm, K), CD) + _nbytes((K, W), CD) + 2 * _nbytes((tm, W), F32)) + 3 * _nbytes((tm, W), F32)
    return pl.pallas_call(
        functools.partial(_kv_kernel, has_prev=prev is not None, has_std=True, last_only=last_only),
        grid=(MP // tm,),
        in_specs=in_specs,
        out_specs=[il_spec, pl.BlockSpec((tm, W), lambda i: (i, 0))],
        out_shape=[jax.ShapeDtypeStruct(il_shape, F32), jax.ShapeDtypeStruct((MP, W), std_dtype)],
        input_output_aliases=aliases,
        compiler_params=_params(("arbitrary",), blk),
        name=name,
    )(*args)


def _kv_proj_sample(xn, w, row0, layer, name):
    x, ss = xn
    K = x.shape[1]
    W = 2 * KV_W
    MP = BATCH * SEQ
    MS = DEC_BATCH * DEC_SEQ
    assert MP % MS == 0
    blk = 2 * (_nbytes((MS, K), CD) + _nbytes((K, W), CD) + _nbytes((MS, W), F32)) + 3 * _nbytes((MS, W), F32)
    return pl.pallas_call(
        functools.partial(_kv_kernel, has_prev=False, has_std=False, last_only=False),
        grid=(1,),
        in_specs=[pl.BlockSpec((MS, K), lambda i: (MP // MS, 0)), _ss_spec(ss, MS, MP // MS),
                  pl.BlockSpec((None, W, K), lambda i: (layer, row0 // W, 0))],
        out_specs=pl.BlockSpec((MS * KVC, HEAD_DIM), lambda i: (0, 0)),
        out_shape=jax.ShapeDtypeStruct((MS * KVC, HEAD_DIM), F32),
        compiler_params=_params(("arbitrary",), blk),
        name=name,
    )(x, ss, w)


def _gmlp_kernel(x_ref, ss_ref, wu_ref, wv_ref, ws_ref, b_ref, *refs, chunk, has_prev, want_v):
    gm_ref = refs[has_prev]
    R = ws_ref.shape[1]
    tm = x_ref.shape[0]
    x = x_ref[...]
    inv = _inv_rms(ss_ref)
    u = jax.nn.gelu(inv * _dot_nt(x, wu_ref[...]))
    v = jax.nn.gelu(inv * _dot_nt(x, wv_ref[...]))
    if want_v:
        refs[has_prev + 1][...] = v
    i = lax.broadcasted_iota(jnp.int32, (R, R), 0)
    j = lax.broadcasted_iota(jnp.int32, (R, R), 1)
    w = jnp.where((j <= i) & (i // chunk == j // chunk), ws_ref[0], 0.0).astype(CD)
    vc = v.astype(CD)
    for c in range(tm // R):
        rows = slice(c * R, (c + 1) * R)
        mixed = _dot(w, vc[rows]) + b_ref[0]
        gm_ref[rows, :] = (u[rows] * mixed).astype(gm_ref.dtype)


def _gmlp(xn, w_uv, uv_row0, layer, ws, bs, n_rows, row_off, tm, R, chunk, prev, want_v):
    x, ss = xn
    M, K = x.shape
    cw = GM_WIDTH // GM_GROUPS
    assert n_rows % tm == 0 and row_off % tm == 0 and tm % R == 0 and cw % LANES == 0
    off = row_off // tm
    blk = 2 * (_nbytes((tm, K), CD) + 2 * _nbytes((K, cw), CD) + 3 * _nbytes((tm, cw), F32)) \
        + 8 * _nbytes((tm, cw), F32)
    assert uv_row0 % cw == 0
    u_spec = pl.BlockSpec((None, cw, K), lambda i, g: (layer, uv_row0 // cw + g, 0))
    v_spec = pl.BlockSpec((None, cw, K), lambda i, g: (layer, uv_row0 // cw + GM_GROUPS + g, 0))
    in_specs = [pl.BlockSpec((tm, K), lambda i, g: (i + off, 0)), _ss_spec(ss, tm, off), u_spec, v_spec,
                pl.BlockSpec((1, R, R), lambda i, g: (g, 0, 0)),
                pl.BlockSpec((1, R, 1), lambda i, g: (g, 0, 0))]
    args = [x, ss, w_uv, w_uv, ws, bs]
    aliases = {}
    if prev is not None:
        in_specs.append(pl.BlockSpec(memory_space=pl.ANY))
        args.append(prev)
        aliases = {6: 0}
    out_specs = [pl.BlockSpec((tm, cw), lambda i, g: (i + off, g))]
    out_shape = [jax.ShapeDtypeStruct((M, GM_WIDTH), CD)]
    if want_v:
        out_specs.append(pl.BlockSpec((tm, cw), lambda i, g: (i, g)))
        out_shape.append(jax.ShapeDtypeStruct((n_rows, GM_WIDTH), F32))
    return pl.pallas_call(
        functools.partial(_gmlp_kernel, chunk=chunk, has_prev=prev is not None, want_v=want_v),
        grid=(n_rows // tm, GM_GROUPS),
        in_specs=in_specs,
        out_specs=out_specs,
        out_shape=out_shape,
        input_output_aliases=aliases,
        compiler_params=_params(("parallel", "parallel"), blk),
        name="gmlp_branch",
    )(*args)


def _compress_rows(load_row, pe_l, w1_lo, w1_hi, w2, xlo, xhi, n_half):
    half = CMP_LEN // 2
    for l in range(half):
        xl = load_row(l)
        cols = slice(l * HEAD_DIM, (l + 1) * HEAD_DIM)
        xlo[:, cols] = (xl + pe_l(l)).astype(CD)
        xhi[:, cols] = (xl + pe_l(half + l)).astype(CD)
    hid = jax.nn.gelu(_dot(xlo[...], w1_lo) + pltpu.roll(_dot(xhi[...], w1_hi), n_half - 1, 0))
    return _dot(hid.astype(CD), w2)


def _compress_prompt_kernel(x_ref, pe_ref, w1_ref, w2_ref, o_ref, xlo, xhi, *, n_half):
    kh = (CMP_LEN // 2) * HEAD_DIM
    tok = _compress_rows(lambda l: x_ref[pl.ds(l, n_half, stride=CMP_STRIDE), :],
                         lambda l: pe_ref[0, l:l + 1, :], w1_ref[0, 0:kh, :], w1_ref[0, kh:2 * kh, :], w2_ref[0],
                         xlo, xhi, n_half)
    o_ref[0, 0] = tok.astype(o_ref.dtype)


def _compress_prompt(kv_c, pe, w1, w2):
    assert CMP_LEN == 2 * CMP_STRIDE and SEQ % (CMP_STRIDE * SUBLANES) == 0
    n_half = SEQ // CMP_STRIDE
    kh = (CMP_LEN // 2) * HEAD_DIM
    blk = 2 * _nbytes((SEQ, HEAD_DIM), F32) + 8 * _nbytes((n_half, HEAD_DIM), F32) \
        + 4 * _nbytes((2 * kh, HEAD_DIM), CD) + 2 * _nbytes((n_half, kh), CD)
    return pl.pallas_call(
        functools.partial(_compress_prompt_kernel, n_half=n_half),
        grid=(BATCH, KVC),
        in_specs=[pl.BlockSpec((SEQ, HEAD_DIM), lambda b, c: (b, c)),
                  pl.BlockSpec((1, CMP_LEN, HEAD_DIM), lambda b, c: (c // KV_HEADS, 0, 0)),
                  pl.BlockSpec((1, 2 * kh, HEAD_DIM), lambda b, c: (c // KV_HEADS, 0, 0)),
                  pl.BlockSpec((1, HEAD_DIM, HEAD_DIM), lambda b, c: (c // KV_HEADS, 0, 0))],
        out_specs=pl.BlockSpec((1, 1, n_half, HEAD_DIM), lambda b, c: (b, c, 0, 0)),
        out_shape=jax.ShapeDtypeStruct((BATCH, KVC, n_half, HEAD_DIM), CD),
        scratch_shapes=[pltpu.VMEM((n_half, kh), CD), pltpu.VMEM((n_half, kh), CD)],
        compiler_params=_params(("parallel", "parallel"), blk),
        name="nsa_compress_prompt",
    )(kv_c, pe, w1, w2)


def _compress_sample_kernel(pt_ref, *refs, n_half):
    page_refs = refs[:CMP_PAGES]
    pe_ref, w1_ref, w2_ref, o_ref, buf, xlo, xhi = refs[CMP_PAGES:]
    step = pl.program_id(1)
    page_rows = PAGE_SIZE * KVC
    kh = (CMP_LEN // 2) * HEAD_DIM
    for k, pr in enumerate(page_refs):
        start = pl.multiple_of((step * CMP_PAGES + k) * PAGE_SIZE, PAGE_SIZE)
        for c in range(KVC):
            buf[c, pl.ds(start, PAGE_SIZE), :] = _head_rows(pr, c, PAGE_SIZE)

    @pl.when(step == pl.num_programs(1) - 1)
    def _():
        def one_slot(c, carry):
            kv = c // KV_HEADS
            tok = _compress_rows(lambda l: buf[c, pl.ds(l, n_half, stride=CMP_STRIDE), :],
                                 lambda l: pe_ref[kv, pl.ds(l, 1), :], w1_ref[kv, pl.ds(0, kh), :],
                                 w1_ref[kv, pl.ds(kh, kh), :], w2_ref[kv], xlo, xhi, n_half)
            o_ref[c] = tok.astype(o_ref.dtype)
            return carry
        lax.fori_loop(0, KVC, one_slot, 0)


def _compress_sample(pool, layer, page_table, pe, w1, w2):
    nb, n_pages = page_table.shape
    assert n_pages % CMP_PAGES == 0 and PAST_LEN % (CMP_STRIDE * SUBLANES) == 0
    n_half = PAST_LEN // CMP_STRIDE
    page_rows = PAGE_SIZE * KVC
    kh = (CMP_LEN // 2) * HEAD_DIM
    blk = _nbytes((PAST_LEN * KVC, HEAD_DIM), F32) + 2 * CMP_PAGES * _nbytes((page_rows, HEAD_DIM), F32) \
        + 4 * _nbytes((2, 2 * kh, HEAD_DIM), CD) + 2 * _nbytes((KVC, n_half, HEAD_DIM), CD) \
        + 2 * _nbytes((n_half, kh), CD) + 10 * _nbytes((n_half, HEAD_DIM), F32)

    def page_spec(k):
        return pl.BlockSpec((None, None, page_rows, HEAD_DIM),
                            lambda b, s, pt: (layer, pt[b, s * CMP_PAGES + k], 0, 0))

    return pl.pallas_call(
        functools.partial(_compress_sample_kernel, n_half=n_half),
        grid_spec=pltpu.PrefetchScalarGridSpec(
            num_scalar_prefetch=1,
            grid=(nb, n_pages // CMP_PAGES),
            in_specs=[page_spec(k) for k in range(CMP_PAGES)]
            + [pl.BlockSpec((2, CMP_LEN, HEAD_DIM), lambda b, s, pt: (0, 0, 0)),
               pl.BlockSpec((2, 2 * kh, HEAD_DIM), lambda b, s, pt: (0, 0, 0)),
               pl.BlockSpec((2, HEAD_DIM, HEAD_DIM), lambda b, s, pt: (0, 0, 0))],
            out_specs=pl.BlockSpec((None, KVC, n_half, HEAD_DIM), lambda b, s, pt: (b, 0, 0, 0)),
            scratch_shapes=[pltpu.VMEM((KVC, PAST_LEN, HEAD_DIM), F32), pltpu.VMEM((n_half, kh), CD),
                            pltpu.VMEM((n_half, kh), CD)]),
        out_shape=jax.ShapeDtypeStruct((nb, KVC, n_half, HEAD_DIM), CD),
        compiler_params=_params(("parallel", "arbitrary"), blk),
        name="nsa_compress_sample",
    )(page_table, *([pool] * CMP_PAGES), pe, w1, w2)


def _masked_softmax(s, mask):
    s = jnp.where(mask, s, NEG_INF)
    m = jnp.max(s, axis=-1, keepdims=True)
    e = jnp.where(mask, jnp.exp(s - m), 0.0)
    d = jnp.sum(e, axis=-1, keepdims=True)
    return e * (1.0 / jnp.where(d > 0.0, d, 1.0))


def _attend_bias(qs, k, v, bias, tq):
    os = []
    for r in range(GROUP):
        t = _dot_nt(qs[r * tq:(r + 1) * tq], k) + bias
        e = jnp.exp2((t - jnp.max(t, axis=-1, keepdims=True)) * (SCALE * LOG2E))
        d = jnp.sum(e, axis=-1, keepdims=True)
        os.append(_dot(e.astype(CD), v) * (1.0 / d))
    return jnp.concatenate(os, axis=0)


def _group_sum_lanes(p, n_out):
    n = p.shape[1]
    src = lax.broadcasted_iota(jnp.int32, (n, n_out), 0) // PER_SEL
    dst = lax.broadcasted_iota(jnp.int32, (n, n_out), 1)
    gmat = (src == dst).astype(CD)
    h1 = p.astype(CD)
    r1 = p - h1.astype(F32)
    h2 = r1.astype(CD)
    h3 = (r1 - h2.astype(F32)).astype(CD)
    return _dot(h1, gmat) + _dot(h2, gmat) + _dot(h3, gmat)


def _stack_heads(q, col0):
    return jnp.concatenate([q[:, col0 + r * HEAD_DIM: col0 + (r + 1) * HEAD_DIM] for r in range(GROUP)], axis=0)


def _gate_sum(gates, col0, o_c, o_s, o_w, r, rows):
    return (gates[:, col0 + r:col0 + r + 1] * o_c[rows]
            + gates[:, col0 + GROUP + r:col0 + GROUP + r + 1] * o_s[rows]
            + gates[:, col0 + 2 * GROUP + r:col0 + 2 * GROUP + r + 1] * o_w[rows])


def _attn_prompt_kernel(q_ref, ks_ref, vs_ref, kw_ref, vw_ref, kc_ref, vc_ref, g_ref, o_ref, *, tq, T):
    i = pl.program_id(2)
    n_sel = T // SEL_LEN
    n_cmp = T // CMP_STRIDE

    qs = _stack_heads(q_ref[...], 0)
    qpos1 = i * tq + lax.broadcasted_iota(jnp.int32, (tq, 1), 0)
    qpos = jnp.concatenate([qpos1] * GROUP, axis=0)

    span = WINDOW + tq
    start = pl.multiple_of(jnp.maximum(i * tq - WINDOW, 0), tq)
    kposw = start + lax.broadcasted_iota(jnp.int32, (1, span), 1)
    bias_w = jnp.where((kposw <= qpos1) & (kposw > qpos1 - WINDOW), 0.0, NEG_INF)
    o_w = _attend_bias(qs, kw_ref[pl.ds(start, span), :], vw_ref[pl.ds(start, span), :], bias_w, tq)

    s = _dot_nt(qs, kc_ref[0, 0]) * SCALE
    blk_end = lax.broadcasted_iota(jnp.int32, (1, n_cmp), 1) * CMP_STRIDE + (CMP_LEN - 1)
    p = _masked_softmax(s, (blk_end <= qpos) & (blk_end < T))
    o_c = _dot(p.astype(CD), vc_ref[0, 0])
    psum = p[0:tq]
    for r in range(1, GROUP):
        psum = psum + p[r * tq:(r + 1) * tq]
    imp = _group_sum_lanes(psum, LANES)

    j = lax.broadcasted_iota(jnp.int32, (1, LANES), 1)
    cur = qpos1 // SEL_LEN
    forced = (j == 0) | (j == cur) | (j == cur - 1)
    score = jnp.where(forced, FORCE, imp)
    score = jnp.where(j <= cur, score, -1.0)
    sc = score.T[:n_sel]
    jrow = lax.broadcasted_iota(jnp.int32, (n_sel, 1), 0)
    rank = jnp.zeros((n_sel, tq), jnp.int32)
    for a in range(n_sel):
        sa = sc[a:a + 1, :]
        rank = rank + ((sa > sc) | ((sa == sc) & (a < jrow))).astype(jnp.int32)
    sel_t = ((rank < SEL_TOP) & (sc >= 0.0)).astype(F32)
    if n_sel < LANES:
        sel_t = jnp.concatenate([sel_t, jnp.zeros((LANES - n_sel, tq), F32)], axis=0)
    sel = sel_t.T.astype(CD)

    def sel_branch(nk):
        def run():
            expand = (lax.broadcasted_iota(jnp.int32, (LANES, nk), 1) // SEL_LEN
                      == lax.broadcasted_iota(jnp.int32, (LANES, nk), 0)).astype(CD)
            key_sel = _dot(sel, expand)
            kpos = lax.broadcasted_iota(jnp.int32, (1, nk), 1)
            bias = jnp.where((key_sel > 0.5) & (kpos <= qpos1), 0.0, NEG_INF)
            return _attend_bias(qs, ks_ref[0:nk, :], vs_ref[0:nk, :], bias, tq)
        return run

    n_ext = T // SEL_KEY_STEP
    o_s = lax.switch((i * tq) // SEL_KEY_STEP, [sel_branch(SEL_KEY_STEP * (k + 1)) for k in range(n_ext)])

    g = g_ref[...]
    for r in range(GROUP):
        o = _gate_sum(g, 0, o_c, o_s, o_w, r, slice(r * tq, (r + 1) * tq))
        o_ref[:, r * HEAD_DIM:(r + 1) * HEAD_DIM] = o.astype(o_ref.dtype)


def _attn_prompt(q, kv_sel, kv_win, kc, gates):
    T = SEQ
    tq = Q_TILE
    assert T % tq == 0 and WINDOW % tq == 0 and WINDOW + tq <= T and T // SEL_LEN <= LANES
    assert T // CMP_STRIDE == LANES and T % SEL_KEY_STEP == 0 and SEL_KEY_STEP % tq == 0
    nq = T // tq
    gw = GROUP * HEAD_DIM
    blk = 2 * 4 * _nbytes((T, HEAD_DIM), CD) + 8 * _nbytes((GROUP * tq, T), F32) + 4 * _nbytes((tq, gw), F32)
    k_spec = pl.BlockSpec((T, HEAD_DIM), lambda b, g, i: (b, g))
    v_spec = pl.BlockSpec((T, HEAD_DIM), lambda b, g, i: (b, KV_HEADS + g))
    return pl.pallas_call(
        functools.partial(_attn_prompt_kernel, tq=tq, T=T),
        grid=(BATCH, KV_HEADS, nq),
        in_specs=[pl.BlockSpec((tq, gw), lambda b, g, i: (b * nq + i, g)),
                  k_spec, v_spec, k_spec, v_spec,
                  pl.BlockSpec((1, 1, T // CMP_STRIDE, HEAD_DIM), lambda b, g, i: (b, g, 0, 0)),
                  pl.BlockSpec((1, 1, T // CMP_STRIDE, HEAD_DIM), lambda b, g, i: (b, KV_HEADS + g, 0, 0)),
                  pl.BlockSpec((tq, LANES), lambda b, g, i: (b * nq + i, g))],
        out_specs=pl.BlockSpec((tq, gw), lambda b, g, i: (b * nq + i, g)),
        out_shape=jax.ShapeDtypeStruct((q.shape[0], ATTN_W), CD),
        compiler_params=_params(("parallel", "parallel", "parallel"), blk),
        name="nsa_prompt_attention",
    )(q, kv_sel, kv_sel, kv_win, kv_win, kc, kc, gates)


def _sample_cmp_kernel(q_ref, kc_ref, vc_ref, oc_ref, sel_ref, *, n_cmp_valid):
    T = DEC_SEQ
    n_cmp = kc_ref.shape[2]
    qs = _stack_heads(q_ref[...], 0).astype(CD)
    qpos1 = PAST_LEN + lax.broadcasted_iota(jnp.int32, (T, 1), 0)
    qpos = jnp.concatenate([qpos1] * GROUP, axis=0)
    s = _dot_nt(qs, kc_ref[0, 0]) * SCALE
    n = lax.broadcasted_iota(jnp.int32, (1, n_cmp), 1)
    p = _masked_softmax(s, (n * CMP_STRIDE + (CMP_LEN - 1) <= qpos) & (n < n_cmp_valid))
    oc_ref[0, 0] = _dot(p.astype(CD), vc_ref[0, 0])
    psum = p[0:T]
    for r in range(1, GROUP):
        psum = psum + p[r * T:(r + 1) * T]
    imp = _group_sum_lanes(psum, LANES)

    j = lax.broadcasted_iota(jnp.int32, (1, LANES), 1)
    cur = qpos1 // SEL_LEN
    score = jnp.where((j == 0) | (j == cur - 1), FORCE, imp)
    rank = jnp.zeros((T, LANES), jnp.int32)
    for sh in range(1, LANES):
        other = pltpu.roll(score, sh, 1)
        oj = jnp.where(j >= sh, j - sh, j - sh + LANES)
        rank = rank + ((other > score) | ((other == score) & (oj < j))).astype(jnp.int32)
    sel_ref[0, 0] = (rank < SEL_TOP - 1).astype(F32)


def _sample_cmp(q_s, kc_s, n_cmp_valid):
    nb = DEC_BATCH
    gw = GROUP * HEAD_DIM
    n_cmp = kc_s.shape[2]
    blk = 16 * _nbytes((GROUP * DEC_SEQ, n_cmp), F32) + 4 * _nbytes((n_cmp, HEAD_DIM), CD)
    return pl.pallas_call(
        functools.partial(_sample_cmp_kernel, n_cmp_valid=n_cmp_valid),
        grid=(nb, KV_HEADS),
        in_specs=[pl.BlockSpec((DEC_SEQ, gw), lambda b, g: (b, g)),
                  pl.BlockSpec((1, 1, n_cmp, HEAD_DIM), lambda b, g: (b, g, 0, 0)),
                  pl.BlockSpec((1, 1, n_cmp, HEAD_DIM), lambda b, g: (b, KV_HEADS + g, 0, 0))],
        out_specs=[pl.BlockSpec((1, 1, GROUP * DEC_SEQ, HEAD_DIM), lambda b, g: (b, g, 0, 0)),
                   pl.BlockSpec((1, 1, DEC_SEQ, LANES), lambda b, g: (b, g, 0, 0))],
        out_shape=[jax.ShapeDtypeStruct((nb, KV_HEADS, GROUP * DEC_SEQ, HEAD_DIM), F32),
                   jax.ShapeDtypeStruct((nb, KV_HEADS, DEC_SEQ, LANES), F32)],
        compiler_params=_params(("parallel", "parallel"), blk),
        name="nsa_sample_cmp_attention",
    )(q_s, kc_s, kc_s)


def _put_rows_kernel(src_ref, dst_any, o_ref):
    o_ref[...] = src_ref[...].astype(o_ref.dtype)


def _put_rows(dst, src, row_off):
    n, w = src.shape
    assert row_off % n == 0 and n % 16 == 0
    return pl.pallas_call(
        _put_rows_kernel,
        grid=(1,),
        in_specs=[pl.BlockSpec((n, w), lambda i: (0, 0)), pl.BlockSpec(memory_space=pl.ANY)],
        out_specs=pl.BlockSpec((n, w), lambda i: (row_off // n, 0)),
        out_shape=jax.ShapeDtypeStruct(dst.shape, dst.dtype),
        input_output_aliases={1: 0},
        compiler_params=_params(("arbitrary",), 4 * _nbytes((n, w), F32)),
        name="put_sample_rows",
    )(src, dst)


def _pad_rows(x, rows):
    return jnp.concatenate([x, jnp.zeros((rows - x.shape[0], x.shape[1]), x.dtype)], axis=0)


def _head_rows(ref, slot, n):
    return ref[pl.ds(slot, n, stride=KVC), :]


def _sample_sel_kernel(pt_ref, q_ref, sel_ref, *refs):
    page_refs = refs[:SEL_PAGES]
    new_ref, o_ref, m_sc, l_sc, acc_sc = refs[SEL_PAGES:]
    T = DEC_SEQ
    R = GROUP * T
    step = pl.program_id(1)
    nk = SEL_PAGES * PAGE_SIZE

    @pl.when(step == 0)
    def _():
        m_sc[...] = jnp.full(m_sc.shape, NEG_INF, F32)
        l_sc[...] = jnp.zeros(l_sc.shape, F32)
        acc_sc[...] = jnp.zeros(acc_sc.shape, F32)

    def update(parts):
        m_prev, l_prev, acc_prev = m_sc[...], l_sc[...], acc_sc[...]
        ms, ls, accs = [], [], []
        for g, (s, mask, v) in enumerate(parts):
            rows = slice(g * R, (g + 1) * R)
            m_new = jnp.maximum(m_prev[rows], jnp.max(jnp.where(mask, s, NEG_INF), axis=-1, keepdims=True))
            alpha = jnp.exp(m_prev[rows] - m_new)
            e = jnp.where(mask, jnp.exp(s - m_new), 0.0)
            ms.append(m_new)
            ls.append(alpha * l_prev[rows] + jnp.sum(e, axis=-1, keepdims=True))
            accs.append(alpha * acc_prev[rows] + _dot(e.astype(CD), v))
        m_sc[...] = jnp.concatenate(ms, axis=0)
        l_sc[...] = jnp.concatenate(ls, axis=0)
        acc_sc[...] = jnp.concatenate(accs, axis=0)

    q = q_ref[...]
    blk_of_key = step * (nk // SEL_LEN) + lax.broadcasted_iota(jnp.int32, (LANES, nk), 1) // SEL_LEN
    expand = (blk_of_key == lax.broadcasted_iota(jnp.int32, (LANES, nk), 0)).astype(CD)
    key_sel = _dot(sel_ref[...].reshape(KV_HEADS * T, LANES).astype(CD), expand)
    parts = []
    for g in range(KV_HEADS):
        qg = _stack_heads(q, g * GROUP * HEAD_DIM).astype(CD)
        k = jnp.concatenate([_head_rows(pr, g, PAGE_SIZE) for pr in page_refs], axis=0).astype(CD)
        v = jnp.concatenate([_head_rows(pr, KV_HEADS + g, PAGE_SIZE) for pr in page_refs], axis=0).astype(CD)
        mask = jnp.concatenate([key_sel[g * T:(g + 1) * T]] * GROUP, axis=0) > 0.5
        parts.append((_dot_nt(qg, k) * SCALE, mask, v))
    update(parts)

    @pl.when(step == pl.num_programs(1) - 1)
    def _():
        t = jnp.concatenate([lax.broadcasted_iota(jnp.int32, (T, 1), 0)] * GROUP, axis=0)
        i = lax.broadcasted_iota(jnp.int32, (1, LANES), 1)
        mask = (i < T) & (i <= t)
        parts = []
        for g in range(KV_HEADS):
            qg = _stack_heads(q, g * GROUP * HEAD_DIM).astype(CD)
            k = _pad_rows(_head_rows(new_ref, g, T), LANES).astype(CD)
            v = _pad_rows(_head_rows(new_ref, KV_HEADS + g, T), LANES).astype(CD)
            parts.append((_dot_nt(qg, k) * SCALE, mask, v))
        update(parts)
        o_ref[...]= (acc_sc[...] * (1.0 / l_sc[...])).reshape(o_ref.shape)


def _sample_sel(q_s, sel, pool, layer, page_table, new_rows):
    nb, n_pages = page_table.shape
    assert n_pages % SEL_PAGES == 0 and (SEL_PAGES * PAGE_SIZE) % SEL_LEN == 0
    R = GROUP * DEC_SEQ
    page_rows = PAGE_SIZE * KVC
    blk = 2 * SEL_PAGES * _nbytes((page_rows, HEAD_DIM), F32) + 16 * _nbytes((R, SEL_PAGES * PAGE_SIZE), F32) \
        + 8 * _nbytes((SEL_PAGES * PAGE_SIZE, HEAD_DIM), F32)

    def page_spec(k):
        return pl.BlockSpec((None, None, page_rows, HEAD_DIM),
                            lambda b, s, pt: (layer, pt[b, s * SEL_PAGES + k], 0, 0))

    return pl.pallas_call(
        _sample_sel_kernel,
        grid_spec=pltpu.PrefetchScalarGridSpec(
            num_scalar_prefetch=1,
            grid=(nb, n_pages // SEL_PAGES),
            in_specs=[pl.BlockSpec((DEC_SEQ, ATTN_W), lambda b, s, pt: (b, 0)),
                      pl.BlockSpec((None, KV_HEADS, DEC_SEQ, LANES), lambda b, s, pt: (b, 0, 0, 0))]
            + [page_spec(k) for k in range(SEL_PAGES)]
            + [pl.BlockSpec((DEC_SEQ * KVC, HEAD_DIM), lambda b, s, pt: (b, 0))],
            out_specs=pl.BlockSpec((None, KV_HEADS, R, HEAD_DIM), lambda b, s, pt: (b, 0, 0, 0)),
            scratch_shapes=[pltpu.VMEM((KV_HEADS * R, 1), F32), pltpu.VMEM((KV_HEADS * R, 1), F32),
                            pltpu.VMEM((KV_HEADS * R, HEAD_DIM), F32)]),
        out_shape=jax.ShapeDtypeStruct((nb, KV_HEADS, R, HEAD_DIM), F32),
        compiler_params=_params(("parallel", "arbitrary"), blk),
        name="nsa_sample_sel_attention",
    )(page_table, q_s, sel, *([pool] * SEL_PAGES), new_rows)


def _sample_win_kernel(q_ref, buf_ref, new_ref, oc_ref, os_ref, g_ref, o_ref, *, wb):
    T = DEC_SEQ
    q = q_ref[...]
    gates = g_ref[...]
    t = jnp.concatenate([lax.broadcasted_iota(jnp.int32, (T, 1), 0)] * GROUP, axis=0)
    qpos = PAST_LEN + t
    i = lax.broadcasted_iota(jnp.int32, (1, wb + LANES), 1)
    kpos = PAST_LEN - wb + i
    mask = (i < wb + T) & (kpos <= qpos) & (kpos > qpos - WINDOW)
    for g in range(KV_HEADS):
        qg = _stack_heads(q, g * GROUP * HEAD_DIM).astype(CD)
        k = jnp.concatenate([_head_rows(buf_ref, g, wb), _pad_rows(_head_rows(new_ref, g, T), LANES)],
                            axis=0).astype(CD)
        v = jnp.concatenate([_head_rows(buf_ref, KV_HEADS + g, wb),
                             _pad_rows(_head_rows(new_ref, KV_HEADS + g, T), LANES)], axis=0).astype(CD)
        p = _masked_softmax(_dot_nt(qg, k) * SCALE, mask)
        o_w = _dot(p.astype(CD), v)
        o_c = oc_ref[g]
        o_s = os_ref[g]
        for r in range(GROUP):
            o = _gate_sum(gates, g * LANES, o_c, o_s, o_w, r, slice(r * T, (r + 1) * T))
            col = (g * GROUP + r) * HEAD_DIM
            o_ref[:, col:col + HEAD_DIM] = o


def _sample_win(q_s, win_buf, layer, new_rows, o_c, o_s, gates_s):
    nb = DEC_BATCH
    wb = win_buf.shape[2] // KVC
    R = GROUP * DEC_SEQ
    assert wb % LANES == 0
    blk = 2 * _nbytes((wb * KVC, HEAD_DIM), F32) + 8 * _nbytes((R, wb + LANES), F32) \
        + 8 * _nbytes((wb + LANES, HEAD_DIM), F32)
    return pl.pallas_call(
        functools.partial(_sample_win_kernel, wb=wb),
        grid=(nb,),
        in_specs=[pl.BlockSpec((DEC_SEQ, ATTN_W), lambda b: (b, 0)),
                  pl.BlockSpec((None, None, wb * KVC, HEAD_DIM), lambda b: (layer, b, 0, 0)),
                  pl.BlockSpec((DEC_SEQ * KVC, HEAD_DIM), lambda b: (b, 0)),
                  pl.BlockSpec((None, KV_HEADS, R, HEAD_DIM), lambda b: (b, 0, 0, 0)),
                  pl.BlockSpec((None, KV_HEADS, R, HEAD_DIM), lambda b: (b, 0, 0, 0)),
                  pl.BlockSpec((DEC_SEQ, KV_HEADS * LANES), lambda b: (b, 0))],
        out_specs=pl.BlockSpec((DEC_SEQ, ATTN_W), lambda b: (b, 0)),
        out_shape=jax.ShapeDtypeStruct((nb * DEC_SEQ, ATTN_W), F32),
        compiler_params=_params(("parallel",), blk),
        name="nsa_sample_win_attention",
    )(q_s, win_buf, new_rows, o_c, o_s, gates_s)


def _half_ffn(h, hn, w_gate, w_up, w_down, layer, g_next):
    act = _swiglu(hn, w_gate, w_up, layer)
    return _resid_proj(act, w_down, layer, h, 0.5, g_next, ROW_TILE // 2, COL_TILE // 2, "ffn_down")


W_IN_ROW_TILE = 512


def _w_in_cast_kernel(w_ref, o_ref):
    o_ref[...] = w_ref[0].astype(o_ref.dtype)


def _prep_w_in(w_in):
    o1 = ATTN_W
    o4 = o1 + 6 * KV_W
    n_gate = 3 * N_HEADS
    wt = jnp.swapaxes(w_in, 1, 2)
    L, n_in, D = wt.shape
    n_out = n_in - n_gate
    tr = W_IN_ROW_TILE
    assert o4 % tr == 0 and n_out % tr == 0 and n_gate % 16 == 0

    def src_row(j):
        return pl.multiple_of(j * tr + jnp.where(j * tr >= o4, n_gate, 0), 16)

    blk = 2 * (_nbytes((tr, D), F32) + _nbytes((tr, D), CD)) + 2 * _nbytes((tr, D), F32)
    main = pl.pallas_call(
        _w_in_cast_kernel,
        grid=(L, n_out // tr),
        in_specs=[pl.BlockSpec((pl.Element(1), pl.Element(tr), pl.Element(D)), lambda l, j: (l, src_row(j), 0))],
        out_specs=pl.BlockSpec((None, tr, D), lambda l, j: (l, j, 0)),
        out_shape=jax.ShapeDtypeStruct((L, n_out, D), CD),
        compiler_params=_params(("parallel", "parallel"), blk),
        name="w_in_cast",
    )(wt)
    wg = wt[:, o4:o4 + n_gate].reshape(L, 3, KV_HEADS, GROUP, D).transpose(0, 2, 1, 3, 4)
    wg = jnp.pad(wg.reshape(L, KV_HEADS, 3 * GROUP, D), ((0, 0), (0, 0), (0, LANES - 3 * GROUP), (0, 0)))
    return dict(main=main, gate=wg.reshape(L, KV_HEADS * LANES, D).astype(CD))


def _layer(h, hn, layer, W, small, caches, page_table, kv_bufs, g_after):
    (nm, pe, w1, w2, ws, bs, n2) = small
    cache_cmp, cache_sel, cache_win = caches
    MP = BATCH * SEQ
    MS = DEC_BATCH * DEC_SEQ
    n_pages = page_table.shape[1]
    assert n_pages * PAGE_SIZE == PAST_LEN and PAST_LEN // SEL_LEN == LANES and DEC_SEQ <= SEL_LEN
    assert DEC_SEQ == SUBLANES and MP % MS == 0

    h, n = _half_ffn(h, hn, W["g1"], W["u1"], W["d1"], layer, nm)

    wm = W["main"]
    kv0 = ATTN_W
    uv0 = kv0 + 6 * KV_W
    mg0 = uv0 + 2 * GM_WIDTH
    q = _proj(n, wm, 0, ATTN_W, layer, None, CD, "proj_q")
    gates = _proj(n, W["gate"], 0, KV_HEADS * LANES, layer, "sigmoid", F32, "proj_branch_gates")
    cmp_buf, kv_c = _kv_proj_prompt(n, wm, kv0, layer, kv_bufs[0], F32, False, "proj_kv_cmp")
    sel_buf, kv_sel = _kv_proj_prompt(n, wm, kv0 + 2 * KV_W, layer, kv_bufs[1], CD, False, "proj_kv_sel")
    win_buf, kv_win = _kv_proj_prompt(n, wm, kv0 + 4 * KV_W, layer, kv_bufs[2], CD, True, "proj_kv_win")
    new_s = [_kv_proj_sample(n, wm, kv0 + s * 2 * KV_W, layer, "proj_kv_sample") for s in range(3)]

    w1c = w1.astype(CD).reshape(2, CMP_LEN * HEAD_DIM, HEAD_DIM)
    w2c = w2.astype(CD)

    kc_p = _compress_prompt(kv_c, pe, w1c, w2c)
    o_p = _attn_prompt(q, kv_sel, kv_win, kc_p, gates)
    (gm,) = _gmlp(n, wm, uv0, layer, ws, bs[:, :, None], MP, 0, 1024, GM_CHUNK, GM_CHUNK, None, False)

    q_s = q[MP:].astype(F32)
    gates_s = gates[MP:]
    kc_s = _compress_sample(cache_cmp, layer, page_table, pe, w1c, w2c)
    n_cmp_valid = (PAST_LEN + DEC_SEQ - CMP_LEN) // CMP_STRIDE + 1
    oc_s, sel_s = _sample_cmp(q_s, kc_s, n_cmp_valid)
    os_s = _sample_sel(q_s, sel_s, cache_sel, layer, page_table, new_s[1])
    o_s = _sample_win(q_s, cache_win, layer, new_s[2], oc_s, os_s, gates_s)
    cl = min(DEC_SEQ, GM_CHUNK)
    ws_s = jnp.tile(ws[:, :cl, :cl], (1, MS // cl, MS // cl))
    bs_s = jnp.tile(bs[:, :cl], (1, MS // cl))[:, :, None]
    gm, v_s = _gmlp(n, wm, uv0, layer, ws_s, bs_s, MS, MP, MS, MS, cl, gm, True)

    o = _put_rows(o_p, o_s, MP)
    merged = _merge(n, o, gm, wm, mg0, W["wua"], W["wug"], layer)
    h, n = _resid_proj(merged, W["wo"], layer, h, 1.0, n2, ROW_TILE, COL_TILE // 2, "mixer_out")
    h, hn = _half_ffn(h, n, W["g2"], W["u2"], W["d2"], layer, g_after)
    return h, hn, (cmp_buf, sel_buf, win_buf), (new_s, v_s)


def kernel(x_prompt, x_sample, cache_cmp_kv, cache_sel_kv, cache_win_kv, page_table, norm_ffn1, ffn1_gate, ffn1_up, ffn1_down, norm_mix, w_in, cmp_pe, cmp_w1, cmp_w2, gm_ws, gm_bs, w_up_attn, w_up_gm, w_out, norm_ffn2, ffn2_gate, ffn2_up, ffn2_down, norm_final):
    MP = BATCH * SEQ
    MS = DEC_BATCH * DEC_SEQ
    kv_row = (2, KV_HEADS, HEAD_DIM)
    h = jnp.concatenate([x_prompt.reshape(MP, D_MODEL), x_sample.reshape(MS, D_MODEL)], axis=0)
    n_phys = cache_cmp_kv.shape[1]
    wb = cache_win_kv.shape[2]
    caches = (cache_cmp_kv.reshape(DEPTH, n_phys, PAGE_SIZE * KVC, HEAD_DIM),
              cache_sel_kv.reshape(DEPTH, n_phys, PAGE_SIZE * KVC, HEAD_DIM),
              cache_win_kv.reshape(DEPTH, DEC_BATCH, wb * KVC, HEAD_DIM))
    W = _prep_w_in(w_in)
    W.update(g1=ffn1_gate, u1=ffn1_up, d1=ffn1_down.astype(CD),
             g2=ffn2_gate, u2=ffn2_up, d2=ffn2_down.astype(CD),
             wua=w_up_attn, wug=w_up_gm, wo=w_out)
    kv_bufs = (None, None, None)
    sample_outs = []
    hn = _norm_prep(h, norm_ffn1[0])
    for l in range(DEPTH):
        small = (norm_mix[l], cmp_pe[l], cmp_w1[l], cmp_w2[l], gm_ws[l], gm_bs[l], norm_ffn2[l])
        g_after = norm_ffn1[l + 1] if l + 1 < DEPTH else None
        h, hn, kv_bufs, s_out = _layer(h, hn, l, W, small, caches, page_table, kv_bufs, g_after)
        sample_outs.append(s_out)
    y_prompt = _rmsnorm(h, norm_final, F32, rows=MP, row_off=0).reshape(BATCH, SEQ, D_MODEL)
    y_sample = _rmsnorm(h, norm_final, F32, rows=MS, row_off=MP, tile=MS).reshape(DEC_BATCH, DEC_SEQ, D_MODEL)

    cmp_buf, sel_buf, win_buf = kv_bufs
    wk = min(WINDOW, SEQ)
    new = [jnp.stack([sample_outs[l][0][s] for l in range(DEPTH)]).reshape((DEPTH, DEC_BATCH, DEC_SEQ) + kv_row)
           for s in range(3)]
    win_all = jnp.concatenate([cache_win_kv, new[2]], axis=2)
    win_s = win_all[:, :, -min(WINDOW, win_all.shape[2]):]
    gm_v = jnp.stack([sample_outs[l][1] for l in range(DEPTH)]).reshape(DEPTH, DEC_BATCH, DEC_SEQ, GM_WIDTH)
    return (y_prompt, y_sample,
            cmp_buf.reshape((DEPTH, BATCH, SEQ) + kv_row), new[0],
            sel_buf.reshape((DEPTH, BATCH, SEQ) + kv_row), new[1],
            win_buf.reshape((DEPTH, BATCH, wk) + kv_row), win_s, gm_v)
```

```python
import functools

import jax
import jax.numpy as jnp
from jax import lax
from jax.experimental import pallas as pl
from jax.experimental.pallas import tpu as pltpu

D_MODEL = 4096
BATCH = 4
SEQ = 2048
DEPTH = 4
DEC_BATCH = 8
DEC_SEQ = 8
PAST_LEN = 8192
PAGE_SIZE = 128

N_HEADS = 16
KV_HEADS = 4
HEAD_DIM = 128
GROUP = N_HEADS // KV_HEADS
ATTN_W = N_HEADS * HEAD_DIM
KV_W = KV_HEADS * HEAD_DIM
CMP_LEN = 32
CMP_STRIDE = 16
SEL_LEN = 64
SEL_TOP = 16
WINDOW = 512
GM_WIDTH = D_MODEL // 2
GM_GROUPS = 4
GM_CHUNK = 128
D_FF = 11008
EPS = 1e-6
NEG_INF = -1e30
FORCE = 1e9
SCALE = HEAD_DIM ** -0.5
LOG2E = 1.4426950408889634

LANES = 128
SUBLANES = 8
VMEM_BUDGET = 56 * 1024 * 1024
CD = jnp.bfloat16
F32 = jnp.float32
PER_SEL = SEL_LEN // CMP_STRIDE
KVC = 2 * KV_HEADS

ROW_TILE = 1376
COL_TILE = 512
ROW_SPLIT = 2
Q_TILE = 512
SEL_KEY_STEP = 512
KV_ROW_TILE = 512
SEL_PAGES = 8
CMP_PAGES = 8


def _pick_tile(n, target, mult):
    best = n
    for t in range(mult, min(n, target) + 1, mult):
        if n % t == 0:
            best = t
    return best


def _params(sem, block_bytes):
    limit = min(VMEM_BUDGET, max(16 * 1024 * 1024, int(block_bytes)))
    return pltpu.CompilerParams(dimension_semantics=sem, vmem_limit_bytes=limit)


def _nbytes(shape, dtype):
    n = 1
    for s in shape:
        n *= s
    return n * jnp.dtype(dtype).itemsize


def _dot(a, b):
    return jnp.dot(a, b, preferred_element_type=F32)


def _dot_nt(a, b):
    return lax.dot_general(a, b, (((1,), (1,)), ((), ())), preferred_element_type=F32)


def _rms_kernel(x_ref, g_ref, o_ref):
    x = x_ref[...]
    y = x * lax.rsqrt(jnp.mean(x * x, axis=-1, keepdims=True) + EPS)
    o_ref[...] = (y * g_ref[...]).astype(o_ref.dtype)


def _rmsnorm(x, g, out_dtype, rows=None, row_off=0, tile=None):
    M, D = x.shape
    rows = M if rows is None else rows
    tm = tile or _pick_tile(rows, 344, 16)
    assert rows % tm == 0 and row_off % tm == 0
    off = row_off // tm
    blk = 2 * _nbytes((tm, D), F32) * 2 + 2 * _nbytes((tm, D), out_dtype)
    return pl.pallas_call(
        _rms_kernel,
        grid=(rows // tm,),
        in_specs=[pl.BlockSpec((tm, D), lambda i: (i + off, 0)),
                  pl.BlockSpec((1, D), lambda i: (0, 0))],
        out_specs=pl.BlockSpec((tm, D), lambda i: (i, 0)),
        out_shape=jax.ShapeDtypeStruct((rows, D), out_dtype),
        compiler_params=_params(("parallel",), blk),
        name="rmsnorm",
    )(x, g.reshape(1, D))


def _inv_rms(ss_ref):
    return lax.rsqrt(ss_ref[...] * (1.0 / D_MODEL) + EPS)


def _ss_spec(ss, tm, off=0):
    return pl.BlockSpec((tm, 1), lambda i, *_: (i + off, 0))


def _norm_prep_kernel(x_ref, g_ref, hb_ref, ss_ref):
    x = x_ref[...]
    hb_ref[...] = (x * g_ref[...]).astype(hb_ref.dtype)
    ss_ref[...] = jnp.sum(x * x, axis=-1, keepdims=True)


def _norm_prep(x, g):
    M, D = x.shape
    tm = _pick_tile(M, 344, 16)
    blk = 4 * _nbytes((tm, D), F32) + 2 * _nbytes((tm, D), CD)
    return pl.pallas_call(
        _norm_prep_kernel,
        grid=(M // tm,),
        in_specs=[pl.BlockSpec((tm, D), lambda i: (i, 0)),
                  pl.BlockSpec((1, D), lambda i: (0, 0))],
        out_specs=[pl.BlockSpec((tm, D), lambda i: (i, 0)), pl.BlockSpec((tm, 1), lambda i: (i, 0))],
        out_shape=[jax.ShapeDtypeStruct((M, D), CD), jax.ShapeDtypeStruct((M, 1), F32)],
        compiler_params=_params(("parallel",), blk),
        name="norm_prep",
    )(x, g.reshape(1, D))


def _proj_kernel(x_ref, ss_ref, w_ref, o_ref, *, act):
    z = _inv_rms(ss_ref) * _dot_nt(x_ref[...], w_ref[...])
    if act == "sigmoid":
        z = jax.nn.sigmoid(z)
    o_ref[...] = z.astype(o_ref.dtype)


def _proj(xn, wt, row0, N, layer, act, out_dtype, name):
    x, ss = xn
    M, K = x.shape
    tm = _pick_tile(M, ROW_TILE, 16)
    tn = _pick_tile(N, COL_TILE, LANES)
    assert row0 % tn == 0
    blk = 2 * (_nbytes((tm, K), x.dtype) + _nbytes((K, tn), wt.dtype) + _nbytes((tm, tn), out_dtype)) \
        + 4 * _nbytes((tm, tn), F32)
    return pl.pallas_call(
        functools.partial(_proj_kernel, act=act),
        grid=(M // tm, N // tn),
        in_specs=[pl.BlockSpec((tm, K), lambda i, j: (i, 0)), _ss_spec(ss, tm),
                  pl.BlockSpec((None, tn, K), lambda i, j: (layer, row0 // tn + j, 0))],
        out_specs=pl.BlockSpec((tm, tn), lambda i, j: (i, j)),
        out_shape=jax.ShapeDtypeStruct((M, N), out_dtype),
        compiler_params=_params(("parallel", "parallel"), blk),
        name=name,
    )(x, ss, wt)


def _row_parts(tm):
    cut = (tm // 32) * 16
    return [slice(0, tm)] if cut == 0 or ROW_SPLIT < 2 else [slice(0, cut), slice(cut, tm)]


def _swiglu_kernel(x_ref, ss_ref, wg_ref, wu_ref, o_ref):
    wg = wg_ref[...].astype(CD)
    wu = wu_ref[...].astype(CD)
    for rows in _row_parts(x_ref.shape[0]):
        inv = _inv_rms(ss_ref)[rows]
        a = inv * _dot(x_ref[rows, :], wg)
        b = inv * _dot(x_ref[rows, :], wu)
        o_ref[rows, :] = (jax.nn.silu(a) * b).astype(o_ref.dtype)


def _swiglu(xn, wg, wu, layer):
    x, ss = xn
    M, K = x.shape
    N = wg.shape[2]
    tm = _pick_tile(M, ROW_TILE, 16)
    tn = _pick_tile(N, COL_TILE // 2, LANES)
    blk = 2 * (_nbytes((tm, K), x.dtype) + 2 * _nbytes((K, tn), wg.dtype) + _nbytes((tm, tn), CD)) \
        + 2 * _nbytes((K, tn), CD) + 6 * _nbytes((tm, tn), F32)
    w_spec = pl.BlockSpec((None, K, tn), lambda i, j: (layer, 0, j))
    return pl.pallas_call(
        _swiglu_kernel,
        grid=(M // tm, N // tn),
        in_specs=[pl.BlockSpec((tm, K), lambda i, j: (i, 0)), _ss_spec(ss, tm), w_spec, w_spec],
        out_specs=pl.BlockSpec((tm, tn), lambda i, j: (i, j)),
        out_shape=jax.ShapeDtypeStruct((M, N), CD),
        compiler_params=_params(("parallel", "parallel"), blk),
        name="ffn_swiglu",
    )(x, ss, wg, wu)


def _resid_kernel(x_ref, w_ref, h_ref, *refs, scale, emit_norm):
    w = w_ref[...].astype(CD)
    o_ref = refs[1] if emit_norm else refs[0]
    sums = []
    for rows in _row_parts(x_ref.shape[0]):
        hn = h_ref[rows, :] + scale * _dot(x_ref[rows, :], w)
        o_ref[rows, :] = hn
        if emit_norm:
            g_ref, _, hb_ref, ss_ref = refs
            hb_ref[rows, :] = (hn * g_ref[...]).astype(hb_ref.dtype)
            sums.append(jnp.sum(hn * hn, axis=-1, keepdims=True))
    if emit_norm:
        part = jnp.concatenate(sums, axis=0)
        j = pl.program_id(1)

        @pl.when(j == 0)
        def _():
            ss_ref[...] = part

        @pl.when(j > 0)
        def _():
            ss_ref[...] += part


def _resid_proj(x, w, layer, h, scale, g_next, tm_target, tn_target, name):
    M, K = x.shape
    N = w.shape[2]
    tm = _pick_tile(M, tm_target, 16)
    tn = _pick_tile(N, tn_target, LANES)
    nj = N // tn
    emit = g_next is not None
    blk = 2 * (_nbytes((tm, K), x.dtype) + _nbytes((K, tn), w.dtype) + 3 * _nbytes((tm, tn), F32)) \
        + (w.dtype != CD) * _nbytes((K, tn), CD) + 4 * _nbytes((tm, tn), F32)
    tile = pl.BlockSpec((tm, tn), lambda i, j: (i, j))
    in_specs = [pl.BlockSpec((tm, K), lambda i, j: (i, 0)),
                pl.BlockSpec((None, K, tn), lambda i, j: (layer, 0, j)), tile]
    args = [x, w, h]
    out_specs = [tile]
    out_shape = [jax.ShapeDtypeStruct((M, N), F32)]
    if emit:
        in_specs.append(pl.BlockSpec((1, tn), lambda i, j: (0, j)))
        args.append(g_next.reshape(1, N))
        out_specs += [tile, pl.BlockSpec((tm, 1), lambda i, j: (i, 0))]
        out_shape += [jax.ShapeDtypeStruct((M, N), CD), jax.ShapeDtypeStruct((M, 1), F32)]
    outs = pl.pallas_call(
        functools.partial(_resid_kernel, scale=scale, emit_norm=emit),
        grid=(M // tm, nj),
        in_specs=in_specs,
        out_specs=out_specs,
        out_shape=out_shape,
        compiler_params=_params(("parallel", "arbitrary"), blk),
        name=name,
    )(*args)
    return (outs[0], (outs[1], outs[2])) if emit else (outs[0], None)


def _merge_kernel(n_ref, ss_ref, o_ref, gm_ref, wma_ref, wmg_ref, wa_ref, wg_ref, out_ref):
    wma, wmg = wma_ref[...], wmg_ref[...]
    wa, wg = wa_ref[...].astype(CD), wg_ref[...].astype(CD)
    for rows in _row_parts(n_ref.shape[0]):
        n = n_ref[rows, :]
        inv = _inv_rms(ss_ref)[rows]
        ga = jax.nn.sigmoid(inv * _dot_nt(n, wma))
        gg = jax.nn.sigmoid(inv * _dot_nt(n, wmg))
        a = _dot(o_ref[rows, :], wa)
        g = _dot(gm_ref[rows, :], wg)
        out_ref[rows, :] = (ga * a + gg * g).astype(out_ref.dtype)


def _merge(xn, o, gm, w_mg, mg_row0, wa, wg, layer):
    n, ss = xn
    M, K = n.shape
    Ka = o.shape[1]
    Kg = gm.shape[1]
    N = wa.shape[2]
    tm = _pick_tile(M, ROW_TILE // 2, 16)
    tn = _pick_tile(N, COL_TILE // 2, LANES)
    nj = N // tn
    blk = 2 * (_nbytes((tm, K + Ka + Kg), CD) + _nbytes((2 * K, tn), CD) + _nbytes((Ka + Kg, tn), wa.dtype)
               + _nbytes((tm, tn), CD)) + _nbytes((Ka + Kg, tn), CD) + 8 * _nbytes((tm, tn), F32)
    return pl.pallas_call(
        _merge_kernel,
        grid=(M // tm, nj),
        in_specs=[pl.BlockSpec((tm, K), lambda i, j: (i, 0)), _ss_spec(ss, tm),
                  pl.BlockSpec((tm, Ka), lambda i, j: (i, 0)),
                  pl.BlockSpec((tm, Kg), lambda i, j: (i, 0)),
                  pl.BlockSpec((None, tn, K), lambda i, j: (layer, mg_row0 // tn + j, 0)),
                  pl.BlockSpec((None, tn, K), lambda i, j: (layer, mg_row0 // tn + j + nj, 0)),
                  pl.BlockSpec((None, Ka, tn), lambda i, j: (layer, 0, j)),
                  pl.BlockSpec((None, Kg, tn), lambda i, j: (layer, 0, j))],
        out_specs=pl.BlockSpec((tm, tn), lambda i, j: (i, j)),
        out_shape=jax.ShapeDtypeStruct((M, N), CD),
        compiler_params=_params(("parallel", "parallel"), blk),
        name="mixer_merge",
    )(n, ss, o, gm, w_mg, w_mg, wa, wg)


def _kv_kernel(*refs, has_prev, has_std, last_only):
    x_ref, ss_ref, w_ref = refs[0], refs[1], refs[2]
    il_ref = refs[3 + has_prev]
    tm = x_ref.shape[0]
    z = _inv_rms(ss_ref) * _dot_nt(x_ref[...], w_ref[...])
    if has_std:
        std_ref = refs[4 + has_prev]
        std_ref[...] = z.astype(std_ref.dtype)

    def write():
        for c in range(KVC):
            il_ref[pl.ds(c, tm, stride=KVC), :] = z[:, c * HEAD_DIM:(c + 1) * HEAD_DIM]

    if last_only:
        tiles_per_batch = SEQ // tm
        pl.when(pl.program_id(0) % tiles_per_batch == tiles_per_batch - 1)(write)
    else:
        write()


def _kv_proj_prompt(xn, w, row0, layer, prev, std_dtype, last_only, name):
    x, ss = xn
    K = x.shape[1]
    W = 2 * KV_W
    MP = BATCH * SEQ
    tm = KV_ROW_TILE
    assert SEQ % tm == 0 and (not last_only or tm == min(WINDOW, SEQ))
    if last_only:
        il_shape = (DEPTH, BATCH, tm * KVC, HEAD_DIM)
        il_spec = pl.BlockSpec((None, None, tm * KVC, HEAD_DIM), lambda i: (layer, i // (SEQ // tm), 0, 0))
    else:
        il_shape = (DEPTH, MP * KVC, HEAD_DIM)
        il_spec = pl.BlockSpec((None, tm * KVC, HEAD_DIM), lambda i: (layer, i, 0))
    in_specs = [pl.BlockSpec((tm, K), lambda i: (i, 0)), _ss_spec(ss, tm),
                pl.BlockSpec((None, W, K), lambda i: (layer, row0 // W, 0))]
    args = [x, ss, w]
    aliases = {}
    if prev is not None:
        in_specs.append(pl.BlockSpec(memory_space=pl.ANY))
        args.append(prev)
        aliases = {3: 0}
    blk = 2 * (_nbytes((tm, K), CD) + _nbytes((K, W), CD) + 2 * _nbytes((tm, W), F32)) + 3 * _nbytes((tm, W), F32)
    return pl.pallas_call(
        functools.partial(_kv_kernel, has_prev=prev is not None, has_std=True, last_only=last_only),
        grid=(MP // tm,),
        in_specs=in_specs,
        out_specs=[il_spec, pl.BlockSpec((tm, W), lambda i: (i, 0))],
        out_shape=[jax.ShapeDtypeStruct(il_shape, F32), jax.ShapeDtypeStruct((MP, W), std_dtype)],
        input_output_aliases=aliases,
        compiler_params=_params(("arbitrary",), blk),
        name=name,
    )(*args)


def _kv_proj_sample(xn, w, row0, layer, name):
    x, ss = xn
    K = x.shape[1]
    W = 2 * KV_W
    MP = BATCH * SEQ
    MS = DEC_BATCH * DEC_SEQ
    assert MP % MS == 0
    blk = 2 * (_nbytes((MS, K), CD) + _nbytes((K, W), CD) + _nbytes((MS, W), F32)) + 3 * _nbytes((MS, W), F32)
    return pl.pallas_call(
        functools.partial(_kv_kernel, has_prev=False, has_std=False, last_only=False),
        grid=(1,),
        in_specs=[pl.BlockSpec((MS, K), lambda i: (MP // MS, 0)), _ss_spec(ss, MS, MP // MS),
                  pl.BlockSpec((None, W, K), lambda i: (layer, row0 // W, 0))],
        out_specs=pl.BlockSpec((MS * KVC, HEAD_DIM), lambda i: (0, 0)),
        out_shape=jax.ShapeDtypeStruct((MS * KVC, HEAD_DIM), F32),
        compiler_params=_params(("arbitrary",), blk),
        name=name,
    )(x, ss, w)


def _gmlp_kernel(x_ref, ss_ref, wu_ref, wv_ref, ws_ref, b_ref, *refs, chunk, has_prev, want_v):
    gm_ref = refs[has_prev]
    R = ws_ref.shape[1]
    tm = x_ref.shape[0]
    x = x_ref[...]
    inv = _inv_rms(ss_ref)
    u = jax.nn.gelu(inv * _dot_nt(x, wu_ref[...]))
    v = jax.nn.gelu(inv * _dot_nt(x, wv_ref[...]))
    if want_v:
        refs[has_prev + 1][...] = v
    i = lax.broadcasted_iota(jnp.int32, (R, R), 0)
    j = lax.broadcasted_iota(jnp.int32, (R, R), 1)
    w = jnp.where((j <= i) & (i // chunk == j // chunk), ws_ref[0], 0.0).astype(CD)
    vc = v.astype(CD)
    for c in range(tm // R):
        rows = slice(c * R, (c + 1) * R)
        mixed = _dot(w, vc[rows]) + b_ref[0]
        gm_ref[rows, :] = (u[rows] * mixed).astype(gm_ref.dtype)


def _gmlp(xn, w_uv, uv_row0, layer, ws, bs, n_rows, row_off, tm, R, chunk, prev, want_v):
    x, ss = xn
    M, K = x.shape
    cw = GM_WIDTH // GM_GROUPS
    assert n_rows % tm == 0 and row_off % tm == 0 and tm % R == 0 and cw % LANES == 0
    off = row_off // tm
    blk = 2 * (_nbytes((tm, K), CD) + 2 * _nbytes((K, cw), CD) + 3 * _nbytes((tm, cw), F32)) \
        + 8 * _nbytes((tm, cw), F32)
    assert uv_row0 % cw == 0
    u_spec = pl.BlockSpec((None, cw, K), lambda i, g: (layer, uv_row0 // cw + g, 0))
    v_spec = pl.BlockSpec((None, cw, K), lambda i, g: (layer, uv_row0 // cw + GM_GROUPS + g, 0))
    in_specs = [pl.BlockSpec((tm, K), lambda i, g: (i + off, 0)), _ss_spec(ss, tm, off), u_spec, v_spec,
                pl.BlockSpec((1, R, R), lambda i, g: (g, 0, 0)),
                pl.BlockSpec((1, R, 1), lambda i, g: (g, 0, 0))]
    args = [x, ss, w_uv, w_uv, ws, bs]
    aliases = {}
    if prev is not None:
        in_specs.append(pl.BlockSpec(memory_space=pl.ANY))
        args.append(prev)
        aliases = {6: 0}
    out_specs = [pl.BlockSpec((tm, cw), lambda i, g: (i + off, g))]
    out_shape = [jax.ShapeDtypeStruct((M, GM_WIDTH), CD)]
    if want_v:
        out_specs.append(pl.BlockSpec((tm, cw), lambda i, g: (i, g)))
        out_shape.append(jax.ShapeDtypeStruct((n_rows, GM_WIDTH), F32))
    return pl.pallas_call(
        functools.partial(_gmlp_kernel, chunk=chunk, has_prev=prev is not None, want_v=want_v),
        grid=(n_rows // tm, GM_GROUPS),
        in_specs=in_specs,
        out_specs=out_specs,
        out_shape=out_shape,
        input_output_aliases=aliases,
        compiler_params=_params(("parallel", "parallel"), blk),
        name="gmlp_branch",
    )(*args)


def _compress_rows(load_row, pe_l, w1_lo, w1_hi, w2, xlo, xhi, n_half):
    half = CMP_LEN // 2
    for l in range(half):
        xl = load_row(l)
        cols = slice(l * HEAD_DIM, (l + 1) * HEAD_DIM)
        xlo[:, cols] = (xl + pe_l(l)).astype(CD)
        xhi[:, cols] = (xl + pe_l(half + l)).astype(CD)
    hid = jax.nn.gelu(_dot(xlo[...], w1_lo) + pltpu.roll(_dot(xhi[...], w1_hi), n_half - 1, 0))
    return _dot(hid.astype(CD), w2)


def _compress_prompt_kernel(x_ref, pe_ref, w1_ref, w2_ref, o_ref, xlo, xhi, *, n_half):
    kh = (CMP_LEN // 2) * HEAD_DIM
    tok = _compress_rows(lambda l: x_ref[pl.ds(l, n_half, stride=CMP_STRIDE), :],
                         lambda l: pe_ref[0, l:l + 1, :], w1_ref[0, 0:kh, :], w1_ref[0, kh:2 * kh, :], w2_ref[0],
                         xlo, xhi, n_half)
    o_ref[0, 0] = tok.astype(o_ref.dtype)


def _compress_prompt(kv_c, pe, w1, w2):
    assert CMP_LEN == 2 * CMP_STRIDE and SEQ % (CMP_STRIDE * SUBLANES) == 0
    n_half = SEQ // CMP_STRIDE
    kh = (CMP_LEN // 2) * HEAD_DIM
    blk = 2 * _nbytes((SEQ, HEAD_DIM), F32) + 8 * _nbytes((n_half, HEAD_DIM), F32) \
        + 4 * _nbytes((2 * kh, HEAD_DIM), CD) + 2 * _nbytes((n_half, kh), CD)
    return pl.pallas_call(
        functools.partial(_compress_prompt_kernel, n_half=n_half),
        grid=(BATCH, KVC),
        in_specs=[pl.BlockSpec((SEQ, HEAD_DIM), lambda b, c: (b, c)),
                  pl.BlockSpec((1, CMP_LEN, HEAD_DIM), lambda b, c: (c // KV_HEADS, 0, 0)),
                  pl.BlockSpec((1, 2 * kh, HEAD_DIM), lambda b, c: (c // KV_HEADS, 0, 0)),
                  pl.BlockSpec((1, HEAD_DIM, HEAD_DIM), lambda b, c: (c // KV_HEADS, 0, 0))],
        out_specs=pl.BlockSpec((1, 1, n_half, HEAD_DIM), lambda b, c: (b, c, 0, 0)),
        out_shape=jax.ShapeDtypeStruct((BATCH, KVC, n_half, HEAD_DIM), CD),
        scratch_shapes=[pltpu.VMEM((n_half, kh), CD), pltpu.VMEM((n_half, kh), CD)],
        compiler_params=_params(("parallel", "parallel"), blk),
        name="nsa_compress_prompt",
    )(kv_c, pe, w1, w2)


def _compress_sample_kernel(pt_ref, *refs, n_half):
    page_refs = refs[:CMP_PAGES]
    pe_ref, w1_ref, w2_ref, o_ref, buf, xlo, xhi = refs[CMP_PAGES:]
    step = pl.program_id(1)
    page_rows = PAGE_SIZE * KVC
    kh = (CMP_LEN // 2) * HEAD_DIM
    for k, pr in enumerate(page_refs):
        start = pl.multiple_of((step * CMP_PAGES + k) * PAGE_SIZE, PAGE_SIZE)
        for c in range(KVC):
            buf[c, pl.ds(start, PAGE_SIZE), :] = _head_rows(pr, c, PAGE_SIZE)

    @pl.when(step == pl.num_programs(1) - 1)
    def _():
        def one_slot(c, carry):
            kv = c // KV_HEADS
            tok = _compress_rows(lambda l: buf[c, pl.ds(l, n_half, stride=CMP_STRIDE), :],
                                 lambda l: pe_ref[kv, pl.ds(l, 1), :], w1_ref[kv, pl.ds(0, kh), :],
                                 w1_ref[kv, pl.ds(kh, kh), :], w2_ref[kv], xlo, xhi, n_half)
            o_ref[c] = tok.astype(o_ref.dtype)
            return carry
        lax.fori_loop(0, KVC, one_slot, 0)


def _compress_sample(pool, layer, page_table, pe, w1, w2):
    nb, n_pages = page_table.shape
    assert n_pages % CMP_PAGES == 0 and PAST_LEN % (CMP_STRIDE * SUBLANES) == 0
    n_half = PAST_LEN // CMP_STRIDE
    page_rows = PAGE_SIZE * KVC
    kh = (CMP_LEN // 2) * HEAD_DIM
    blk = _nbytes((PAST_LEN * KVC, HEAD_DIM), F32) + 2 * CMP_PAGES * _nbytes((page_rows, HEAD_DIM), F32) \
        + 4 * _nbytes((2, 2 * kh, HEAD_DIM), CD) + 2 * _nbytes((KVC, n_half, HEAD_DIM), CD) \
        + 2 * _nbytes((n_half, kh), CD) + 10 * _nbytes((n_half, HEAD_DIM), F32)

    def page_spec(k):
        return pl.BlockSpec((None, None, page_rows, HEAD_DIM),
                            lambda b, s, pt: (layer, pt[b, s * CMP_PAGES + k], 0, 0))

    return pl.pallas_call(
        functools.partial(_compress_sample_kernel, n_half=n_half),
        grid_spec=pltpu.PrefetchScalarGridSpec(
            num_scalar_prefetch=1,
            grid=(nb, n_pages // CMP_PAGES),
            in_specs=[page_spec(k) for k in range(CMP_PAGES)]
            + [pl.BlockSpec((2, CMP_LEN, HEAD_DIM), lambda b, s, pt: (0, 0, 0)),
               pl.BlockSpec((2, 2 * kh, HEAD_DIM), lambda b, s, pt: (0, 0, 0)),
               pl.BlockSpec((2, HEAD_DIM, HEAD_DIM), lambda b, s, pt: (0, 0, 0))],
            out_specs=pl.BlockSpec((None, KVC, n_half, HEAD_DIM), lambda b, s, pt: (b, 0, 0, 0)),
            scratch_shapes=[pltpu.VMEM((KVC, PAST_LEN, HEAD_DIM), F32), pltpu.VMEM((n_half, kh), CD),
                            pltpu.VMEM((n_half, kh), CD)]),
        out_shape=jax.ShapeDtypeStruct((nb, KVC, n_half, HEAD_DIM), CD),
        compiler_params=_params(("parallel", "arbitrary"), blk),
        name="nsa_compress_sample",
    )(page_table, *([pool] * CMP_PAGES), pe, w1, w2)


def _masked_softmax(s, mask):
    s = jnp.where(mask, s, NEG_INF)
    m = jnp.max(s, axis=-1, keepdims=True)
    e = jnp.where(mask, jnp.exp(s - m), 0.0)
    d = jnp.sum(e, axis=-1, keepdims=True)
    return e * (1.0 / jnp.where(d > 0.0, d, 1.0))


def _attend_bias(qs, k, v, bias, tq):
    os = []
    for r in range(GROUP):
        t = _dot_nt(qs[r * tq:(r + 1) * tq], k) + bias
        e = jnp.exp2((t - jnp.max(t, axis=-1, keepdims=True)) * (SCALE * LOG2E))
        d = jnp.sum(e, axis=-1, keepdims=True)
        os.append(_dot(e.astype(CD), v) * (1.0 / d))
    return jnp.concatenate(os, axis=0)


def _group_sum_lanes(p, n_out):
    n = p.shape[1]
    src = lax.broadcasted_iota(jnp.int32, (n, n_out), 0) // PER_SEL
    dst = lax.broadcasted_iota(jnp.int32, (n, n_out), 1)
    gmat = (src == dst).astype(CD)
    h1 = p.astype(CD)
    r1 = p - h1.astype(F32)
    h2 = r1.astype(CD)
    h3 = (r1 - h2.astype(F32)).astype(CD)
    return _dot(h1, gmat) + _dot(h2, gmat) + _dot(h3, gmat)


def _stack_heads(q, col0):
    return jnp.concatenate([q[:, col0 + r * HEAD_DIM: col0 + (r + 1) * HEAD_DIM] for r in range(GROUP)], axis=0)


def _gate_sum(gates, col0, o_c, o_s, o_w, r, rows):
    return (gates[:, col0 + r:col0 + r + 1] * o_c[rows]
            + gates[:, col0 + GROUP + r:col0 + GROUP + r + 1] * o_s[rows]
            + gates[:, col0 + 2 * GROUP + r:col0 + 2 * GROUP + r + 1] * o_w[rows])


def _attn_prompt_kernel(q_ref, ks_ref, vs_ref, kw_ref, vw_ref, kc_ref, vc_ref, g_ref, o_ref, *, tq, T):
    i = pl.program_id(2)
    n_sel = T // SEL_LEN
    n_cmp = T // CMP_STRIDE

    qs = _stack_heads(q_ref[...], 0)
    qpos1 = i * tq + lax.broadcasted_iota(jnp.int32, (tq, 1), 0)
    qpos = jnp.concatenate([qpos1] * GROUP, axis=0)

    span = WINDOW + tq
    start = pl.multiple_of(jnp.maximum(i * tq - WINDOW, 0), tq)
    kposw = start + lax.broadcasted_iota(jnp.int32, (1, span), 1)
    bias_w = jnp.where((kposw <= qpos1) & (kposw > qpos1 - WINDOW), 0.0, NEG_INF)
    o_w = _attend_bias(qs, kw_ref[pl.ds(start, span), :], vw_ref[pl.ds(start, span), :], bias_w, tq)

    s = _dot_nt(qs, kc_ref[0, 0]) * SCALE
    blk_end = lax.broadcasted_iota(jnp.int32, (1, n_cmp), 1) * CMP_STRIDE + (CMP_LEN - 1)
    p = _masked_softmax(s, (blk_end <= qpos) & (blk_end < T))
    o_c = _dot(p.astype(CD), vc_ref[0, 0])
    psum = p[0:tq]
    for r in range(1, GROUP):
        psum = psum + p[r * tq:(r + 1) * tq]
    imp = _group_sum_lanes(psum, LANES)

    j = lax.broadcasted_iota(jnp.int32, (1, LANES), 1)
    cur = qpos1 // SEL_LEN
    forced = (j == 0) | (j == cur) | (j == cur - 1)
    score = jnp.where(forced, FORCE, imp)
    score = jnp.where(j <= cur, score, -1.0)
    sc = score.T[:n_sel]
    jrow = lax.broadcasted_iota(jnp.int32, (n_sel, 1), 0)
    rank = jnp.zeros((n_sel, tq), jnp.int32)
    for a in range(n_sel):
        sa = sc[a:a + 1, :]
        rank = rank + ((sa > sc) | ((sa == sc) & (a < jrow))).astype(jnp.int32)
    sel_t = ((rank < SEL_TOP) & (sc >= 0.0)).astype(F32)
    if n_sel < LANES:
        sel_t = jnp.concatenate([sel_t, jnp.zeros((LANES - n_sel, tq), F32)], axis=0)
    sel = sel_t.T.astype(CD)

    def sel_branch(nk):
        def run():
            expand = (lax.broadcasted_iota(jnp.int32, (LANES, nk), 1) // SEL_LEN
                      == lax.broadcasted_iota(jnp.int32, (LANES, nk), 0)).astype(CD)
            key_sel = _dot(sel, expand)
            kpos = lax.broadcasted_iota(jnp.int32, (1, nk), 1)
            bias = jnp.where((key_sel > 0.5) & (kpos <= qpos1), 0.0, NEG_INF)
            return _attend_bias(qs, ks_ref[0:nk, :], vs_ref[0:nk, :], bias, tq)
        return run

    n_ext = T // SEL_KEY_STEP
    o_s = lax.switch((i * tq) // SEL_KEY_STEP, [sel_branch(SEL_KEY_STEP * (k + 1)) for k in range(n_ext)])

    g = g_ref[...]
    for r in range(GROUP):
        o = _gate_sum(g, 0, o_c, o_s, o_w, r, slice(r * tq, (r + 1) * tq))
        o_ref[:, r * HEAD_DIM:(r + 1) * HEAD_DIM] = o.astype(o_ref.dtype)


def _attn_prompt(q, kv_sel, kv_win, kc, gates):
    T = SEQ
    tq = Q_TILE
    assert T % tq == 0 and WINDOW % tq == 0 and WINDOW + tq <= T and T // SEL_LEN <= LANES
    assert T // CMP_STRIDE == LANES and T % SEL_KEY_STEP == 0 and SEL_KEY_STEP % tq == 0
    nq = T // tq
    gw = GROUP * HEAD_DIM
    blk = 2 * 4 * _nbytes((T, HEAD_DIM), CD) + 8 * _nbytes((GROUP * tq, T), F32) + 4 * _nbytes((tq, gw), F32)
    k_spec = pl.BlockSpec((T, HEAD_DIM), lambda b, g, i: (b, g))
    v_spec = pl.BlockSpec((T, HEAD_DIM), lambda b, g, i: (b, KV_HEADS + g))
    return pl.pallas_call(
        functools.partial(_attn_prompt_kernel, tq=tq, T=T),
        grid=(BATCH, KV_HEADS, nq),
        in_specs=[pl.BlockSpec((tq, gw), lambda b, g, i: (b * nq + i, g)),
                  k_spec, v_spec, k_spec, v_spec,
                  pl.BlockSpec((1, 1, T // CMP_STRIDE, HEAD_DIM), lambda b, g, i: (b, g, 0, 0)),
                  pl.BlockSpec((1, 1, T // CMP_STRIDE, HEAD_DIM), lambda b, g, i: (b, KV_HEADS + g, 0, 0)),
                  pl.BlockSpec((tq, LANES), lambda b, g, i: (b * nq + i, g))],
        out_specs=pl.BlockSpec((tq, gw), lambda b, g, i: (b * nq + i, g)),
        out_shape=jax.ShapeDtypeStruct((q.shape[0], ATTN_W), CD),
        compiler_params=_params(("parallel", "parallel", "parallel"), blk),
        name="nsa_prompt_attention",
    )(q, kv_sel, kv_sel, kv_win, kv_win, kc, kc, gates)


def _sample_cmp_kernel(q_ref, kc_ref, vc_ref, oc_ref, sel_ref, *, n_cmp_valid):
    T = DEC_SEQ
    n_cmp = kc_ref.shape[2]
    qs = _stack_heads(q_ref[...], 0).astype(CD)
    qpos1 = PAST_LEN + lax.broadcasted_iota(jnp.int32, (T, 1), 0)
    qpos = jnp.concatenate([qpos1] * GROUP, axis=0)
    s = _dot_nt(qs, kc_ref[0, 0]) * SCALE
    n = lax.broadcasted_iota(jnp.int32, (1, n_cmp), 1)
    p = _masked_softmax(s, (n * CMP_STRIDE + (CMP_LEN - 1) <= qpos) & (n < n_cmp_valid))
    oc_ref[0, 0] = _dot(p.astype(CD), vc_ref[0, 0])
    psum = p[0:T]
    for r in range(1, GROUP):
        psum = psum + p[r * T:(r + 1) * T]
    imp = _group_sum_lanes(psum, LANES)

    j = lax.broadcasted_iota(jnp.int32, (1, LANES), 1)
    cur = qpos1 // SEL_LEN
    score = jnp.where((j == 0) | (j == cur - 1), FORCE, imp)
    rank = jnp.zeros((T, LANES), jnp.int32)
    for sh in range(1, LANES):
        other = pltpu.roll(score, sh, 1)
        oj = jnp.where(j >= sh, j - sh, j - sh + LANES)
        rank = rank + ((other > score) | ((other == score) & (oj < j))).astype(jnp.int32)
    sel_ref[0, 0] = (rank < SEL_TOP - 1).astype(F32)


def _sample_cmp(q_s, kc_s, n_cmp_valid):
    nb = DEC_BATCH
    gw = GROUP * HEAD_DIM
    n_cmp = kc_s.shape[2]
    blk = 16 * _nbytes((GROUP * DEC_SEQ, n_cmp), F32) + 4 * _nbytes((n_cmp, HEAD_DIM), CD)
    return pl.pallas_call(
        functools.partial(_sample_cmp_kernel, n_cmp_valid=n_cmp_valid),
        grid=(nb, KV_HEADS),
        in_specs=[pl.BlockSpec((DEC_SEQ, gw), lambda b, g: (b, g)),
                  pl.BlockSpec((1, 1, n_cmp, HEAD_DIM), lambda b, g: (b, g, 0, 0)),
                  pl.BlockSpec((1, 1, n_cmp, HEAD_DIM), lambda b, g: (b, KV_HEADS + g, 0, 0))],
        out_specs=[pl.BlockSpec((1, 1, GROUP * DEC_SEQ, HEAD_DIM), lambda b, g: (b, g, 0, 0)),
                   pl.BlockSpec((1, 1, DEC_SEQ, LANES), lambda b, g: (b, g, 0, 0))],
        out_shape=[jax.ShapeDtypeStruct((nb, KV_HEADS, GROUP * DEC_SEQ, HEAD_DIM), F32),
                   jax.ShapeDtypeStruct((nb, KV_HEADS, DEC_SEQ, LANES), F32)],
        compiler_params=_params(("parallel", "parallel"), blk),
        name="nsa_sample_cmp_attention",
    )(q_s, kc_s, kc_s)


def _put_rows_kernel(src_ref, dst_any, o_ref):
    o_ref[...] = src_ref[...].astype(o_ref.dtype)


def _put_rows(dst, src, row_off):
    n, w = src.shape
    assert row_off % n == 0 and n % 16 == 0
    return pl.pallas_call(
        _put_rows_kernel,
        grid=(1,),
        in_specs=[pl.BlockSpec((n, w), lambda i: (0, 0)), pl.BlockSpec(memory_space=pl.ANY)],
        out_specs=pl.BlockSpec((n, w), lambda i: (row_off // n, 0)),
        out_shape=jax.ShapeDtypeStruct(dst.shape, dst.dtype),
        input_output_aliases={1: 0},
        compiler_params=_params(("arbitrary",), 4 * _nbytes((n, w), F32)),
        name="put_sample_rows",
    )(src, dst)


def _pad_rows(x, rows):
    return jnp.concatenate([x, jnp.zeros((rows - x.shape[0], x.shape[1]), x.dtype)], axis=0)


def _head_rows(ref, slot, n):
    return ref[pl.ds(slot, n, stride=KVC), :]


def _sample_sel_kernel(pt_ref, q_ref, sel_ref, *refs):
    page_refs = refs[:SEL_PAGES]
    new_ref, o_ref, m_sc, l_sc, acc_sc = refs[SEL_PAGES:]
    T = DEC_SEQ
    R = GROUP * T
    step = pl.program_id(1)
    nk = SEL_PAGES * PAGE_SIZE

    @pl.when(step == 0)
    def _():
        m_sc[...] = jnp.full(m_sc.shape, NEG_INF, F32)
        l_sc[...] = jnp.zeros(l_sc.shape, F32)
        acc_sc[...] = jnp.zeros(acc_sc.shape, F32)

    def update(parts):
        m_prev, l_prev, acc_prev = m_sc[...], l_sc[...], acc_sc[...]
        ms, ls, accs = [], [], []
        for g, (s, mask, v) in enumerate(parts):
            rows = slice(g * R, (g + 1) * R)
            m_new = jnp.maximum(m_prev[rows], jnp.max(jnp.where(mask, s, NEG_INF), axis=-1, keepdims=True))
            alpha = jnp.exp(m_prev[rows] - m_new)
            e = jnp.where(mask, jnp.exp(s - m_new), 0.0)
            ms.append(m_new)
            ls.append(alpha * l_prev[rows] + jnp.sum(e, axis=-1, keepdims=True))
            accs.append(alpha * acc_prev[rows] + _dot(e.astype(CD), v))
        m_sc[...] = jnp.concatenate(ms, axis=0)
        l_sc[...] = jnp.concatenate(ls, axis=0)
        acc_sc[...] = jnp.concatenate(accs, axis=0)

    q = q_ref[...]
    blk_of_key = step * (nk // SEL_LEN) + lax.broadcasted_iota(jnp.int32, (LANES, nk), 1) // SEL_LEN
    expand = (blk_of_key == lax.broadcasted_iota(jnp.int32, (LANES, nk), 0)).astype(CD)
    key_sel = _dot(sel_ref[...].reshape(KV_HEADS * T, LANES).astype(CD), expand)
    parts = []
    for g in range(KV_HEADS):
        qg = _stack_heads(q, g * GROUP * HEAD_DIM).astype(CD)
        k = jnp.concatenate([_head_rows(pr, g, PAGE_SIZE) for pr in page_refs], axis=0).astype(CD)
        v = jnp.concatenate([_head_rows(pr, KV_HEADS + g, PAGE_SIZE) for pr in page_refs], axis=0).astype(CD)
        mask = jnp.concatenate([key_sel[g * T:(g + 1) * T]] * GROUP, axis=0) > 0.5
        parts.append((_dot_nt(qg, k) * SCALE, mask, v))
    update(parts)

    @pl.when(step == pl.num_programs(1) - 1)
    def _():
        t = jnp.concatenate([lax.broadcasted_iota(jnp.int32, (T, 1), 0)] * GROUP, axis=0)
        i = lax.broadcasted_iota(jnp.int32, (1, LANES), 1)
        mask = (i < T) & (i <= t)
        parts = []
        for g in range(KV_HEADS):
            qg = _stack_heads(q, g * GROUP * HEAD_DIM).astype(CD)
            k = _pad_rows(_head_rows(new_ref, g, T), LANES).astype(CD)
            v = _pad_rows(_head_rows(new_ref, KV_HEADS + g, T), LANES).astype(CD)
            parts.append((_dot_nt(qg, k) * SCALE, mask, v))
        update(parts)
        o_ref[...]= (acc_sc[...] * (1.0 / l_sc[...])).reshape(o_ref.shape)


def _sample_sel(q_s, sel, pool, layer, page_table, new_rows):
    nb, n_pages = page_table.shape
    assert n_pages % SEL_PAGES == 0 and (SEL_PAGES * PAGE_SIZE) % SEL_LEN == 0
    R = GROUP * DEC_SEQ
    page_rows = PAGE_SIZE * KVC
    blk = 2 * SEL_PAGES * _nbytes((page_rows, HEAD_DIM), F32) + 16 * _nbytes((R, SEL_PAGES * PAGE_SIZE), F32) \
        + 8 * _nbytes((SEL_PAGES * PAGE_SIZE, HEAD_DIM), F32)

    def page_spec(k):
        return pl.BlockSpec((None, None, page_rows, HEAD_DIM),
                            lambda b, s, pt: (layer, pt[b, s * SEL_PAGES + k], 0, 0))

    return pl.pallas_call(
        _sample_sel_kernel,
        grid_spec=pltpu.PrefetchScalarGridSpec(
            num_scalar_prefetch=1,
            grid=(nb, n_pages // SEL_PAGES),
            in_specs=[pl.BlockSpec((DEC_SEQ, ATTN_W), lambda b, s, pt: (b, 0)),
                      pl.BlockSpec((None, KV_HEADS, DEC_SEQ, LANES), lambda b, s, pt: (b, 0, 0, 0))]
            + [page_spec(k) for k in range(SEL_PAGES)]
            + [pl.BlockSpec((DEC_SEQ * KVC, HEAD_DIM), lambda b, s, pt: (b, 0))],
            out_specs=pl.BlockSpec((None, KV_HEADS, R, HEAD_DIM), lambda b, s, pt: (b, 0, 0, 0)),
            scratch_shapes=[pltpu.VMEM((KV_HEADS * R, 1), F32), pltpu.VMEM((KV_HEADS * R, 1), F32),
                            pltpu.VMEM((KV_HEADS * R, HEAD_DIM), F32)]),
        out_shape=jax.ShapeDtypeStruct((nb, KV_HEADS, R, HEAD_DIM), F32),
        compiler_params=_params(("parallel", "arbitrary"), blk),
        name="nsa_sample_sel_attention",
    )(page_table, q_s, sel, *([pool] * SEL_PAGES), new_rows)


def _sample_win_kernel(q_ref, buf_ref, new_ref, oc_ref, os_ref, g_ref, o_ref, *, wb):
    T = DEC_SEQ
    q = q_ref[...]
    gates = g_ref[...]
    t = jnp.concatenate([lax.broadcasted_iota(jnp.int32, (T, 1), 0)] * GROUP, axis=0)
    qpos = PAST_LEN + t
    i = lax.broadcasted_iota(jnp.int32, (1, wb + LANES), 1)
    kpos = PAST_LEN - wb + i
    mask = (i < wb + T) & (kpos <= qpos) & (kpos > qpos - WINDOW)
    for g in range(KV_HEADS):
        qg = _stack_heads(q, g * GROUP * HEAD_DIM).astype(CD)
        k = jnp.concatenate([_head_rows(buf_ref, g, wb), _pad_rows(_head_rows(new_ref, g, T), LANES)],
                            axis=0).astype(CD)
        v = jnp.concatenate([_head_rows(buf_ref, KV_HEADS + g, wb),
                             _pad_rows(_head_rows(new_ref, KV_HEADS + g, T), LANES)], axis=0).astype(CD)
        p = _masked_softmax(_dot_nt(qg, k) * SCALE, mask)
        o_w = _dot(p.astype(CD), v)
        o_c = oc_ref[g]
        o_s = os_ref[g]
        for r in range(GROUP):
            o = _gate_sum(gates, g * LANES, o_c, o_s, o_w, r, slice(r * T, (r + 1) * T))
            col = (g * GROUP + r) * HEAD_DIM
            o_ref[:, col:col + HEAD_DIM] = o


def _sample_win(q_s, win_buf, layer, new_rows, o_c, o_s, gates_s):
    nb = DEC_BATCH
    wb = win_buf.shape[2] // KVC
    R = GROUP * DEC_SEQ
    assert wb % LANES == 0
    blk = 2 * _nbytes((wb * KVC, HEAD_DIM), F32) + 8 * _nbytes((R, wb + LANES), F32) \
        + 8 * _nbytes((wb + LANES, HEAD_DIM), F32)
    return pl.pallas_call(
        functools.partial(_sample_win_kernel, wb=wb),
        grid=(nb,),
        in_specs=[pl.BlockSpec((DEC_SEQ, ATTN_W), lambda b: (b, 0)),
                  pl.BlockSpec((None, None, wb * KVC, HEAD_DIM), lambda b: (layer, b, 0, 0)),
                  pl.BlockSpec((DEC_SEQ * KVC, HEAD_DIM), lambda b: (b, 0)),
                  pl.BlockSpec((None, KV_HEADS, R, HEAD_DIM), lambda b: (b, 0, 0, 0)),
                  pl.BlockSpec((None, KV_HEADS, R, HEAD_DIM), lambda b: (b, 0, 0, 0)),
                  pl.BlockSpec((DEC_SEQ, KV_HEADS * LANES), lambda b: (b, 0))],
        out_specs=pl.BlockSpec((DEC_SEQ, ATTN_W), lambda b: (b, 0)),
        out_shape=jax.ShapeDtypeStruct((nb * DEC_SEQ, ATTN_W), F32),
        compiler_params=_params(("parallel",), blk),
        name="nsa_sample_win_attention",
    )(q_s, win_buf, new_rows, o_c, o_s, gates_s)


def _half_ffn(h, hn, w_gate, w_up, w_down, layer, g_next):
    act = _swiglu(hn, w_gate, w_up, layer)
    return _resid_proj(act, w_down, layer, h, 0.5, g_next, ROW_TILE // 2, COL_TILE // 2, "ffn_down")


W_IN_ROW_TILE = 512


def _w_in_cast_kernel(w_ref, o_ref):
    o_ref[...] = w_ref[0].astype(o_ref.dtype)


def _prep_w_in(w_in):
    o1 = ATTN_W
    o4 = o1 + 6 * KV_W
    n_gate = 3 * N_HEADS
    wt = jnp.swapaxes(w_in, 1, 2)
    L, n_in, D = wt.shape
    n_out = n_in - n_gate
    tr = W_IN_ROW_TILE
    assert o4 % tr == 0 and n_out % tr == 0 and n_gate % 16 == 0

    def src_row(j):
        return pl.multiple_of(j * tr + jnp.where(j * tr >= o4, n_gate, 0), 16)

    blk = 2 * (_nbytes((tr, D), F32) + _nbytes((tr, D), CD)) + 2 * _nbytes((tr, D), F32)
    main = pl.pallas_call(
        _w_in_cast_kernel,
        grid=(L, n_out // tr),
        in_specs=[pl.BlockSpec((pl.Element(1), pl.Element(tr), pl.Element(D)), lambda l, j: (l, src_row(j), 0))],
        out_specs=pl.BlockSpec((None, tr, D), lambda l, j: (l, j, 0)),
        out_shape=jax.ShapeDtypeStruct((L, n_out, D), CD),
        compiler_params=_params(("parallel", "parallel"), blk),
        name="w_in_cast",
    )(wt)
    wg = wt[:, o4:o4 + n_gate].reshape(L, 3, KV_HEADS, GROUP, D).transpose(0, 2, 1, 3, 4)
    wg = jnp.pad(wg.reshape(L, KV_HEADS, 3 * GROUP, D), ((0, 0), (0, 0), (0, LANES - 3 * GROUP), (0, 0)))
    return dict(main=main, gate=wg.reshape(L, KV_HEADS * LANES, D).astype(CD))


def _layer(h, hn, layer, W, small, caches, page_table, kv_bufs, g_after):
    (nm, pe, w1, w2, ws, bs, n2) = small
    cache_cmp, cache_sel, cache_win = caches
    MP = BATCH * SEQ
    MS = DEC_BATCH * DEC_SEQ
    n_pages = page_table.shape[1]
    assert n_pages * PAGE_SIZE == PAST_LEN and PAST_LEN // SEL_LEN == LANES and DEC_SEQ <= SEL_LEN
    assert DEC_SEQ == SUBLANES and MP % MS == 0

    h, n = _half_ffn(h, hn, W["g1"], W["u1"], W["d1"], layer, nm)

    wm = W["main"]
    kv0 = ATTN_W
    uv0 = kv0 + 6 * KV_W
    mg0 = uv0 + 2 * GM_WIDTH
    q = _proj(n, wm, 0, ATTN_W, layer, None, CD, "proj_q")
    gates = _proj(n, W["gate"], 0, KV_HEADS * LANES, layer, "sigmoid", F32, "proj_branch_gates")
    cmp_buf, kv_c = _kv_proj_prompt(n, wm, kv0, layer, kv_bufs[0], F32, False, "proj_kv_cmp")
    sel_buf, kv_sel = _kv_proj_prompt(n, wm, kv0 + 2 * KV_W, layer, kv_bufs[1], CD, False, "proj_kv_sel")
    win_buf, kv_win = _kv_proj_prompt(n, wm, kv0 + 4 * KV_W, layer, kv_bufs[2], CD, True, "proj_kv_win")
    new_s = [_kv_proj_sample(n, wm, kv0 + s * 2 * KV_W, layer, "proj_kv_sample") for s in range(3)]

    w1c = w1.astype(CD).reshape(2, CMP_LEN * HEAD_DIM, HEAD_DIM)
    w2c = w2.astype(CD)

    kc_p = _compress_prompt(kv_c, pe, w1c, w2c)
    o_p = _attn_prompt(q, kv_sel, kv_win, kc_p, gates)
    (gm,) = _gmlp(n, wm, uv0, layer, ws, bs[:, :, None], MP, 0, 1024, GM_CHUNK, GM_CHUNK, None, False)

    q_s = q[MP:].astype(F32)
    gates_s = gates[MP:]
    kc_s = _compress_sample(cache_cmp, layer, page_table, pe, w1c, w2c)
    n_cmp_valid = (PAST_LEN + DEC_SEQ - CMP_LEN) // CMP_STRIDE + 1
    oc_s, sel_s = _sample_cmp(q_s, kc_s, n_cmp_valid)
    os_s = _sample_sel(q_s, sel_s, cache_sel, layer, page_table, new_s[1])
    o_s = _sample_win(q_s, cache_win, layer, new_s[2], oc_s, os_s, gates_s)
    cl = min(DEC_SEQ, GM_CHUNK)
    ws_s = jnp.tile(ws[:, :cl, :cl], (1, MS // cl, MS // cl))
    bs_s = jnp.tile(bs[:, :cl], (1, MS // cl))[:, :, None]
    gm, v_s = _gmlp(n, wm, uv0, layer, ws_s, bs_s, MS, MP, MS, MS, cl, gm, True)

    o = _put_rows(o_p, o_s, MP)
    merged = _merge(n, o, gm, wm, mg0, W["wua"], W["wug"], layer)
    h, n = _resid_proj(merged, W["wo"], layer, h, 1.0, n2, ROW_TILE, COL_TILE // 2, "mixer_out")
    h, hn = _half_ffn(h, n, W["g2"], W["u2"], W["d2"], layer, g_after)
    return h, hn, (cmp_buf, sel_buf, win_buf), (new_s, v_s)


def kernel(x_prompt, x_sample, cache_cmp_kv, cache_sel_kv, cache_win_kv, page_table, norm_ffn1, ffn1_gate, ffn1_up, ffn1_down, norm_mix, w_in, cmp_pe, cmp_w1, cmp_w2, gm_ws, gm_bs, w_up_attn, w_up_gm, w_out, norm_ffn2, ffn2_gate, ffn2_up, ffn2_down, norm_final):
    MP = BATCH * SEQ
    MS = DEC_BATCH * DEC_SEQ
    kv_row = (2, KV_HEADS, HEAD_DIM)
    h = jnp.concatenate([x_prompt.reshape(MP, D_MODEL), x_sample.reshape(MS, D_MODEL)], axis=0)
    n_phys = cache_cmp_kv.shape[1]
    wb = cache_win_kv.shape[2]
    caches = (cache_cmp_kv.reshape(DEPTH, n_phys, PAGE_SIZE * KVC, HEAD_DIM),
              cache_sel_kv.reshape(DEPTH, n_phys, PAGE_SIZE * KVC, HEAD_DIM),
              cache_win_kv.reshape(DEPTH, DEC_BATCH, wb * KVC, HEAD_DIM))
    W = _prep_w_in(w_in)
    W.update(g1=ffn1_gate, u1=ffn1_up, d1=ffn1_down.astype(CD),
             g2=ffn2_gate, u2=ffn2_up, d2=ffn2_down.astype(CD),
             wua=w_up_attn, wug=w_up_gm, wo=w_out)
    kv_bufs = (None, None, None)
    sample_outs = []
    hn = _norm_prep(h, norm_ffn1[0])
    for l in range(DEPTH):
        small = (norm_mix[l], cmp_pe[l], cmp_w1[l], cmp_w2[l], gm_ws[l], gm_bs[l], norm_ffn2[l])
        g_after = norm_ffn1[l + 1] if l + 1 < DEPTH else None
        h, hn, kv_bufs, s_out = _layer(h, hn, l, W, small, caches, page_table, kv_bufs, g_after)
        sample_outs.append(s_out)
    y_prompt = _rmsnorm(h, norm_final, F32, rows=MP, row_off=0).reshape(BATCH, SEQ, D_MODEL)
    y_sample = _rmsnorm(h, norm_final, F32, rows=MS, row_off=MP, tile=MS).reshape(DEC_BATCH, DEC_SEQ, D_MODEL)

    cmp_buf, sel_buf, win_buf = kv_bufs
    wk = min(WINDOW, SEQ)
    new = [jnp.stack([sample_outs[l][0][s] for l in range(DEPTH)]).reshape((DEPTH, DEC_BATCH, DEC_SEQ) + kv_row)
           for s in range(3)]
    win_all = jnp.concatenate([cache_win_kv, new[2]], axis=2)
    win_s = win_all[:, :, -min(WINDOW, win_all.shape[2]):]
    gm_v = jnp.stack([sample_outs[l][1] for l in range(DEPTH)]).reshape(DEPTH, DEC_BATCH, DEC_SEQ, GM_WIDTH)
    return (y_prompt, y_sample,
            cmp_buf.reshape((DEPTH, BATCH, SEQ) + kv_row), new[0],
            sel_buf.reshape((DEPTH, BATCH, SEQ) + kv_row), new[1],
            win_buf.reshape((DEPTH, BATCH, wk) + kv_row), win_s, gm_v)
```

```python
import functools

import jax
import jax.numpy as jnp
from jax import lax
from jax.experimental import pallas as pl
from jax.experimental.pallas import tpu as pltpu

D_MODEL = 4096
BATCH = 4
SEQ = 2048
DEPTH = 4
DEC_BATCH = 8
DEC_SEQ = 8
PAST_LEN = 8192
PAGE_SIZE = 128

N_HEADS = 16
KV_HEADS = 4
HEAD_DIM = 128
GROUP = N_HEADS // KV_HEADS
ATTN_W = N_HEADS * HEAD_DIM
KV_W = KV_HEADS * HEAD_DIM
CMP_LEN = 32
CMP_STRIDE = 16
SEL_LEN = 64
SEL_TOP = 16
WINDOW = 512
GM_WIDTH = D_MODEL // 2
GM_GROUPS = 4
GM_CHUNK = 128
D_FF = 11008
EPS = 1e-6
NEG_INF = -1e30
FORCE = 1e9
SCALE = HEAD_DIM ** -0.5
LOG2E = 1.4426950408889634

LANES = 128
SUBLANES = 8
VMEM_BUDGET = 56 * 1024 * 1024
CD = jnp.bfloat16
F32 = jnp.float32
PER_SEL = SEL_LEN // CMP_STRIDE
KVC = 2 * KV_HEADS

ROW_TILE = 1376
COL_TILE = 512
ROW_SPLIT = 2
Q_TILE = 512
SEL_KEY_STEP = 512
KV_ROW_TILE = 512
SEL_PAGES = 16
CMP_PAGES = 8


def _pick_tile(n, target, mult):
    best = n
    for t in range(mult, min(n, target) + 1, mult):
        if n % t == 0:
            best = t
    return best


def _params(sem, block_bytes):
    limit = min(VMEM_BUDGET, max(16 * 1024 * 1024, int(block_bytes)))
    return pltpu.CompilerParams(dimension_semantics=sem, vmem_limit_bytes=limit)


def _nbytes(shape, dtype):
    n = 1
    for s in shape:
        n *= s
    return n * jnp.dtype(dtype).itemsize


def _dot(a, b):
    return jnp.dot(a, b, preferred_element_type=F32)


def _dot_nt(a, b):
    return lax.dot_general(a, b, (((1,), (1,)), ((), ())), preferred_element_type=F32)


def _rms_kernel(x_ref, g_ref, o_ref):
    x = x_ref[...]
    y = x * lax.rsqrt(jnp.mean(x * x, axis=-1, keepdims=True) + EPS)
    o_ref[...] = (y * g_ref[...]).astype(o_ref.dtype)


def _rmsnorm(x, g, out_dtype, rows=None, row_off=0, tile=None):
    M, D = x.shape
    rows = M if rows is None else rows
    tm = tile or _pick_tile(rows, 344, 16)
    assert rows % tm == 0 and row_off % tm == 0
    off = row_off // tm
    blk = 2 * _nbytes((tm, D), F32) * 2 + 2 * _nbytes((tm, D), out_dtype)
    return pl.pallas_call(
        _rms_kernel,
        grid=(rows // tm,),
        in_specs=[pl.BlockSpec((tm, D), lambda i: (i + off, 0)),
                  pl.BlockSpec((1, D), lambda i: (0, 0))],
        out_specs=pl.BlockSpec((tm, D), lambda i: (i, 0)),
        out_shape=jax.ShapeDtypeStruct((rows, D), out_dtype),
        compiler_params=_params(("parallel",), blk),
        name="rmsnorm",
    )(x, g.reshape(1, D))


def _inv_rms(ss_ref):
    return lax.rsqrt(ss_ref[...] * (1.0 / D_MODEL) + EPS)


def _ss_spec(ss, tm, off=0):
    return pl.BlockSpec((tm, 1), lambda i, *_: (i + off, 0))


def _norm_prep_kernel(x_ref, g_ref, hb_ref, ss_ref):
    x = x_ref[...]
    hb_ref[...] = (x * g_ref[...]).astype(hb_ref.dtype)
    ss_ref[...] = jnp.sum(x * x, axis=-1, keepdims=True)


def _norm_prep(x, g):
    M, D = x.shape
    tm = _pick_tile(M, 344, 16)
    blk = 4 * _nbytes((tm, D), F32) + 2 * _nbytes((tm, D), CD)
    return pl.pallas_call(
        _norm_prep_kernel,
        grid=(M // tm,),
        in_specs=[pl.BlockSpec((tm, D), lambda i: (i, 0)),
                  pl.BlockSpec((1, D), lambda i: (0, 0))],
        out_specs=[pl.BlockSpec((tm, D), lambda i: (i, 0)), pl.BlockSpec((tm, 1), lambda i: (i, 0))],
        out_shape=[jax.ShapeDtypeStruct((M, D), CD), jax.ShapeDtypeStruct((M, 1), F32)],
        compiler_params=_params(("parallel",), blk),
        name="norm_prep",
    )(x, g.reshape(1, D))


def _proj_kernel(x_ref, ss_ref, w_ref, o_ref, *, act):
    z = _inv_rms(ss_ref) * _dot_nt(x_ref[...], w_ref[...])
    if act == "sigmoid":
        z = jax.nn.sigmoid(z)
    o_ref[...] = z.astype(o_ref.dtype)


def _proj(xn, wt, row0, N, layer, act, out_dtype, name):
    x, ss = xn
    M, K = x.shape
    tm = _pick_tile(M, ROW_TILE, 16)
    tn = _pick_tile(N, COL_TILE, LANES)
    assert row0 % tn == 0
    blk = 2 * (_nbytes((tm, K), x.dtype) + _nbytes((K, tn), wt.dtype) + _nbytes((tm, tn), out_dtype)) \
        + 4 * _nbytes((tm, tn), F32)
    return pl.pallas_call(
        functools.partial(_proj_kernel, act=act),
        grid=(M // tm, N // tn),
        in_specs=[pl.BlockSpec((tm, K), lambda i, j: (i, 0)), _ss_spec(ss, tm),
                  pl.BlockSpec((None, tn, K), lambda i, j: (layer, row0 // tn + j, 0))],
        out_specs=pl.BlockSpec((tm, tn), lambda i, j: (i, j)),
        out_shape=jax.ShapeDtypeStruct((M, N), out_dtype),
        compiler_params=_params(("parallel", "parallel"), blk),
        name=name,
    )(x, ss, wt)


def _row_parts(tm):
    cut = (tm // 32) * 16
    return [slice(0, tm)] if cut == 0 or ROW_SPLIT < 2 else [slice(0, cut), slice(cut, tm)]


def _swiglu_kernel(x_ref, ss_ref, wg_ref, wu_ref, o_ref):
    wg = wg_ref[...].astype(CD)
    wu = wu_ref[...].astype(CD)
    for rows in _row_parts(x_ref.shape[0]):
        inv = _inv_rms(ss_ref)[rows]
        a = inv * _dot(x_ref[rows, :], wg)
        b = inv * _dot(x_ref[rows, :], wu)
        o_ref[rows, :] = (jax.nn.silu(a) * b).astype(o_ref.dtype)


def _swiglu(xn, wg, wu, layer):
    x, ss = xn
    M, K = x.shape
    N = wg.shape[2]
    tm = _pick_tile(M, ROW_TILE, 16)
    tn = _pick_tile(N, COL_TILE // 2, LANES)
    blk = 2 * (_nbytes((tm, K), x.dtype) + 2 * _nbytes((K, tn), wg.dtype) + _nbytes((tm, tn), CD)) \
        + 2 * _nbytes((K, tn), CD) + 6 * _nbytes((tm, tn), F32)
    w_spec = pl.BlockSpec((None, K, tn), lambda i, j: (layer, 0, j))
    return pl.pallas_call(
        _swiglu_kernel,
        grid=(M // tm, N // tn),
        in_specs=[pl.BlockSpec((tm, K), lambda i, j: (i, 0)), _ss_spec(ss, tm), w_spec, w_spec],
        out_specs=pl.BlockSpec((tm, tn), lambda i, j: (i, j)),
        out_shape=jax.ShapeDtypeStruct((M, N), CD),
        compiler_params=_params(("parallel", "parallel"), blk),
        name="ffn_swiglu",
    )(x, ss, wg, wu)


def _resid_kernel(x_ref, w_ref, h_ref, *refs, scale, emit_norm):
    w = w_ref[...].astype(CD)
    o_ref = refs[1] if emit_norm else refs[0]
    sums = []
    for rows in _row_parts(x_ref.shape[0]):
        hn = h_ref[rows, :] + scale * _dot(x_ref[rows, :], w)
        o_ref[rows, :] = hn
        if emit_norm:
            g_ref, _, hb_ref, ss_ref = refs
            hb_ref[rows, :] = (hn * g_ref[...]).astype(hb_ref.dtype)
            sums.append(jnp.sum(hn * hn, axis=-1, keepdims=True))
    if emit_norm:
        part = jnp.concatenate(sums, axis=0)
        j = pl.program_id(1)

        @pl.when(j == 0)
        def _():
            ss_ref[...] = part

        @pl.when(j > 0)
        def _():
            ss_ref[...] += part


def _resid_proj(x, w, layer, h, scale, g_next, tm_target, tn_target, name):
    M, K = x.shape
    N = w.shape[2]
    tm = _pick_tile(M, tm_target, 16)
    tn = _pick_tile(N, tn_target, LANES)
    nj = N // tn
    emit = g_next is not None
    blk = 2 * (_nbytes((tm, K), x.dtype) + _nbytes((K, tn), w.dtype) + 3 * _nbytes((tm, tn), F32)) \
        + (w.dtype != CD) * _nbytes((K, tn), CD) + 4 * _nbytes((tm, tn), F32)
    tile = pl.BlockSpec((tm, tn), lambda i, j: (i, j))
    in_specs = [pl.BlockSpec((tm, K), lambda i, j: (i, 0)),
                pl.BlockSpec((None, K, tn), lambda i, j: (layer, 0, j)), tile]
    args = [x, w, h]
    out_specs = [tile]
    out_shape = [jax.ShapeDtypeStruct((M, N), F32)]
    if emit:
        in_specs.append(pl.BlockSpec((1, tn), lambda i, j: (0, j)))
        args.append(g_next.reshape(1, N))
        out_specs += [tile, pl.BlockSpec((tm, 1), lambda i, j: (i, 0))]
        out_shape += [jax.ShapeDtypeStruct((M, N), CD), jax.ShapeDtypeStruct((M, 1), F32)]
    outs = pl.pallas_call(
        functools.partial(_resid_kernel, scale=scale, emit_norm=emit),
        grid=(M // tm, nj),
        in_specs=in_specs,
        out_specs=out_specs,
        out_shape=out_shape,
        compiler_params=_params(("parallel", "arbitrary"), blk),
        name=name,
    )(*args)
    return (outs[0], (outs[1], outs[2])) if emit else (outs[0], None)


def _merge_kernel(n_ref, ss_ref, o_ref, gm_ref, wma_ref, wmg_ref, wa_ref, wg_ref, out_ref):
    wma, wmg = wma_ref[...], wmg_ref[...]
    wa, wg = wa_ref[...].astype(CD), wg_ref[...].astype(CD)
    for rows in _row_parts(n_ref.shape[0]):
        n = n_ref[rows, :]
        inv = _inv_rms(ss_ref)[rows]
        ga = jax.nn.sigmoid(inv * _dot_nt(n, wma))
        gg = jax.nn.sigmoid(inv * _dot_nt(n, wmg))
        a = _dot(o_ref[rows, :], wa)
        g = _dot(gm_ref[rows, :], wg)
        out_ref[rows, :] = (ga * a + gg * g).astype(out_ref.dtype)


def _merge(xn, o, gm, w_mg, mg_row0, wa, wg, layer):
    n, ss = xn
    M, K = n.shape
    Ka = o.shape[1]
    Kg = gm.shape[1]
    N = wa.shape[2]
    tm = _pick_tile(M, ROW_TILE // 2, 16)
    tn = _pick_tile(N, COL_TILE // 2, LANES)
    nj = N // tn
    blk = 2 * (_nbytes((tm, K + Ka + Kg), CD) + _nbytes((2 * K, tn), CD) + _nbytes((Ka + Kg, tn), wa.dtype)
               + _nbytes((tm, tn), CD)) + _nbytes((Ka + Kg, tn), CD) + 8 * _nbytes((tm, tn), F32)
    return pl.pallas_call(
        _merge_kernel,
        grid=(M // tm, nj),
        in_specs=[pl.BlockSpec((tm, K), lambda i, j: (i, 0)), _ss_spec(ss, tm),
                  pl.BlockSpec((tm, Ka), lambda i, j: (i, 0)),
                  pl.BlockSpec((tm, Kg), lambda i, j: (i, 0)),
                  pl.BlockSpec((None, tn, K), lambda i, j: (layer, mg_row0 // tn + j, 0)),
                  pl.BlockSpec((None, tn, K), lambda i, j: (layer, mg_row0 // tn + j + nj, 0)),
                  pl.BlockSpec((None, Ka, tn), lambda i, j: (layer, 0, j)),
                  pl.BlockSpec((None, Kg, tn), lambda i, j: (layer, 0, j))],
        out_specs=pl.BlockSpec((tm, tn), lambda i, j: (i, j)),
        out_shape=jax.ShapeDtypeStruct((M, N), CD),
        compiler_params=_params(("parallel", "parallel"), blk),
        name="mixer_merge",
    )(n, ss, o, gm, w_mg, w_mg, wa, wg)


def _kv_kernel(*refs, has_prev, has_std, last_only):
    x_ref, ss_ref, w_ref = refs[0], refs[1], refs[2]
    il_ref = refs[3 + has_prev]
    tm = x_ref.shape[0]
    z = _inv_rms(ss_ref) * _dot_nt(x_ref[...], w_ref[...])
    if has_std:
        std_ref = refs[4 + has_prev]
        std_ref[...] = z.astype(std_ref.dtype)

    def write():
        for c in range(KVC):
            il_ref[pl.ds(c, tm, stride=KVC), :] = z[:, c * HEAD_DIM:(c + 1) * HEAD_DIM]

    if last_only:
        tiles_per_batch = SEQ // tm
        pl.when(pl.program_id(0) % tiles_per_batch == tiles_per_batch - 1)(write)
    else:
        write()


def _kv_proj_prompt(xn, w, row0, layer, prev, std_dtype, last_only, name):
    x, ss = xn
    K = x.shape[1]
    W = 2 * KV_W
    MP = BATCH * SEQ
    tm = KV_ROW_TILE
    assert SEQ % tm == 0 and (not last_only or tm == min(WINDOW, SEQ))
    if last_only:
        il_shape = (DEPTH, BATCH, tm * KVC, HEAD_DIM)
        il_spec = pl.BlockSpec((None, None, tm * KVC, HEAD_DIM), lambda i: (layer, i // (SEQ // tm), 0, 0))
    else:
        il_shape = (DEPTH, MP * KVC, HEAD_DIM)
        il_spec = pl.BlockSpec((None, tm * KVC, HEAD_DIM), lambda i: (layer, i, 0))
    in_specs = [pl.BlockSpec((tm, K), lambda i: (i, 0)), _ss_spec(ss, tm),
                pl.BlockSpec((None, W, K), lambda i: (layer, row0 // W, 0))]
    args = [x, ss, w]
    aliases = {}
    if prev is not None:
        in_specs.append(pl.BlockSpec(memory_space=pl.ANY))
        args.append(prev)
        aliases = {3: 0}
    blk = 2 * (_nbytes((tm, K), CD) + _nbytes((K, W), CD) + 2 * _nbytes((tm, W), F32)) + 3 * _nbytes((tm, W), F32)
    return pl.pallas_call(
        functools.partial(_kv_kernel, has_prev=prev is not None, has_std=True, last_only=last_only),
        grid=(MP // tm,),
        in_specs=in_specs,
        out_specs=[il_spec, pl.BlockSpec((tm, W), lambda i: (i, 0))],
        out_shape=[jax.ShapeDtypeStruct(il_shape, F32), jax.ShapeDtypeStruct((MP, W), std_dtype)],
        input_output_aliases=aliases,
        compiler_params=_params(("arbitrary",), blk),
        name=name,
    )(*args)


def _kv_proj_sample(xn, w, row0, layer, name):
    x, ss = xn
    K = x.shape[1]
    W = 2 * KV_W
    MP = BATCH * SEQ
    MS = DEC_BATCH * DEC_SEQ
    assert MP % MS == 0
    blk = 2 * (_nbytes((MS, K), CD) + _nbytes((K, W), CD) + _nbytes((MS, W), F32)) + 3 * _nbytes((MS, W), F32)
    return pl.pallas_call(
        functools.partial(_kv_kernel, has_prev=False, has_std=False, last_only=False),
        grid=(1,),
        in_specs=[pl.BlockSpec((MS, K), lambda i: (MP // MS, 0)), _ss_spec(ss, MS, MP // MS),
                  pl.BlockSpec((None, W, K), lambda i: (layer, row0 // W, 0))],
        out_specs=pl.BlockSpec((MS * KVC, HEAD_DIM), lambda i: (0, 0)),
        out_shape=jax.ShapeDtypeStruct((MS * KVC, HEAD_DIM), F32),
        compiler_params=_params(("arbitrary",), blk),
        name=name,
    )(x, ss, w)


def _gmlp_kernel(x_ref, ss_ref, wu_ref, wv_ref, ws_ref, b_ref, *refs, chunk, has_prev, want_v):
    gm_ref = refs[has_prev]
    R = ws_ref.shape[1]
    tm = x_ref.shape[0]
    x = x_ref[...]
    inv = _inv_rms(ss_ref)
    u = jax.nn.gelu(inv * _dot_nt(x, wu_ref[...]))
    v = jax.nn.gelu(inv * _dot_nt(x, wv_ref[...]))
    if want_v:
        refs[has_prev + 1][...] = v
    i = lax.broadcasted_iota(jnp.int32, (R, R), 0)
    j = lax.broadcasted_iota(jnp.int32, (R, R), 1)
    w = jnp.where((j <= i) & (i // chunk == j // chunk), ws_ref[0], 0.0).astype(CD)
    vc = v.astype(CD)
    for c in range(tm // R):
        rows = slice(c * R, (c + 1) * R)
        mixed = _dot(w, vc[rows]) + b_ref[0]
        gm_ref[rows, :] = (u[rows] * mixed).astype(gm_ref.dtype)


def _gmlp(xn, w_uv, uv_row0, layer, ws, bs, n_rows, row_off, tm, R, chunk, prev, want_v):
    x, ss = xn
    M, K = x.shape
    cw = GM_WIDTH // GM_GROUPS
    assert n_rows % tm == 0 and row_off % tm == 0 and tm % R == 0 and cw % LANES == 0
    off = row_off // tm
    blk = 2 * (_nbytes((tm, K), CD) + 2 * _nbytes((K, cw), CD) + 3 * _nbytes((tm, cw), F32)) \
        + 8 * _nbytes((tm, cw), F32)
    assert uv_row0 % cw == 0
    u_spec = pl.BlockSpec((None, cw, K), lambda i, g: (layer, uv_row0 // cw + g, 0))
    v_spec = pl.BlockSpec((None, cw, K), lambda i, g: (layer, uv_row0 // cw + GM_GROUPS + g, 0))
    in_specs = [pl.BlockSpec((tm, K), lambda i, g: (i + off, 0)), _ss_spec(ss, tm, off), u_spec, v_spec,
                pl.BlockSpec((1, R, R), lambda i, g: (g, 0, 0)),
                pl.BlockSpec((1, R, 1), lambda i, g: (g, 0, 0))]
    args = [x, ss, w_uv, w_uv, ws, bs]
    aliases = {}
    if prev is not None:
        in_specs.append(pl.BlockSpec(memory_space=pl.ANY))
        args.append(prev)
        aliases = {6: 0}
    out_specs = [pl.BlockSpec((tm, cw), lambda i, g: (i + off, g))]
    out_shape = [jax.ShapeDtypeStruct((M, GM_WIDTH), CD)]
    if want_v:
        out_specs.append(pl.BlockSpec((tm, cw), lambda i, g: (i, g)))
        out_shape.append(jax.ShapeDtypeStruct((n_rows, GM_WIDTH), F32))
    return pl.pallas_call(
        functools.partial(_gmlp_kernel, chunk=chunk, has_prev=prev is not None, want_v=want_v),
        grid=(n_rows // tm, GM_GROUPS),
        in_specs=in_specs,
        out_specs=out_specs,
        out_shape=out_shape,
        input_output_aliases=aliases,
        compiler_params=_params(("parallel", "parallel"), blk),
        name="gmlp_branch",
    )(*args)


def _compress_rows(load_row, pe_l, w1_lo, w1_hi, w2, xlo, xhi, n_half):
    half = CMP_LEN // 2
    for l in range(half):
        xl = load_row(l)
        cols = slice(l * HEAD_DIM, (l + 1) * HEAD_DIM)
        xlo[:, cols] = (xl + pe_l(l)).astype(CD)
        xhi[:, cols] = (xl + pe_l(half + l)).astype(CD)
    hid = jax.nn.gelu(_dot(xlo[...], w1_lo) + pltpu.roll(_dot(xhi[...], w1_hi), n_half - 1, 0))
    return _dot(hid.astype(CD), w2)


def _compress_prompt_kernel(x_ref, pe_ref, w1_ref, w2_ref, o_ref, xlo, xhi, *, n_half):
    kh = (CMP_LEN // 2) * HEAD_DIM
    tok = _compress_rows(lambda l: x_ref[pl.ds(l, n_half, stride=CMP_STRIDE), :],
                         lambda l: pe_ref[0, l:l + 1, :], w1_ref[0, 0:kh, :], w1_ref[0, kh:2 * kh, :], w2_ref[0],
                         xlo, xhi, n_half)
    o_ref[0, 0] = tok.astype(o_ref.dtype)


def _compress_prompt(kv_c, pe, w1, w2):
    assert CMP_LEN == 2 * CMP_STRIDE and SEQ % (CMP_STRIDE * SUBLANES) == 0
    n_half = SEQ // CMP_STRIDE
    kh = (CMP_LEN // 2) * HEAD_DIM
    blk = 2 * _nbytes((SEQ, HEAD_DIM), F32) + 8 * _nbytes((n_half, HEAD_DIM), F32) \
        + 4 * _nbytes((2 * kh, HEAD_DIM), CD) + 2 * _nbytes((n_half, kh), CD)
    return pl.pallas_call(
        functools.partial(_compress_prompt_kernel, n_half=n_half),
        grid=(BATCH, KVC),
        in_specs=[pl.BlockSpec((SEQ, HEAD_DIM), lambda b, c: (b, c)),
                  pl.BlockSpec((1, CMP_LEN, HEAD_DIM), lambda b, c: (c // KV_HEADS, 0, 0)),
                  pl.BlockSpec((1, 2 * kh, HEAD_DIM), lambda b, c: (c // KV_HEADS, 0, 0)),
                  pl.BlockSpec((1, HEAD_DIM, HEAD_DIM), lambda b, c: (c // KV_HEADS, 0, 0))],
        out_specs=pl.BlockSpec((1, 1, n_half, HEAD_DIM), lambda b, c: (b, c, 0, 0)),
        out_shape=jax.ShapeDtypeStruct((BATCH, KVC, n_half, HEAD_DIM), CD),
        scratch_shapes=[pltpu.VMEM((n_half, kh), CD), pltpu.VMEM((n_half, kh), CD)],
        compiler_params=_params(("parallel", "parallel"), blk),
        name="nsa_compress_prompt",
    )(kv_c, pe, w1, w2)


def _compress_sample_kernel(pt_ref, *refs, n_half):
    page_refs = refs[:CMP_PAGES]
    pe_ref, w1_ref, w2_ref, o_ref, buf, xlo, xhi = refs[CMP_PAGES:]
    step = pl.program_id(1)
    page_rows = PAGE_SIZE * KVC
    kh = (CMP_LEN // 2) * HEAD_DIM
    for k, pr in enumerate(page_refs):
        start = pl.multiple_of((step * CMP_PAGES + k) * PAGE_SIZE, PAGE_SIZE)
        for c in range(KVC):
            buf[c, pl.ds(start, PAGE_SIZE), :] = _head_rows(pr, c, PAGE_SIZE)

    @pl.when(step == pl.num_programs(1) - 1)
    def _():
        def one_slot(c, carry):
            kv = c // KV_HEADS
            tok = _compress_rows(lambda l: buf[c, pl.ds(l, n_half, stride=CMP_STRIDE), :],
                                 lambda l: pe_ref[kv, pl.ds(l, 1), :], w1_ref[kv, pl.ds(0, kh), :],
                                 w1_ref[kv, pl.ds(kh, kh), :], w2_ref[kv], xlo, xhi, n_half)
            o_ref[c] = tok.astype(o_ref.dtype)
            return carry
        lax.fori_loop(0, KVC, one_slot, 0)


def _compress_sample(pool, layer, page_table, pe, w1, w2):
    nb, n_pages = page_table.shape
    assert n_pages % CMP_PAGES == 0 and PAST_LEN % (CMP_STRIDE * SUBLANES) == 0
    n_half = PAST_LEN // CMP_STRIDE
    page_rows = PAGE_SIZE * KVC
    kh = (CMP_LEN // 2) * HEAD_DIM
    blk = _nbytes((PAST_LEN * KVC, HEAD_DIM), F32) + 2 * CMP_PAGES * _nbytes((page_rows, HEAD_DIM), F32) \
        + 4 * _nbytes((2, 2 * kh, HEAD_DIM), CD) + 2 * _nbytes((KVC, n_half, HEAD_DIM), CD) \
        + 2 * _nbytes((n_half, kh), CD) + 10 * _nbytes((n_half, HEAD_DIM), F32)

    def page_spec(k):
        return pl.BlockSpec((None, None, page_rows, HEAD_DIM),
                            lambda b, s, pt: (layer, pt[b, s * CMP_PAGES + k], 0, 0))

    return pl.pallas_call(
        functools.partial(_compress_sample_kernel, n_half=n_half),
        grid_spec=pltpu.PrefetchScalarGridSpec(
            num_scalar_prefetch=1,
            grid=(nb, n_pages // CMP_PAGES),
            in_specs=[page_spec(k) for k in range(CMP_PAGES)]
            + [pl.BlockSpec((2, CMP_LEN, HEAD_DIM), lambda b, s, pt: (0, 0, 0)),
               pl.BlockSpec((2, 2 * kh, HEAD_DIM), lambda b, s, pt: (0, 0, 0)),
               pl.BlockSpec((2, HEAD_DIM, HEAD_DIM), lambda b, s, pt: (0, 0, 0))],
            out_specs=pl.BlockSpec((None, KVC, n_half, HEAD_DIM), lambda b, s, pt: (b, 0, 0, 0)),
            scratch_shapes=[pltpu.VMEM((KVC, PAST_LEN, HEAD_DIM), F32), pltpu.VMEM((n_half, kh), CD),
                            pltpu.VMEM((n_half, kh), CD)]),
        out_shape=jax.ShapeDtypeStruct((nb, KVC, n_half, HEAD_DIM), CD),
        compiler_params=_params(("parallel", "arbitrary"), blk),
        name="nsa_compress_sample",
    )(page_table, *([pool] * CMP_PAGES), pe, w1, w2)


def _masked_softmax(s, mask):
    s = jnp.where(mask, s, NEG_INF)
    m = jnp.max(s, axis=-1, keepdims=True)
    e = jnp.where(mask, jnp.exp(s - m), 0.0)
    d = jnp.sum(e, axis=-1, keepdims=True)
    return e * (1.0 / jnp.where(d > 0.0, d, 1.0))


def _attend_bias(qs, k, v, bias, tq):
    os = []
    for r in range(GROUP):
        t = _dot_nt(qs[r * tq:(r + 1) * tq], k) + bias
        e = jnp.exp2((t - jnp.max(t, axis=-1, keepdims=True)) * (SCALE * LOG2E))
        d = jnp.sum(e, axis=-1, keepdims=True)
        os.append(_dot(e.astype(CD), v) * (1.0 / d))
    return jnp.concatenate(os, axis=0)


def _group_sum_lanes(p, n_out):
    n = p.shape[1]
    src = lax.broadcasted_iota(jnp.int32, (n, n_out), 0) // PER_SEL
    dst = lax.broadcasted_iota(jnp.int32, (n, n_out), 1)
    gmat = (src == dst).astype(CD)
    h1 = p.astype(CD)
    r1 = p - h1.astype(F32)
    h2 = r1.astype(CD)
    h3 = (r1 - h2.astype(F32)).astype(CD)
    return _dot(h1, gmat) + _dot(h2, gmat) + _dot(h3, gmat)


def _stack_heads(q, col0):
    return jnp.concatenate([q[:, col0 + r * HEAD_DIM: col0 + (r + 1) * HEAD_DIM] for r in range(GROUP)], axis=0)


def _gate_sum(gates, col0, o_c, o_s, o_w, r, rows):
    return (gates[:, col0 + r:col0 + r + 1] * o_c[rows]
            + gates[:, col0 + GROUP + r:col0 + GROUP + r + 1] * o_s[rows]
            + gates[:, col0 + 2 * GROUP + r:col0 + 2 * GROUP + r + 1] * o_w[rows])


def _attn_prompt_kernel(q_ref, ks_ref, vs_ref, kw_ref, vw_ref, kc_ref, vc_ref, g_ref, o_ref, *, tq, T):
    i = pl.program_id(2)
    n_sel = T // SEL_LEN
    n_cmp = T // CMP_STRIDE

    qs = _stack_heads(q_ref[...], 0)
    qpos1 = i * tq + lax.broadcasted_iota(jnp.int32, (tq, 1), 0)
    qpos = jnp.concatenate([qpos1] * GROUP, axis=0)

    span = WINDOW + tq
    start = pl.multiple_of(jnp.maximum(i * tq - WINDOW, 0), tq)
    kposw = start + lax.broadcasted_iota(jnp.int32, (1, span), 1)
    bias_w = jnp.where((kposw <= qpos1) & (kposw > qpos1 - WINDOW), 0.0, NEG_INF)
    o_w = _attend_bias(qs, kw_ref[pl.ds(start, span), :], vw_ref[pl.ds(start, span), :], bias_w, tq)

    s = _dot_nt(qs, kc_ref[0, 0]) * SCALE
    blk_end = lax.broadcasted_iota(jnp.int32, (1, n_cmp), 1) * CMP_STRIDE + (CMP_LEN - 1)
    p = _masked_softmax(s, (blk_end <= qpos) & (blk_end < T))
    o_c = _dot(p.astype(CD), vc_ref[0, 0])
    psum = p[0:tq]
    for r in range(1, GROUP):
        psum = psum + p[r * tq:(r + 1) * tq]
    imp = _group_sum_lanes(psum, LANES)

    j = lax.broadcasted_iota(jnp.int32, (1, LANES), 1)
    cur = qpos1 // SEL_LEN
    forced = (j == 0) | (j == cur) | (j == cur - 1)
    score = jnp.where(forced, FORCE, imp)
    score = jnp.where(j <= cur, score, -1.0)
    sc = score.T[:n_sel]
    jrow = lax.broadcasted_iota(jnp.int32, (n_sel, 1), 0)
    rank = jnp.zeros((n_sel, tq), jnp.int32)
    for a in range(n_sel):
        sa = sc[a:a + 1, :]
        rank = rank + ((sa > sc) | ((sa == sc) & (a < jrow))).astype(jnp.int32)
    sel_t = ((rank < SEL_TOP) & (sc >= 0.0)).astype(F32)
    if n_sel < LANES:
        sel_t = jnp.concatenate([sel_t, jnp.zeros((LANES - n_sel, tq), F32)], axis=0)
    sel = sel_t.T.astype(CD)

    def sel_branch(nk):
        def run():
            expand = (lax.broadcasted_iota(jnp.int32, (LANES, nk), 1) // SEL_LEN
                      == lax.broadcasted_iota(jnp.int32, (LANES, nk), 0)).astype(CD)
            key_sel = _dot(sel, expand)
            kpos = lax.broadcasted_iota(jnp.int32, (1, nk), 1)
            bias = jnp.where((key_sel > 0.5) & (kpos <= qpos1), 0.0, NEG_INF)
            return _attend_bias(qs, ks_ref[0:nk, :], vs_ref[0:nk, :], bias, tq)
        return run

    n_ext = T // SEL_KEY_STEP
    o_s = lax.switch((i * tq) // SEL_KEY_STEP, [sel_branch(SEL_KEY_STEP * (k + 1)) for k in range(n_ext)])

    g = g_ref[...]
    for r in range(GROUP):
        o = _gate_sum(g, 0, o_c, o_s, o_w, r, slice(r * tq, (r + 1) * tq))
        o_ref[:, r * HEAD_DIM:(r + 1) * HEAD_DIM] = o.astype(o_ref.dtype)


def _attn_prompt(q, kv_sel, kv_win, kc, gates):
    T = SEQ
    tq = Q_TILE
    assert T % tq == 0 and WINDOW % tq == 0 and WINDOW + tq <= T and T // SEL_LEN <= LANES
    assert T // CMP_STRIDE == LANES and T % SEL_KEY_STEP == 0 and SEL_KEY_STEP % tq == 0
    nq = T // tq
    gw = GROUP * HEAD_DIM
    blk = 2 * 4 * _nbytes((T, HEAD_DIM), CD) + 8 * _nbytes((GROUP * tq, T), F32) + 4 * _nbytes((tq, gw), F32)
    k_spec = pl.BlockSpec((T, HEAD_DIM), lambda b, g, i: (b, g))
    v_spec = pl.BlockSpec((T, HEAD_DIM), lambda b, g, i: (b, KV_HEADS + g))
    return pl.pallas_call(
        functools.partial(_attn_prompt_kernel, tq=tq, T=T),
        grid=(BATCH, KV_HEADS, nq),
        in_specs=[pl.BlockSpec((tq, gw), lambda b, g, i: (b * nq + i, g)),
                  k_spec, v_spec, k_spec, v_spec,
                  pl.BlockSpec((1, 1, T // CMP_STRIDE, HEAD_DIM), lambda b, g, i: (b, g, 0, 0)),
                  pl.BlockSpec((1, 1, T // CMP_STRIDE, HEAD_DIM), lambda b, g, i: (b, KV_HEADS + g, 0, 0)),
                  pl.BlockSpec((tq, LANES), lambda b, g, i: (b * nq + i, g))],
        out_specs=pl.BlockSpec((tq, gw), lambda b, g, i: (b * nq + i, g)),
        out_shape=jax.ShapeDtypeStruct((q.shape[0], ATTN_W), CD),
        compiler_params=_params(("parallel", "parallel", "parallel"), blk),
        name="nsa_prompt_attention",
    )(q, kv_sel, kv_sel, kv_win, kv_win, kc, kc, gates)


def _sample_cmp_kernel(q_ref, kc_ref, vc_ref, oc_ref, sel_ref, *, n_cmp_valid):
    T = DEC_SEQ
    n_cmp = kc_ref.shape[2]
    qs = _stack_heads(q_ref[...], 0).astype(CD)
    qpos1 = PAST_LEN + lax.broadcasted_iota(jnp.int32, (T, 1), 0)
    qpos = jnp.concatenate([qpos1] * GROUP, axis=0)
    s = _dot_nt(qs, kc_ref[0, 0]) * SCALE
    n = lax.broadcasted_iota(jnp.int32, (1, n_cmp), 1)
    p = _masked_softmax(s, (n * CMP_STRIDE + (CMP_LEN - 1) <= qpos) & (n < n_cmp_valid))
    oc_ref[0, 0] = _dot(p.astype(CD), vc_ref[0, 0])
    psum = p[0:T]
    for r in range(1, GROUP):
        psum = psum + p[r * T:(r + 1) * T]
    imp = _group_sum_lanes(psum, LANES)

    j = lax.broadcasted_iota(jnp.int32, (1, LANES), 1)
    cur = qpos1 // SEL_LEN
    score = jnp.where((j == 0) | (j == cur - 1), FORCE, imp)
    rank = jnp.zeros((T, LANES), jnp.int32)
    for sh in range(1, LANES):
        other = pltpu.roll(score, sh, 1)
        oj = jnp.where(j >= sh, j - sh, j - sh + LANES)
        rank = rank + ((other > score) | ((other == score) & (oj < j))).astype(jnp.int32)
    sel_ref[0, 0] = (rank < SEL_TOP - 1).astype(F32)


def _sample_cmp(q_s, kc_s, n_cmp_valid):
    nb = DEC_BATCH
    gw = GROUP * HEAD_DIM
    n_cmp = kc_s.shape[2]
    blk = 16 * _nbytes((GROUP * DEC_SEQ, n_cmp), F32) + 4 * _nbytes((n_cmp, HEAD_DIM), CD)
    return pl.pallas_call(
        functools.partial(_sample_cmp_kernel, n_cmp_valid=n_cmp_valid),
        grid=(nb, KV_HEADS),
        in_specs=[pl.BlockSpec((DEC_SEQ, gw), lambda b, g: (b, g)),
                  pl.BlockSpec((1, 1, n_cmp, HEAD_DIM), lambda b, g: (b, g, 0, 0)),
                  pl.BlockSpec((1, 1, n_cmp, HEAD_DIM), lambda b, g: (b, KV_HEADS + g, 0, 0))],
        out_specs=[pl.BlockSpec((1, 1, GROUP * DEC_SEQ, HEAD_DIM), lambda b, g: (b, g, 0, 0)),
                   pl.BlockSpec((1, 1, DEC_SEQ, LANES), lambda b, g: (b, g, 0, 0))],
        out_shape=[jax.ShapeDtypeStruct((nb, KV_HEADS, GROUP * DEC_SEQ, HEAD_DIM), F32),
                   jax.ShapeDtypeStruct((nb, KV_HEADS, DEC_SEQ, LANES), F32)],
        compiler_params=_params(("parallel", "parallel"), blk),
        name="nsa_sample_cmp_attention",
    )(q_s, kc_s, kc_s)


def _put_rows_kernel(src_ref, dst_any, o_ref):
    o_ref[...] = src_ref[...].astype(o_ref.dtype)


def _put_rows(dst, src, row_off):
    n, w = src.shape
    assert row_off % n == 0 and n % 16 == 0
    return pl.pallas_call(
        _put_rows_kernel,
        grid=(1,),
        in_specs=[pl.BlockSpec((n, w), lambda i: (0, 0)), pl.BlockSpec(memory_space=pl.ANY)],
        out_specs=pl.BlockSpec((n, w), lambda i: (row_off // n, 0)),
        out_shape=jax.ShapeDtypeStruct(dst.shape, dst.dtype),
        input_output_aliases={1: 0},
        compiler_params=_params(("arbitrary",), 4 * _nbytes((n, w), F32)),
        name="put_sample_rows",
    )(src, dst)


def _pad_rows(x, rows):
    return jnp.concatenate([x, jnp.zeros((rows - x.shape[0], x.shape[1]), x.dtype)], axis=0)


def _head_rows(ref, slot, n):
    return ref[pl.ds(slot, n, stride=KVC), :]


def _sample_sel_kernel(pt_ref, q_ref, sel_ref, *refs):
    page_refs = refs[:SEL_PAGES]
    new_ref, o_ref, m_sc, l_sc, acc_sc = refs[SEL_PAGES:]
    T = DEC_SEQ
    R = GROUP * T
    step = pl.program_id(1)
    nk = SEL_PAGES * PAGE_SIZE

    @pl.when(step == 0)
    def _():
        m_sc[...] = jnp.full(m_sc.shape, NEG_INF, F32)
        l_sc[...] = jnp.zeros(l_sc.shape, F32)
        acc_sc[...] = jnp.zeros(acc_sc.shape, F32)

    def update(parts):
        m_prev, l_prev, acc_prev = m_sc[...], l_sc[...], acc_sc[...]
        ms, ls, accs = [], [], []
        for g, (s, mask, v) in enumerate(parts):
            rows = slice(g * R, (g + 1) * R)
            m_new = jnp.maximum(m_prev[rows], jnp.max(jnp.where(mask, s, NEG_INF), axis=-1, keepdims=True))
            alpha = jnp.exp(m_prev[rows] - m_new)
            e = jnp.where(mask, jnp.exp(s - m_new), 0.0)
            ms.append(m_new)
            ls.append(alpha * l_prev[rows] + jnp.sum(e, axis=-1, keepdims=True))
            accs.append(alpha * acc_prev[rows] + _dot(e.astype(CD), v))
        m_sc[...] = jnp.concatenate(ms, axis=0)
        l_sc[...] = jnp.concatenate(ls, axis=0)
        acc_sc[...] = jnp.concatenate(accs, axis=0)

    q = q_ref[...]
    blk_of_key = step * (nk // SEL_LEN) + lax.broadcasted_iota(jnp.int32, (LANES, nk), 1) // SEL_LEN
    expand = (blk_of_key == lax.broadcasted_iota(jnp.int32, (LANES, nk), 0)).astype(CD)
    key_sel = _dot(sel_ref[...].reshape(KV_HEADS * T, LANES).astype(CD), expand)
    parts = []
    for g in range(KV_HEADS):
        qg = _stack_heads(q, g * GROUP * HEAD_DIM).astype(CD)
        k = jnp.concatenate([_head_rows(pr, g, PAGE_SIZE) for pr in page_refs], axis=0).astype(CD)
        v = jnp.concatenate([_head_rows(pr, KV_HEADS + g, PAGE_SIZE) for pr in page_refs], axis=0).astype(CD)
        mask = jnp.concatenate([key_sel[g * T:(g + 1) * T]] * GROUP, axis=0) > 0.5
        parts.append((_dot_nt(qg, k) * SCALE, mask, v))
    update(parts)

    @pl.when(step == pl.num_programs(1) - 1)
    def _():
        t = jnp.concatenate([lax.broadcasted_iota(jnp.int32, (T, 1), 0)] * GROUP, axis=0)
        i = lax.broadcasted_iota(jnp.int32, (1, LANES), 1)
        mask = (i < T) & (i <= t)
        parts = []
        for g in range(KV_HEADS):
            qg = _stack_heads(q, g * GROUP * HEAD_DIM).astype(CD)
            k = _pad_rows(_head_rows(new_ref, g, T), LANES).astype(CD)
            v = _pad_rows(_head_rows(new_ref, KV_HEADS + g, T), LANES).astype(CD)
            parts.append((_dot_nt(qg, k) * SCALE, mask, v))
        update(parts)
        o_ref[...]= (acc_sc[...] * (1.0 / l_sc[...])).reshape(o_ref.shape)


def _sample_sel(q_s, sel, pool, layer, page_table, new_rows):
    nb, n_pages = page_table.shape
    assert n_pages % SEL_PAGES == 0 and (SEL_PAGES * PAGE_SIZE) % SEL_LEN == 0
    R = GROUP * DEC_SEQ
    page_rows = PAGE_SIZE * KVC
    blk = 2 * SEL_PAGES * _nbytes((page_rows, HEAD_DIM), F32) + 16 * _nbytes((R, SEL_PAGES * PAGE_SIZE), F32) \
        + 8 * _nbytes((SEL_PAGES * PAGE_SIZE, HEAD_DIM), F32)

    def page_spec(k):
        return pl.BlockSpec((None, None, page_rows, HEAD_DIM),
                            lambda b, s, pt: (layer, pt[b, s * SEL_PAGES + k], 0, 0))

    return pl.pallas_call(
        _sample_sel_kernel,
        grid_spec=pltpu.PrefetchScalarGridSpec(
            num_scalar_prefetch=1,
            grid=(nb, n_pages // SEL_PAGES),
            in_specs=[pl.BlockSpec((DEC_SEQ, ATTN_W), lambda b, s, pt: (b, 0)),
                      pl.BlockSpec((None, KV_HEADS, DEC_SEQ, LANES), lambda b, s, pt: (b, 0, 0, 0))]
            + [page_spec(k) for k in range(SEL_PAGES)]
            + [pl.BlockSpec((DEC_SEQ * KVC, HEAD_DIM), lambda b, s, pt: (b, 0))],
            out_specs=pl.BlockSpec((None, KV_HEADS, R, HEAD_DIM), lambda b, s, pt: (b, 0, 0, 0)),
            scratch_shapes=[pltpu.VMEM((KV_HEADS * R, 1), F32), pltpu.VMEM((KV_HEADS * R, 1), F32),
                            pltpu.VMEM((KV_HEADS * R, HEAD_DIM), F32)]),
        out_shape=jax.ShapeDtypeStruct((nb, KV_HEADS, R, HEAD_DIM), F32),
        compiler_params=_params(("parallel", "arbitrary"), blk),
        name="nsa_sample_sel_attention",
    )(page_table, q_s, sel, *([pool] * SEL_PAGES), new_rows)


def _sample_win_kernel(q_ref, buf_ref, new_ref, oc_ref, os_ref, g_ref, o_ref, *, wb):
    T = DEC_SEQ
    q = q_ref[...]
    gates = g_ref[...]
    t = jnp.concatenate([lax.broadcasted_iota(jnp.int32, (T, 1), 0)] * GROUP, axis=0)
    qpos = PAST_LEN + t
    i = lax.broadcasted_iota(jnp.int32, (1, wb + LANES), 1)
    kpos = PAST_LEN - wb + i
    mask = (i < wb + T) & (kpos <= qpos) & (kpos > qpos - WINDOW)
    for g in range(KV_HEADS):
        qg = _stack_heads(q, g * GROUP * HEAD_DIM).astype(CD)
        k = jnp.concatenate([_head_rows(buf_ref, g, wb), _pad_rows(_head_rows(new_ref, g, T), LANES)],
                            axis=0).astype(CD)
        v = jnp.concatenate([_head_rows(buf_ref, KV_HEADS + g, wb),
                             _pad_rows(_head_rows(new_ref, KV_HEADS + g, T), LANES)], axis=0).astype(CD)
        p = _masked_softmax(_dot_nt(qg, k) * SCALE, mask)
        o_w = _dot(p.astype(CD), v)
        o_c = oc_ref[g]
        o_s = os_ref[g]
        for r in range(GROUP):
            o = _gate_sum(gates, g * LANES, o_c, o_s, o_w, r, slice(r * T, (r + 1) * T))
            col = (g * GROUP + r) * HEAD_DIM
            o_ref[:, col:col + HEAD_DIM] = o


def _sample_win(q_s, win_buf, layer, new_rows, o_c, o_s, gates_s):
    nb = DEC_BATCH
    wb = win_buf.shape[2] // KVC
    R = GROUP * DEC_SEQ
    assert wb % LANES == 0
    blk = 2 * _nbytes((wb * KVC, HEAD_DIM), F32) + 8 * _nbytes((R, wb + LANES), F32) \
        + 8 * _nbytes((wb + LANES, HEAD_DIM), F32)
    return pl.pallas_call(
        functools.partial(_sample_win_kernel, wb=wb),
        grid=(nb,),
        in_specs=[pl.BlockSpec((DEC_SEQ, ATTN_W), lambda b: (b, 0)),
                  pl.BlockSpec((None, None, wb * KVC, HEAD_DIM), lambda b: (layer, b, 0, 0)),
                  pl.BlockSpec((DEC_SEQ * KVC, HEAD_DIM), lambda b: (b, 0)),
                  pl.BlockSpec((None, KV_HEADS, R, HEAD_DIM), lambda b: (b, 0, 0, 0)),
                  pl.BlockSpec((None, KV_HEADS, R, HEAD_DIM), lambda b: (b, 0, 0, 0)),
                  pl.BlockSpec((DEC_SEQ, KV_HEADS * LANES), lambda b: (b, 0))],
        out_specs=pl.BlockSpec((DEC_SEQ, ATTN_W), lambda b: (b, 0)),
        out_shape=jax.ShapeDtypeStruct((nb * DEC_SEQ, ATTN_W), F32),
        compiler_params=_params(("parallel",), blk),
        name="nsa_sample_win_attention",
    )(q_s, win_buf, new_rows, o_c, o_s, gates_s)


def _half_ffn(h, hn, w_gate, w_up, w_down, layer, g_next):
    act = _swiglu(hn, w_gate, w_up, layer)
    return _resid_proj(act, w_down, layer, h, 0.5, g_next, ROW_TILE // 2, COL_TILE // 2, "ffn_down")


W_IN_ROW_TILE = 512


def _w_in_cast_kernel(w_ref, o_ref):
    o_ref[...] = w_ref[0].astype(o_ref.dtype)


def _prep_w_in(w_in):
    o1 = ATTN_W
    o4 = o1 + 6 * KV_W
    n_gate = 3 * N_HEADS
    wt = jnp.swapaxes(w_in, 1, 2)
    L, n_in, D = wt.shape
    n_out = n_in - n_gate
    tr = W_IN_ROW_TILE
    assert o4 % tr == 0 and n_out % tr == 0 and n_gate % 16 == 0

    def src_row(j):
        return pl.multiple_of(j * tr + jnp.where(j * tr >= o4, n_gate, 0), 16)

    blk = 2 * (_nbytes((tr, D), F32) + _nbytes((tr, D), CD)) + 2 * _nbytes((tr, D), F32)
    main = pl.pallas_call(
        _w_in_cast_kernel,
        grid=(L, n_out // tr),
        in_specs=[pl.BlockSpec((pl.Element(1), pl.Element(tr), pl.Element(D)), lambda l, j: (l, src_row(j), 0))],
        out_specs=pl.BlockSpec((None, tr, D), lambda l, j: (l, j, 0)),
        out_shape=jax.ShapeDtypeStruct((L, n_out, D), CD),
        compiler_params=_params(("parallel", "parallel"), blk),
        name="w_in_cast",
    )(wt)
    wg = wt[:, o4:o4 + n_gate].reshape(L, 3, KV_HEADS, GROUP, D).transpose(0, 2, 1, 3, 4)
    wg = jnp.pad(wg.reshape(L, KV_HEADS, 3 * GROUP, D), ((0, 0), (0, 0), (0, LANES - 3 * GROUP), (0, 0)))
    return dict(main=main, gate=wg.reshape(L, KV_HEADS * LANES, D).astype(CD))


def _layer(h, hn, layer, W, small, caches, page_table, kv_bufs, g_after):
    (nm, pe, w1, w2, ws, bs, n2) = small
    cache_cmp, cache_sel, cache_win = caches
    MP = BATCH * SEQ
    MS = DEC_BATCH * DEC_SEQ
    n_pages = page_table.shape[1]
    assert n_pages * PAGE_SIZE == PAST_LEN and PAST_LEN // SEL_LEN == LANES and DEC_SEQ <= SEL_LEN
    assert DEC_SEQ == SUBLANES and MP % MS == 0

    h, n = _half_ffn(h, hn, W["g1"], W["u1"], W["d1"], layer, nm)

    wm = W["main"]
    kv0 = ATTN_W
    uv0 = kv0 + 6 * KV_W
    mg0 = uv0 + 2 * GM_WIDTH
    q = _proj(n, wm, 0, ATTN_W, layer, None, CD, "proj_q")
    gates = _proj(n, W["gate"], 0, KV_HEADS * LANES, layer, "sigmoid", F32, "proj_branch_gates")
    cmp_buf, kv_c = _kv_proj_prompt(n, wm, kv0, layer, kv_bufs[0], F32, False, "proj_kv_cmp")
    sel_buf, kv_sel = _kv_proj_prompt(n, wm, kv0 + 2 * KV_W, layer, kv_bufs[1], CD, False, "proj_kv_sel")
    win_buf, kv_win = _kv_proj_prompt(n, wm, kv0 + 4 * KV_W, layer, kv_bufs[2], CD, True, "proj_kv_win")
    new_s = [_kv_proj_sample(n, wm, kv0 + s * 2 * KV_W, layer, "proj_kv_sample") for s in range(3)]

    w1c = w1.astype(CD).reshape(2, CMP_LEN * HEAD_DIM, HEAD_DIM)
    w2c = w2.astype(CD)

    kc_p = _compress_prompt(kv_c, pe, w1c, w2c)
    o_p = _attn_prompt(q, kv_sel, kv_win, kc_p, gates)
    (gm,) = _gmlp(n, wm, uv0, layer, ws, bs[:, :, None], MP, 0, 1024, GM_CHUNK, GM_CHUNK, None, False)

    q_s = q[MP:].astype(F32)
    gates_s = gates[MP:]
    kc_s = _compress_sample(cache_cmp, layer, page_table, pe, w1c, w2c)
    n_cmp_valid = (PAST_LEN + DEC_SEQ - CMP_LEN) // CMP_STRIDE + 1
    oc_s, sel_s = _sample_cmp(q_s, kc_s, n_cmp_valid)
    os_s = _sample_sel(q_s, sel_s, cache_sel, layer, page_table, new_s[1])
    o_s = _sample_win(q_s, cache_win, layer, new_s[2], oc_s, os_s, gates_s)
    cl = min(DEC_SEQ, GM_CHUNK)
    ws_s = jnp.tile(ws[:, :cl, :cl], (1, MS // cl, MS // cl))
    bs_s = jnp.tile(bs[:, :cl], (1, MS // cl))[:, :, None]
    gm, v_s = _gmlp(n, wm, uv0, layer, ws_s, bs_s, MS, MP, MS, MS, cl, gm, True)

    o = _put_rows(o_p, o_s, MP)
    merged = _merge(n, o, gm, wm, mg0, W["wua"], W["wug"], layer)
    h, n = _resid_proj(merged, W["wo"], layer, h, 1.0, n2, ROW_TILE, COL_TILE // 2, "mixer_out")
    h, hn = _half_ffn(h, n, W["g2"], W["u2"], W["d2"], layer, g_after)
    return h, hn, (cmp_buf, sel_buf, win_buf), (new_s, v_s)


def kernel(x_prompt, x_sample, cache_cmp_kv, cache_sel_kv, cache_win_kv, page_table, norm_ffn1, ffn1_gate, ffn1_up, ffn1_down, norm_mix, w_in, cmp_pe, cmp_w1, cmp_w2, gm_ws, gm_bs, w_up_attn, w_up_gm, w_out, norm_ffn2, ffn2_gate, ffn2_up, ffn2_down, norm_final):
    MP = BATCH * SEQ
    MS = DEC_BATCH * DEC_SEQ
    kv_row = (2, KV_HEADS, HEAD_DIM)
    h = jnp.concatenate([x_prompt.reshape(MP, D_MODEL), x_sample.reshape(MS, D_MODEL)], axis=0)
    n_phys = cache_cmp_kv.shape[1]
    wb = cache_win_kv.shape[2]
    caches = (cache_cmp_kv.reshape(DEPTH, n_phys, PAGE_SIZE * KVC, HEAD_DIM),
              cache_sel_kv.reshape(DEPTH, n_phys, PAGE_SIZE * KVC, HEAD_DIM),
              cache_win_kv.reshape(DEPTH, DEC_BATCH, wb * KVC, HEAD_DIM))
    W = _prep_w_in(w_in)
    W.update(g1=ffn1_gate, u1=ffn1_up, d1=ffn1_down.astype(CD),
             g2=ffn2_gate, u2=ffn2_up, d2=ffn2_down.astype(CD),
             wua=w_up_attn, wug=w_up_gm, wo=w_out)
    kv_bufs = (None, None, None)
    sample_outs = []
    hn = _norm_prep(h, norm_ffn1[0])
    for l in range(DEPTH):
        small = (norm_mix[l], cmp_pe[l], cmp_w1[l], cmp_w2[l], gm_ws[l], gm_bs[l], norm_ffn2[l])
        g_after = norm_ffn1[l + 1] if l + 1 < DEPTH else None
        h, hn, kv_bufs, s_out = _layer(h, hn, l, W, small, caches, page_table, kv_bufs, g_after)
        sample_outs.append(s_out)
    y_prompt = _rmsnorm(h, norm_final, F32, rows=MP, row_off=0).reshape(BATCH, SEQ, D_MODEL)
    y_sample = _rmsnorm(h, norm_final, F32, rows=MS, row_off=MP, tile=MS).reshape(DEC_BATCH, DEC_SEQ, D_MODEL)

    cmp_buf, sel_buf, win_buf = kv_bufs
    wk = min(WINDOW, SEQ)
    new = [jnp.stack([sample_outs[l][0][s] for l in range(DEPTH)]).reshape((DEPTH, DEC_BATCH, DEC_SEQ) + kv_row)
           for s in range(3)]
    win_all = jnp.concatenate([cache_win_kv, new[2]], axis=2)
    win_s = win_all[:, :, -min(WINDOW, win_all.shape[2]):]
    gm_v = jnp.stack([sample_outs[l][1] for l in range(DEPTH)]).reshape(DEPTH, DEC_BATCH, DEC_SEQ, GM_WIDTH)
    return (y_prompt, y_sample,
            cmp_buf.reshape((DEPTH, BATCH, SEQ) + kv_row), new[0],
            sel_buf.reshape((DEPTH, BATCH, SEQ) + kv_row), new[1],
            win_buf.reshape((DEPTH, BATCH, wk) + kv_row), win_s, gm_v)
```
